```python
import jax, jax.numpy as jnp
from jax import lax
import numpy as np

D_MODEL = 2048
BATCH = 8
SEQ = 2048
DEPTH = 1
DEC_BATCH = 128
DEC_SEQ = 8
PAST_LEN = 8192
PAGE_SIZE = 128

CHUNK = 128
A_WIDTH = 1024
A_GROUPS = 8
A_GROUP_DIM = A_WIDTH // A_GROUPS
HEAD_DIM = 64
N_HEADS = 16
N_KV_HEADS = 4
Q_PER_KV = N_HEADS // N_KV_HEADS
B_WIDTH = N_HEADS * HEAD_DIM
KV_WIDTH = N_KV_HEADS * HEAD_DIM
WINDOW = 128
ROPE_THETA = 10000.0
EPS = 1e-6
IN_COLS = 3 * A_WIDTH + 2 * B_WIDTH + 2 * KV_WIDTH + 2 * D_MODEL

kernel_name = "hybrid_gmlp_swa_sink_decode_step"


def _rmsnorm(x, g):
    xf = x.astype(jnp.float32)
    y = xf * lax.rsqrt(jnp.mean(xf * xf, axis=-1, keepdims=True) + EPS)
    return (y * g.astype(jnp.float32)).astype(x.dtype)


def _layernorm(x, g, b):
    xf = x.astype(jnp.float32)
    mu = jnp.mean(xf, axis=-1, keepdims=True)
    xc = xf - mu
    y = xc * lax.rsqrt(jnp.mean(xc * xc, axis=-1, keepdims=True) + EPS)
    return (y * g.astype(jnp.float32) + b.astype(jnp.float32)).astype(x.dtype)


def _rope(x, pos):
    inv = ROPE_THETA ** (-jnp.arange(0, HEAD_DIM, 2, dtype=jnp.float32) / HEAD_DIM)
    ang = pos.astype(jnp.float32)[:, None] * inv[None, :]
    cos = jnp.cos(ang)[:, None, :]
    sin = jnp.sin(ang)[:, None, :]
    xf = x.astype(jnp.float32)
    x1, x2 = xf[..., : HEAD_DIM // 2], xf[..., HEAD_DIM // 2:]
    out = jnp.concatenate([x1 * cos - x2 * sin, x2 * cos + x1 * sin], axis=-1)
    return out.astype(x.dtype)


def _project_in(x, pos, g_pre, w_in, ln_v_g, ln_v_b):
    bsz, t, _ = x.shape
    z = _rmsnorm(x, g_pre) @ w_in
    sizes = (A_WIDTH, A_WIDTH, A_WIDTH, B_WIDTH, KV_WIDTH, KV_WIDTH, B_WIDTH, D_MODEL, D_MODEL)
    idx = [int(i) for i in np.cumsum(sizes)[:-1]]
    u, v, z_a, q, k, v_att, z_b, gate_a, gate_b = jnp.split(z, idx, axis=-1)
    u = jax.nn.gelu(u, approximate=False)
    v = _layernorm(jax.nn.gelu(v, approximate=False), ln_v_g, ln_v_b)
    q = _rope(q.reshape(bsz, t, N_HEADS, HEAD_DIM), pos)
    k = _rope(k.reshape(bsz, t, N_KV_HEADS, HEAD_DIM), pos)
    v_att = v_att.reshape(bsz, t, N_KV_HEADS, HEAD_DIM)
    return u, v, z_a, q, k, v_att, z_b, gate_a, gate_b


def _spatial_gate_branch(u, v, z_a, w_s, b_s):
    bsz, t, _ = v.shape
    n = -(-t // CHUNK)
    pad = n * CHUNK - t
    vp = jnp.pad(v, ((0, 0), (0, pad), (0, 0))).reshape(bsz, n, CHUNK, A_GROUPS, A_GROUP_DIM)
    mask = jnp.tril(jnp.ones((CHUNK, CHUNK), dtype=bool))
    w = jnp.where(mask[None], w_s, jnp.zeros_like(w_s))
    s = jnp.einsum('gts,bnsgc->bntgc', w, vp) + b_s.T[None, None, :, :, None]
    s = s.reshape(bsz, n * CHUNK, A_WIDTH)[:, :t]
    return (u * s) * jax.nn.silu(z_a)


def _sink_attention(q, k, v, mask, sinks):
    lead = q.shape[:-3]
    tq = q.shape[-3]
    qg = q.reshape(*lead, tq, N_KV_HEADS, Q_PER_KV, HEAD_DIM)
    s = jnp.einsum('...qkgd,...skd->...kgqs', qg, k).astype(jnp.float32) * (HEAD_DIM ** -0.5)
    s = jnp.where(mask[..., None, None, :, :], s, -jnp.inf)
    sink = sinks.astype(jnp.float32).reshape(N_KV_HEADS, Q_PER_KV, 1, 1)
    m = jnp.maximum(jnp.max(s, axis=-1, keepdims=True), sink)
    p = jnp.exp(s - m)
    denom = jnp.sum(p, axis=-1, keepdims=True) + jnp.exp(sink - m)
    p = (p / denom).astype(v.dtype)
    o = jnp.einsum('...kgqs,...skd->...qkgd', p, v)
    return o.reshape(*lead, tq, B_WIDTH)


def _prompt_window_attention(q, k, v, sinks):
    bsz, t = q.shape[:2]
    nb = t // WINDOW

    def blocks(a):
        ab = a.reshape(bsz, nb, WINDOW, N_KV_HEADS, HEAD_DIM)
        prev = jnp.concatenate([jnp.zeros_like(ab[:, :1]), ab[:, :-1]], axis=1)
        return jnp.concatenate([prev, ab], axis=2)

    qb = q.reshape(bsz, nb, WINDOW, N_HEADS, HEAD_DIM)
    qpos = jnp.arange(nb)[:, None] * WINDOW + jnp.arange(WINDOW)[None, :]
    kpos = jnp.arange(nb)[:, None] * WINDOW + jnp.arange(-WINDOW, WINDOW)[None, :]
    d = qpos[:, :, None] - kpos[:, None, :]
    mask = (d >= 0) & (d < WINDOW) & (kpos[:, None, :] >= 0)
    o = _sink_attention(qb, blocks(k), blocks(v), mask, sinks)
    return o.reshape(bsz, t, B_WIDTH)


def _sample_window_attention(q, k, v, cache_k, cache_v, sinks):
    bsz, t = q.shape[:2]
    kk = jnp.concatenate([cache_k, k], axis=1)
    vv = jnp.concatenate([cache_v, v], axis=1)
    qpos = PAST_LEN + jnp.arange(t)
    kpos = PAST_LEN - WINDOW + jnp.arange(WINDOW + t)
    d = qpos[:, None] - kpos[None, :]
    mask = (d >= 0) & (d < WINDOW)
    o = _sink_attention(q, kk, vv, mask, sinks)
    return o, kk[:, -WINDOW:], vv[:, -WINDOW:]


def _merge_out(x, a_out, b_out, gate_a, gate_b, w_proj_a, w_proj_b, w_out, g_post):
    merged = jax.nn.sigmoid(gate_a) * (a_out @ w_proj_a) + jax.nn.sigmoid(gate_b) * (b_out @ w_proj_b)
    return x + _rmsnorm(merged @ w_out, g_post)


def setup_inputs(seed: int = 0) -> dict:
    key = jax.random.key(seed)
    ks = jax.random.split(key, 16)
    f32 = jnp.float32
    nrm = lambda k, shape, scale: jax.random.normal(k, shape, f32) * scale
    return {
        "x_prompt": nrm(ks[0], (BATCH, SEQ, D_MODEL), 1.0),
        "x_sample": nrm(ks[1], (DEC_BATCH, DEC_SEQ, D_MODEL), 1.0),
        "cache_k_win": nrm(ks[2], (DEPTH, DEC_BATCH, WINDOW, N_KV_HEADS, HEAD_DIM), 1.0),
        "cache_v_win": nrm(ks[3], (DEPTH, DEC_BATCH, WINDOW, N_KV_HEADS, HEAD_DIM), 1.0),
        "g_pre": 1.0 + nrm(ks[4], (DEPTH, D_MODEL), 0.02),
        "w_in": nrm(ks[5], (DEPTH, D_MODEL, IN_COLS), D_MODEL ** -0.5),
        "ln_v_g": 1.0 + nrm(ks[6], (DEPTH, A_WIDTH), 0.02),
        "ln_v_b": nrm(ks[7], (DEPTH, A_WIDTH), 0.02),
        "w_spatial": nrm(ks[8], (DEPTH, A_GROUPS, CHUNK, CHUNK), CHUNK ** -0.5),
        "b_spatial": 1.0 + nrm(ks[9], (DEPTH, A_GROUPS, CHUNK), 0.02),
        "sinks": nrm(ks[10], (DEPTH, N_HEADS), 0.5),
        "w_proj_a": nrm(ks[11], (DEPTH, A_WIDTH, D_MODEL), A_WIDTH ** -0.5),
        "w_proj_b": nrm(ks[12], (DEPTH, B_WIDTH, D_MODEL), B_WIDTH ** -0.5),
        "w_out": nrm(ks[13], (DEPTH, D_MODEL, D_MODEL), D_MODEL ** -0.5),
        "g_post": 1.0 + nrm(ks[14], (DEPTH, D_MODEL), 0.02),
    }


def reference(x_prompt, x_sample, cache_k_win, cache_v_win, g_pre, w_in, ln_v_g, ln_v_b,
              w_spatial, b_spatial, sinks, w_proj_a, w_proj_b, w_out, g_post):
    pos_p = jnp.arange(SEQ)
    pos_s = PAST_LEN + jnp.arange(DEC_SEQ)
    xp, xs = x_prompt, x_sample
    kp_l, vp_l, ks_l, vs_l, cv_l = [], [], [], [], []
    for l in range(DEPTH):
        u, v, z_a, q, k, va, z_b, ga, gb = _project_in(xp, pos_p, g_pre[l], w_in[l], ln_v_g[l], ln_v_b[l])
        a_out = _spatial_gate_branch(u, v, z_a, w_spatial[l], b_spatial[l])
        b_out = _prompt_window_attention(q, k, va, sinks[l]) * jax.nn.silu(z_b)
        xp = _merge_out(xp, a_out, b_out, ga, gb, w_proj_a[l], w_proj_b[l], w_out[l], g_post[l])
        kp_l.append(k[:, -WINDOW:])
        vp_l.append(va[:, -WINDOW:])
        u, v, z_a, q, k, va, z_b, ga, gb = _project_in(xs, pos_s, g_pre[l], w_in[l], ln_v_g[l], ln_v_b[l])
        a_out = _spatial_gate_branch(u, v, z_a, w_spatial[l], b_spatial[l])
        o, k_new_win, v_new_win = _sample_window_attention(q, k, va, cache_k_win[l], cache_v_win[l], sinks[l])
        b_out = o * jax.nn.silu(z_b)
        xs = _merge_out(xs, a_out, b_out, ga, gb, w_proj_a[l], w_proj_b[l], w_out[l], g_post[l])
        ks_l.append(k_new_win)
        vs_l.append(v_new_win)
        cv_l.append(v)
    return (xp, xs, jnp.stack(kp_l), jnp.stack(vp_l), jnp.stack(ks_l), jnp.stack(vs_l), jnp.stack(cv_l))
```

```python
import functools

import numpy as np
import jax
import jax.numpy as jnp
from jax import lax
from jax.experimental import pallas as pl
from jax.experimental.pallas import tpu as pltpu

D_MODEL = 2048
SEQ = 2048
DEC_SEQ = 8
PAST_LEN = 8192
CHUNK = 128
A_WIDTH = 1024
A_GROUPS = 8
HEAD_DIM = 64
N_HEADS = 16
N_KV_HEADS = 4
B_WIDTH = N_HEADS * HEAD_DIM
KV_WIDTH = N_KV_HEADS * HEAD_DIM
WINDOW = 128
ROPE_THETA = 10000.0
EPS = 1e-6
IN_COLS = 3 * A_WIDTH + 2 * B_WIDTH + 2 * KV_WIDTH + 2 * D_MODEL

LANES = 128
BF16 = jnp.bfloat16
F32 = jnp.float32

COL_TILE = 512
SEC_GATE_END = 8
SEC_GELU_END = 12
SEC_ZA_END = 14
SEC_Q_END = 16
SEC_ZB_END = 18
N_COL_TILES = 19
COL_U, COL_V, COL_ZA, COL_Q, COL_ZB = 4, 5, 6, 7, 8
COL_KV = 18

HEAD_PERM = tuple(8 * c + 4 * half + jq for c in range(2) for jq in range(4) for half in range(2))

IN_PROJ_ROWS = 1024
MERGE_ROWS = 512
VMEM_LIMIT = 56 * 1024 * 1024


def _gelu(x):
    return 0.5 * x * (1.0 + lax.erf(x * np.float32(1.0 / np.sqrt(2.0))))


def _sigmoid(x):
    return 1.0 / (1.0 + jnp.exp(-x))


def _rope(x, cos, sin_signed):
    width = x.shape[1]
    lane = lax.broadcasted_iota(jnp.int32, x.shape, 1)
    first_half = (lane & (HEAD_DIM - 1)) < (HEAD_DIM // 2)
    partner = jnp.where(first_half,
                        pltpu.roll(x, width - HEAD_DIM // 2, axis=1),
                        pltpu.roll(x, HEAD_DIM // 2, axis=1))
    reps = width // LANES
    cos_w = jnp.concatenate([cos] * reps, axis=1) if reps > 1 else cos
    sin_w = jnp.concatenate([sin_signed] * reps, axis=1) if reps > 1 else sin_signed
    return x * cos_w + partner * sin_w


def _in_proj_kernel(x_ref, g_ref, w_ref, cos_ref, sin_ref, z_ref, kv32_ref, h_scr):
    j = pl.program_id(1)

    @pl.when(j == 0)
    def _():
        x = x_ref[...]
        ms = jnp.mean(x * x, axis=-1, keepdims=True)
        h_scr[...] = (x * lax.rsqrt(ms + EPS) * g_ref[...]).astype(BF16)

    acc = jnp.dot(h_scr[...], w_ref[...], preferred_element_type=F32)

    @pl.when(j < SEC_GATE_END)
    def _():
        z_ref[...] = _sigmoid(acc).astype(BF16)

    @pl.when((j >= SEC_GATE_END) & (j < SEC_GELU_END))
    def _():
        z_ref[...] = _gelu(acc).astype(BF16)

    @pl.when(((j >= SEC_GELU_END) & (j < SEC_ZA_END)) | ((j >= SEC_Q_END) & (j < SEC_ZB_END)))
    def _():
        z_ref[...] = (acc * _sigmoid(acc)).astype(BF16)

    @pl.when((j >= SEC_ZA_END) & (j < SEC_Q_END))
    def _():
        q = _rope(acc, cos_ref[...], sin_ref[...]) * np.float32(HEAD_DIM ** -0.5)
        z_ref[...] = q.astype(BF16)

    @pl.when(j == N_COL_TILES - 1)
    def _():
        k = _rope(acc[:, :KV_WIDTH], cos_ref[...], sin_ref[...])
        v = acc[:, KV_WIDTH:]
        kv32_ref[:, :KV_WIDTH] = k
        kv32_ref[:, KV_WIDTH:] = v
        z_ref[:, :KV_WIDTH] = k.astype(BF16)
        z_ref[:, KV_WIDTH:] = v.astype(BF16)


def _in_proj(x2d, g_pre, w_in_p, cos_t, sin_t, table_blocks):
    n = x2d.shape[0]
    tm = IN_PROJ_ROWS
    grid = (n // tm, N_COL_TILES)
    return pl.pallas_call(
        _in_proj_kernel,
        grid=grid,
        in_specs=[
            pl.BlockSpec((tm, D_MODEL), lambda i, j: (i, 0)),
            pl.BlockSpec((1, D_MODEL), lambda i, j: (0, 0)),
            pl.BlockSpec((D_MODEL, COL_TILE), lambda i, j: (0, j)),
            pl.BlockSpec((tm, LANES), lambda i, j: (i % table_blocks, 0)),
            pl.BlockSpec((tm, LANES), lambda i, j: (i % table_blocks, 0)),
        ],
        out_specs=[
            pl.BlockSpec((tm, COL_TILE), lambda i, j: (i, j)),
            pl.BlockSpec((tm, COL_TILE), lambda i, j: (i, 0)),
        ],
        out_shape=[
            jax.ShapeDtypeStruct((n, IN_COLS), BF16),
            jax.ShapeDtypeStruct((n, 2 * KV_WIDTH), F32),
        ],
        scratch_shapes=[pltpu.VMEM((tm, D_MODEL), BF16)],
        compiler_params=pltpu.CompilerParams(
            dimension_semantics=("parallel", "arbitrary"), vmem_limit_bytes=VMEM_LIMIT),
        name="in_proj",
    )(x2d, g_pre, w_in_p, cos_t, sin_t)


def _spatial_gate(u_ref, v_ref, za_ref, lng_ref, lnb_ref, ws_ref, bs_ref, w_mask, a_ref, vn_ref):
    vg = v_ref[...].astype(F32)
    mu = jnp.mean(vg, axis=-1, keepdims=True)
    xc = vg - mu
    var = jnp.mean(xc * xc, axis=-1, keepdims=True)
    vn = xc * lax.rsqrt(var + EPS) * lng_ref[...] + lnb_ref[...]
    if vn_ref is not None:
        vn_ref[...] = vn
    vn_b = vn.astype(BF16)
    for g in range(A_GROUPS):
        cols = slice(g * LANES, (g + 1) * LANES)
        w = jnp.where(w_mask, ws_ref[g], 0.0).astype(BF16)
        s = jnp.dot(w, vn_b[:, cols], preferred_element_type=F32) + bs_ref[:, cols]
        a = (u_ref[:, cols].astype(F32) * s) * za_ref[:, cols].astype(F32)
        a_ref[:, cols] = a.astype(BF16)


def _attention(q_ref, zb_ref, keys, values, mask, sinks_ref, b_ref):
    lane = lax.broadcasted_iota(jnp.int32, (q_ref.shape[0], LANES), 1)
    low_half = lane < HEAD_DIM
    for m in range(N_HEADS // 2):
        cols = slice(m * LANES, (m + 1) * LANES)
        c = m // 4
        k_c = keys[:, c * LANES:(c + 1) * LANES]
        v_c = values[:, c * LANES:(c + 1) * LANES]
        q_m = q_ref[:, cols]
        outs = []
        for half in range(2):
            sink = sinks_ref[HEAD_PERM[2 * m + half]]
            q_h = jnp.where(low_half if half == 0 else ~low_half, q_m, jnp.zeros_like(q_m))
            s = lax.dot_general(q_h, k_c, (((1,), (1,)), ((), ())), preferred_element_type=F32)
            s = jnp.where(mask, s, -jnp.inf)
            mx = jnp.maximum(jnp.max(s, axis=-1, keepdims=True), sink)
            p = jnp.exp(s - mx)
            denom = jnp.sum(p, axis=-1, keepdims=True) + jnp.exp(sink - mx)
            o = jnp.dot(p.astype(BF16), v_c, preferred_element_type=F32)
            outs.append(o * (1.0 / denom))
        o_m = jnp.where(low_half, outs[0], outs[1])
        b_ref[:, cols] = (o_m * zb_ref[:, cols].astype(F32)).astype(BF16)


def _prompt_mixers_kernel(u_ref, v_ref, za_ref, q_ref, zb_ref, kv_ref, kvp_ref, lng_ref, lnb_ref,
                          ws_ref, bs_ref, sinks_ref, a_ref, b_ref):
    blk = pl.program_id(0) % (SEQ // WINDOW)
    row = lax.broadcasted_iota(jnp.int32, (CHUNK, CHUNK), 0)
    col = lax.broadcasted_iota(jnp.int32, (CHUNK, CHUNK), 1)
    _spatial_gate(u_ref, v_ref, za_ref, lng_ref, lnb_ref, ws_ref, bs_ref, col <= row, a_ref, None)

    kv = jnp.concatenate([kvp_ref[...], kv_ref[...]], axis=0)
    t = lax.broadcasted_iota(jnp.int32, (WINDOW, 2 * WINDOW), 0)
    jj = lax.broadcasted_iota(jnp.int32, (WINDOW, 2 * WINDOW), 1)
    mask = (jj > t) & (jj <= t + WINDOW) & ((jj >= WINDOW) | (blk > 0))
    _attention(q_ref, zb_ref, kv[:, :KV_WIDTH], kv[:, KV_WIDTH:], mask, sinks_ref, b_ref)


def _prompt_mixers(z, ln_g, ln_b, w_s, b_s_rows, sinks):
    n = z.shape[0]
    nblk = n // CHUNK
    wide = lambda cb: pl.BlockSpec((CHUNK, A_WIDTH), lambda r, cb=cb: (r, cb))
    const2 = lambda shape: pl.BlockSpec(shape, lambda r: (0, 0))
    return pl.pallas_call(
        _prompt_mixers_kernel,
        grid=(nblk,),
        in_specs=[
            wide(COL_U), wide(COL_V), wide(COL_ZA), wide(COL_Q), wide(COL_ZB),
            pl.BlockSpec((CHUNK, 2 * KV_WIDTH), lambda r: (r, COL_KV)),
            pl.BlockSpec((CHUNK, 2 * KV_WIDTH), lambda r: (jnp.maximum(r - 1, 0), COL_KV)),
            const2((1, A_WIDTH)), const2((1, A_WIDTH)),
            pl.BlockSpec((A_GROUPS, CHUNK, CHUNK), lambda r: (0, 0, 0)),
            const2((CHUNK, A_WIDTH)),
            pl.BlockSpec(memory_space=pltpu.SMEM),
        ],
        out_specs=[
            pl.BlockSpec((CHUNK, A_WIDTH), lambda r: (r, 0)),
            pl.BlockSpec((CHUNK, B_WIDTH), lambda r: (r, 0)),
        ],
        out_shape=[
            jax.ShapeDtypeStruct((n, A_WIDTH), BF16),
            jax.ShapeDtypeStruct((n, B_WIDTH), BF16),
        ],
        compiler_params=pltpu.CompilerParams(
            dimension_semantics=("parallel",), vmem_limit_bytes=VMEM_LIMIT),
        name="prompt_mixers",
    )(z, z, z, z, z, z, z, ln_g, ln_b, w_s, b_s_rows, sinks)


SEQS_PER_STEP = CHUNK // DEC_SEQ


def _sample_mixers_kernel(u_ref, v_ref, za_ref, q_ref, zb_ref, kv_ref, kv32_ref, ck_ref, cv_ref,
                          lng_ref, lnb_ref, ws_ref, bs_ref, sinks_ref,
                          a_ref, b_ref, vn_ref, kwin_ref, vwin_ref):
    row = lax.broadcasted_iota(jnp.int32, (CHUNK, CHUNK), 0)
    col = lax.broadcasted_iota(jnp.int32, (CHUNK, CHUNK), 1)
    w_mask = ((row >> 3) == (col >> 3)) & (col <= row)
    _spatial_gate(u_ref, v_ref, za_ref, lng_ref, lnb_ref, ws_ref, bs_ref, w_mask, a_ref, vn_ref)

    n_cache = SEQS_PER_STEP * WINDOW
    ck = ck_ref[...].reshape(n_cache, KV_WIDTH)
    cv = cv_ref[...].reshape(n_cache, KV_WIDTH)
    kv_new = kv_ref[...]
    keys = jnp.concatenate([ck.astype(BF16), kv_new[:, :KV_WIDTH]], axis=0)
    values = jnp.concatenate([cv.astype(BF16), kv_new[:, KV_WIDTH:]], axis=0)
    n_keys = n_cache + CHUNK
    r = lax.broadcasted_iota(jnp.int32, (CHUNK, n_keys), 0)
    kc = lax.broadcasted_iota(jnp.int32, (CHUNK, n_keys), 1)
    q_seq, q_t = r >> 3, r & (DEC_SEQ - 1)
    in_cache = kc < n_cache
    k_seq = jnp.where(in_cache, kc >> 7, (kc - n_cache) >> 3)
    k_off = jnp.where(in_cache, kc & (WINDOW - 1), (kc - n_cache) & (DEC_SEQ - 1))
    mask = (q_seq == k_seq) & ((in_cache & (k_off > q_t)) | (~in_cache & (k_off <= q_t)))
    _attention(q_ref, zb_ref, keys, values, mask, sinks_ref, b_ref)

    kv32 = kv32_ref[...].reshape(SEQS_PER_STEP, DEC_SEQ, 2 * KV_WIDTH)
    kwin_ref[:, :WINDOW - DEC_SEQ, :] = ck_ref[:, DEC_SEQ:, :]
    kwin_ref[:, WINDOW - DEC_SEQ:, :] = kv32[:, :, :KV_WIDTH]
    vwin_ref[:, :WINDOW - DEC_SEQ, :] = cv_ref[:, DEC_SEQ:, :]
    vwin_ref[:, WINDOW - DEC_SEQ:, :] = kv32[:, :, KV_WIDTH:]


def _sample_mixers(z, kv32, cache_k, cache_v, ln_g, ln_b, w_s_tiled, b_s_rows, sinks):
    n = z.shape[0]
    nblk = n // CHUNK
    n_seq = cache_k.shape[0]
    wide = lambda cb: pl.BlockSpec((CHUNK, A_WIDTH), lambda r, cb=cb: (r, cb))
    const2 = lambda shape: pl.BlockSpec(shape, lambda r: (0, 0))
    win = pl.BlockSpec((SEQS_PER_STEP, WINDOW, KV_WIDTH), lambda r: (r, 0, 0))
    return pl.pallas_call(
        _sample_mixers_kernel,
        grid=(nblk,),
        in_specs=[
            wide(COL_U), wide(COL_V), wide(COL_ZA), wide(COL_Q), wide(COL_ZB),
            pl.BlockSpec((CHUNK, 2 * KV_WIDTH), lambda r: (r, COL_KV)),
            pl.BlockSpec((CHUNK, 2 * KV_WIDTH), lambda r: (r, 0)),
            win, win,
            const2((1, A_WIDTH)), const2((1, A_WIDTH)),
            pl.BlockSpec((A_GROUPS, CHUNK, CHUNK), lambda r: (0, 0, 0)),
            const2((CHUNK, A_WIDTH)),
            pl.BlockSpec(memory_space=pltpu.SMEM),
        ],
        out_specs=[
            pl.BlockSpec((CHUNK, A_WIDTH), lambda r: (r, 0)),
            pl.BlockSpec((CHUNK, B_WIDTH), lambda r: (r, 0)),
            pl.BlockSpec((CHUNK, A_WIDTH), lambda r: (r, 0)),
            win, win,
        ],
        out_shape=[
            jax.ShapeDtypeStruct((n, A_WIDTH), BF16),
            jax.ShapeDtypeStruct((n, B_WIDTH), BF16),
            jax.ShapeDtypeStruct((n, A_WIDTH), F32),
            jax.ShapeDtypeStruct((n_seq, WINDOW, KV_WIDTH), F32),
            jax.ShapeDtypeStruct((n_seq, WINDOW, KV_WIDTH), F32),
        ],
        compiler_params=pltpu.CompilerParams(
            dimension_semantics=("parallel",), vmem_limit_bytes=VMEM_LIMIT),
        name="sample_mixers",
    )(z, z, z, z, z, z, kv32, cache_k, cache_v, ln_g, ln_b, w_s_tiled, b_s_rows, sinks)


def _merge_out_kernel(x_ref, a_ref, b_ref, ga_ref, gb_ref, wa_ref, wb_ref, wo_ref, g_ref, y_ref):
    pa = jnp.dot(a_ref[...], wa_ref[...], preferred_element_type=F32)
    pb = jnp.dot(b_ref[...], wb_ref[...], preferred_element_type=F32)
    merged = ga_ref[...].astype(F32) * pa + gb_ref[...].astype(F32) * pb
    out = jnp.dot(merged.astype(BF16), wo_ref[...], preferred_element_type=F32)
    ms = jnp.mean(out * out, axis=-1, keepdims=True)
    y_ref[...] = x_ref[...] + out * lax.rsqrt(ms + EPS) * g_ref[...]


def _merge_out(x2d, a, b, z, w_a, w_b, w_o, g_post):
    n = x2d.shape[0]
    tm = MERGE_ROWS
    resident = lambda shape: pl.BlockSpec(shape, lambda i: (0, 0), pipeline_mode=pl.Buffered(1))
    return pl.pallas_call(
        _merge_out_kernel,
        grid=(n // tm,),
        in_specs=[
            pl.BlockSpec((tm, D_MODEL), lambda i: (i, 0)),
            pl.BlockSpec((tm, A_WIDTH), lambda i: (i, 0)),
            pl.BlockSpec((tm, B_WIDTH), lambda i: (i, 0)),
            pl.BlockSpec((tm, D_MODEL), lambda i: (i, 0)),
            pl.BlockSpec((tm, D_MODEL), lambda i: (i, 1)),
            resident((A_WIDTH, D_MODEL)),
            resident((B_WIDTH, D_MODEL)),
            resident((D_MODEL, D_MODEL)),
            resident((1, D_MODEL)),
        ],
        out_specs=pl.BlockSpec((tm, D_MODEL), lambda i: (i, 0)),
        out_shape=jax.ShapeDtypeStruct((n, D_MODEL), F32),
        compiler_params=pltpu.CompilerParams(
            dimension_semantics=("parallel",), vmem_limit_bytes=VMEM_LIMIT),
        name="merge_out",
    )(x2d, a, b, z, z, w_a, w_b, w_o, g_post)


def _permute_heads(w, axis):
    shape = w.shape
    split = shape[:axis] + (2, 2, 4, HEAD_DIM) + shape[axis + 1:]
    w5 = w.reshape(split)
    w5 = jnp.swapaxes(w5, axis + 1, axis + 2)
    return w5.reshape(shape)


def _rope_tables(pos):
    inv = ROPE_THETA ** (-jnp.arange(0, HEAD_DIM, 2, dtype=F32) / HEAD_DIM)
    ang = pos.astype(F32)[:, None] * inv[None, :]
    cos, sin = jnp.cos(ang), jnp.sin(ang)
    cos_t = jnp.concatenate([cos, cos, cos, cos], axis=1)
    sin_t = jnp.concatenate([-sin, sin, -sin, sin], axis=1)
    return cos_t, sin_t


def kernel(x_prompt, x_sample, cache_k_win, cache_v_win, g_pre, w_in, ln_v_g, ln_v_b, w_spatial,
           b_spatial, sinks, w_proj_a, w_proj_b, w_out, g_post):
    bsz, seq, _ = x_prompt.shape
    dbsz, dseq, _ = x_sample.shape
    assert seq == SEQ and dseq == DEC_SEQ and seq % IN_PROJ_ROWS == 0
    assert (dbsz * dseq) % IN_PROJ_ROWS == 0 and g_pre.shape[0] == 1

    w = w_in[0]
    o_u, o_v, o_za = 0, A_WIDTH, 2 * A_WIDTH
    o_q = 3 * A_WIDTH
    o_k = o_q + B_WIDTH
    o_va = o_k + KV_WIDTH
    o_zb = o_va + KV_WIDTH
    o_ga = o_zb + B_WIDTH
    o_gb = o_ga + D_MODEL
    w_in_p = jnp.concatenate([
        w[:, o_ga:o_ga + D_MODEL], w[:, o_gb:o_gb + D_MODEL],
        w[:, o_u:o_u + A_WIDTH], w[:, o_v:o_v + A_WIDTH], w[:, o_za:o_za + A_WIDTH],
        _permute_heads(w[:, o_q:o_q + B_WIDTH], 1), _permute_heads(w[:, o_zb:o_zb + B_WIDTH], 1),
        w[:, o_k:o_k + KV_WIDTH], w[:, o_va:o_va + KV_WIDTH]], axis=1).astype(BF16)
    w_a = w_proj_a[0].astype(BF16)
    w_b = _permute_heads(w_proj_b[0], 0).astype(BF16)
    w_o = w_out[0].astype(BF16)
    b_s_rows = jnp.repeat(b_spatial[0].T, LANES, axis=1)
    w_s_tiled = jnp.tile(w_spatial[0][:, :DEC_SEQ, :DEC_SEQ], (1, SEQS_PER_STEP, SEQS_PER_STEP))
    b_s_tiled = jnp.tile(b_s_rows[:DEC_SEQ], (SEQS_PER_STEP, 1))
    cos_p, sin_p = _rope_tables(jnp.arange(SEQ))
    cos_s, sin_s = _rope_tables(PAST_LEN + jnp.arange(dbsz * dseq) % dseq)

    xp = x_prompt.reshape(bsz * seq, D_MODEL)
    z_p, kv32_p = _in_proj(xp, g_pre, w_in_p, cos_p, sin_p, seq // IN_PROJ_ROWS)
    a_p, b_p = _prompt_mixers(z_p, ln_v_g, ln_v_b, w_spatial[0], b_s_rows, sinks[0])
    y_p = _merge_out(xp, a_p, b_p, z_p, w_a, w_b, w_o, g_post)
    kv_win_p = kv32_p.reshape(bsz, seq, 2 * KV_WIDTH)[:, seq - WINDOW:]
    k_win_p = kv_win_p[..., :KV_WIDTH].reshape(1, bsz, WINDOW, N_KV_HEADS, HEAD_DIM)
    v_win_p = kv_win_p[..., KV_WIDTH:].reshape(1, bsz, WINDOW, N_KV_HEADS, HEAD_DIM)

    xs = x_sample.reshape(dbsz * dseq, D_MODEL)
    z_s, kv32_s = _in_proj(xs, g_pre, w_in_p, cos_s, sin_s, 1)
    ck = cache_k_win[0].reshape(dbsz, WINDOW, KV_WIDTH)
    cv = cache_v_win[0].reshape(dbsz, WINDOW, KV_WIDTH)
    a_s, b_s, vn_s, k_win_s, v_win_s = _sample_mixers(
        z_s, kv32_s, ck, cv, ln_v_g, ln_v_b, w_s_tiled, b_s_tiled, sinks[0])
    y_s = _merge_out(xs, a_s, b_s, z_s, w_a, w_b, w_o, g_post)

    return (y_p.reshape(bsz, seq, D_MODEL),
            y_s.reshape(dbsz, dseq, D_MODEL),
            k_win_p, v_win_p,
            k_win_s.reshape(1, dbsz, WINDOW, N_KV_HEADS, HEAD_DIM),
            v_win_s.reshape(1, dbsz, WINDOW, N_KV_HEADS, HEAD_DIM),
            vn_s.reshape(1, dbsz, dseq, A_WIDTH))
```

```python
import functools

import numpy as np
import jax
import jax.numpy as jnp
from jax import lax
from jax.experimental import pallas as pl
from jax.experimental.pallas import tpu as pltpu

D_MODEL = 2048
SEQ = 2048
DEC_SEQ = 8
PAST_LEN = 8192
CHUNK = 128
A_WIDTH = 1024
A_GROUPS = 8
HEAD_DIM = 64
N_HEADS = 16
N_KV_HEADS = 4
Q_PER_KV = N_HEADS // N_KV_HEADS
B_WIDTH = N_HEADS * HEAD_DIM
KV_WIDTH = N_KV_HEADS * HEAD_DIM
WINDOW = 128
ROPE_THETA = 10000.0
EPS = 1e-6
IN_COLS = 3 * A_WIDTH + 2 * B_WIDTH + 2 * KV_WIDTH + 2 * D_MODEL

LANES = 128
SUBLANES = 8
BF16 = jnp.bfloat16
F32 = jnp.float32

COL_TILE = 512
W_TILE_ORDER = (11, 12, 13, 14, 15, 16, 17, 18, 0, 1, 2, 3, 4, 5, 6, 7, 9, 10, 8)
SEC_GATE_END = 8
SEC_GELU_END = 12
SEC_ZA_END = 14
SEC_Q_END = 16
SEC_ZB_END = 18
N_COL_TILES = 19
Z_COLS = (N_COL_TILES - 1) * COL_TILE
COL_U, COL_V, COL_ZA, COL_Q, COL_ZB = 4, 5, 6, 7, 8
KVD_WIDTH = 2 * N_KV_HEADS * LANES

IN_PROJ_ROWS = 1024
IN_PROJ_SUB = 256
MERGE_ROWS = 512
VMEM_LIMIT = 56 * 1024 * 1024


def _gelu(x):
    return 0.5 * x * (1.0 + lax.erf(x * np.float32(1.0 / np.sqrt(2.0))))


def _sigmoid(x):
    return 1.0 / (1.0 + jnp.exp(-x))


def _rope(x, cos, sin_signed):
    width = x.shape[1]
    lane = lax.broadcasted_iota(jnp.int32, x.shape, 1)
    first_half = (lane & (HEAD_DIM - 1)) < (HEAD_DIM // 2)
    partner = jnp.where(first_half,
                        pltpu.roll(x, width - HEAD_DIM // 2, axis=1),
                        pltpu.roll(x, HEAD_DIM // 2, axis=1))
    reps = width // LANES
    cos_w = jnp.concatenate([cos] * reps, axis=1) if reps > 1 else cos
    sin_w = jnp.concatenate([sin_signed] * reps, axis=1) if reps > 1 else sin_signed
    return x * cos_w + partner * sin_w


def _dup_heads(x):
    lane = lax.broadcasted_iota(jnp.int32, (x.shape[0], LANES), 1)
    low = lane < HEAD_DIM
    out = []
    for c in range(x.shape[1] // LANES):
        xc = x[:, c * LANES:(c + 1) * LANES]
        sw = pltpu.roll(xc, HEAD_DIM, axis=1)
        out += [jnp.where(low, xc, sw), jnp.where(low, sw, xc)]
    return jnp.concatenate(out, axis=1)


def _in_proj_kernel(x_ref, g_ref, w_ref, cos_ref, sin_ref, z_ref, kvd_ref, kv32_ref, h_scr, *,
                    table_rows):
    j = pl.program_id(1)
    tm = x_ref.shape[0]
    sub = IN_PROJ_SUB

    def tables(rs):
        if table_rows == tm:
            return cos_ref[rs, :], sin_ref[rs, :]
        tile = lambda t: jnp.broadcast_to(t[None], (sub // table_rows, table_rows, LANES)).reshape(sub, LANES)
        return tile(cos_ref[...]), tile(sin_ref[...])

    def run(epilogue, norm=False):
        for r in range(tm // sub):
            rs = slice(r * sub, (r + 1) * sub)
            if norm:
                x = x_ref[rs, :]
                ms = jnp.mean(x * x, axis=-1, keepdims=True)
                h = (x * lax.rsqrt(ms + EPS) * g_ref[...]).astype(BF16)
                h_scr[rs, :] = h
            else:
                h = h_scr[rs, :]
            epilogue(jnp.dot(h, w_ref[...], preferred_element_type=F32), rs)

    def gate_epi(acc, rs):
        z_ref[rs, :] = _sigmoid(acc).astype(BF16)

    def gelu_epi(acc, rs):
        z_ref[rs, :] = _gelu(acc).astype(BF16)

    def silu_epi(acc, rs):
        z_ref[rs, :] = (acc * _sigmoid(acc)).astype(BF16)

    def q_epi(acc, rs):
        cos, sin = tables(rs)
        z_ref[rs, :] = (_rope(acc, cos, sin) * np.float32(HEAD_DIM ** -0.5)).astype(BF16)

    def kv_epi(acc, rs):
        cos, sin = tables(rs)
        k = _rope(acc[:, :KV_WIDTH], cos, sin)
        v = acc[:, KV_WIDTH:]
        kv32_ref[rs, :KV_WIDTH] = k
        kv32_ref[rs, KV_WIDTH:] = v
        kvd_ref[rs, :KVD_WIDTH // 2] = _dup_heads(k).astype(BF16)
        kvd_ref[rs, KVD_WIDTH // 2:] = _dup_heads(v).astype(BF16)

    pl.when(j == 0)(lambda: run(gate_epi, norm=True))
    pl.when((j > 0) & (j < SEC_GATE_END))(lambda: run(gate_epi))
    pl.when((j >= SEC_GATE_END) & (j < SEC_GELU_END))(lambda: run(gelu_epi))
    pl.when(((j >= SEC_GELU_END) & (j < SEC_ZA_END)) | ((j >= SEC_Q_END) & (j < SEC_ZB_END)))(
        lambda: run(silu_epi))
    pl.when((j >= SEC_ZA_END) & (j < SEC_Q_END))(lambda: run(q_epi))
    pl.when(j == N_COL_TILES - 1)(lambda: run(kv_epi))


def _w_tile(j):
    return jnp.where(j < 8, j + 11, jnp.where(j < 16, j - 8, jnp.where(j < 18, j - 7, 8)))


assert all(w == (j + 11 if j < 8 else j - 8 if j < 16 else j - 7 if j < 18 else 8)
           for j, w in enumerate(W_TILE_ORDER))


def _in_proj(x2d, g_pre, w_in_b, cos_t, sin_t, table_rows):
    n = x2d.shape[0]
    tm = IN_PROJ_ROWS
    if table_rows == tm:
        n_tab = cos_t.shape[0] // tm
        table_spec = pl.BlockSpec((tm, LANES), lambda i, j: (i % n_tab, 0))
    else:
        table_spec = pl.BlockSpec((table_rows, LANES), lambda i, j: (0, 0))
    return pl.pallas_call(
        functools.partial(_in_proj_kernel, table_rows=table_rows),
        grid=(n // tm, N_COL_TILES),
        in_specs=[
            pl.BlockSpec((tm, D_MODEL), lambda i, j: (i, 0)),
            pl.BlockSpec((1, D_MODEL), lambda i, j: (0, 0)),
            pl.BlockSpec((D_MODEL, COL_TILE), lambda i, j: (0, _w_tile(j))),
            table_spec, table_spec,
        ],
        out_specs=[
            pl.BlockSpec((tm, COL_TILE), lambda i, j: (i, jnp.minimum(j, N_COL_TILES - 2))),
            pl.BlockSpec((tm, KVD_WIDTH), lambda i, j: (i, 0)),
            pl.BlockSpec((tm, 2 * KV_WIDTH), lambda i, j: (i, 0)),
        ],
        out_shape=[
            jax.ShapeDtypeStruct((n, Z_COLS), BF16),
            jax.ShapeDtypeStruct((n, KVD_WIDTH), BF16),
            jax.ShapeDtypeStruct((n, 2 * KV_WIDTH), F32),
        ],
        scratch_shapes=[pltpu.VMEM((tm, D_MODEL), BF16)],
        compiler_params=pltpu.CompilerParams(
            dimension_semantics=("parallel", "arbitrary"), vmem_limit_bytes=VMEM_LIMIT),
        name="in_proj",
    )(x2d, g_pre, w_in_b, cos_t, sin_t)


def _layernorm_v(v_ref, lng_ref, lnb_ref):
    vg = v_ref[...].astype(F32)
    mu = jnp.mean(vg, axis=-1, keepdims=True)
    xc = vg - mu
    var = jnp.mean(xc * xc, axis=-1, keepdims=True)
    return xc * lax.rsqrt(var + EPS) * lng_ref[...] + lnb_ref[...]


def _prompt_mixers_kernel(u_ref, v_ref, za_ref, q_ref, zb_ref, kvd_ref, kvdp_ref, lng_ref, lnb_ref,
                          ws_ref, bs_ref, sinks_ref, a_ref, b_ref):
    blk = pl.program_id(0) % (SEQ // WINDOW)

    row = lax.broadcasted_iota(jnp.int32, (CHUNK, CHUNK), 0)
    col = lax.broadcasted_iota(jnp.int32, (CHUNK, CHUNK), 1)
    vn_b = _layernorm_v(v_ref, lng_ref, lnb_ref).astype(BF16)
    for g in range(A_GROUPS):
        cols = slice(g * LANES, (g + 1) * LANES)
        w = jnp.where(col <= row, ws_ref[g], 0.0).astype(BF16)
        s = jnp.dot(w, vn_b[:, cols], preferred_element_type=F32) + bs_ref[:, cols]
        a_ref[:, cols] = ((u_ref[:, cols].astype(F32) * s) * za_ref[:, cols].astype(F32)).astype(BF16)

    t = lax.broadcasted_iota(jnp.int32, (WINDOW, 2 * WINDOW), 0)
    jj = lax.broadcasted_iota(jnp.int32, (WINDOW, 2 * WINDOW), 1)
    mask = (jj > t) & (jj <= t + WINDOW) & ((jj >= WINDOW) | (blk > 0))
    lane = lax.broadcasted_iota(jnp.int32, (WINDOW, LANES), 1)
    low_half = lane < HEAD_DIM
    for m in range(N_HEADS // 2):
        cols = slice(m * LANES, (m + 1) * LANES)
        kcols = slice((m // 2) * LANES, (m // 2 + 1) * LANES)
        vcols = slice(KVD_WIDTH // 2 + (m // 2) * LANES, KVD_WIDTH // 2 + (m // 2 + 1) * LANES)
        k_c = jnp.concatenate([kvdp_ref[:, kcols], kvd_ref[:, kcols]], axis=0)
        v_c = jnp.concatenate([kvdp_ref[:, vcols], kvd_ref[:, vcols]], axis=0)
        q_m = q_ref[:, cols]
        outs = []
        for half in range(2):
            sink = sinks_ref[2 * m + half]
            q_h = jnp.where(low_half if half == 0 else ~low_half, q_m, jnp.zeros_like(q_m))
            s = lax.dot_general(q_h, k_c, (((1,), (1,)), ((), ())), preferred_element_type=F32)
            s = jnp.where(mask, s, -jnp.inf)
            mx = jnp.maximum(jnp.max(s, axis=-1, keepdims=True), sink)
            p = jnp.exp(s - mx)
            denom = jnp.sum(p, axis=-1, keepdims=True) + jnp.exp(sink - mx)
            o = jnp.dot(p.astype(BF16), v_c, preferred_element_type=F32)
            outs.append(o * (1.0 / denom))
        o_m = jnp.where(low_half, outs[0], outs[1])
        b_ref[:, cols] = (o_m * zb_ref[:, cols].astype(F32)).astype(BF16)


def _prompt_mixers(z, kvd, ln_g, ln_b, w_s, b_s_rows, sinks):
    n = z.shape[0]
    wide = lambda cb: pl.BlockSpec((CHUNK, A_WIDTH), lambda r, cb=cb: (r, cb))
    const2 = lambda shape: pl.BlockSpec(shape, lambda r: (0, 0))
    return pl.pallas_call(
        _prompt_mixers_kernel,
        grid=(n // CHUNK,),
        in_specs=[
            wide(COL_U), wide(COL_V), wide(COL_ZA), wide(COL_Q), wide(COL_ZB),
            pl.BlockSpec((CHUNK, KVD_WIDTH), lambda r: (r, 0)),
            pl.BlockSpec((CHUNK, KVD_WIDTH), lambda r: (jnp.maximum(r - 1, 0), 0)),
            const2((1, A_WIDTH)), const2((1, A_WIDTH)),
            pl.BlockSpec((A_GROUPS, CHUNK, CHUNK), lambda r: (0, 0, 0)),
            const2((CHUNK, A_WIDTH)),
            pl.BlockSpec(memory_space=pltpu.SMEM),
        ],
        out_specs=[
            pl.BlockSpec((CHUNK, A_WIDTH), lambda r: (r, 0)),
            pl.BlockSpec((CHUNK, B_WIDTH), lambda r: (r, 0)),
        ],
        out_shape=[
            jax.ShapeDtypeStruct((n, A_WIDTH), BF16),
            jax.ShapeDtypeStruct((n, B_WIDTH), BF16),
        ],
        compiler_params=pltpu.CompilerParams(
            dimension_semantics=("parallel",), vmem_limit_bytes=VMEM_LIMIT),
        name="prompt_mixers",
    )(z, z, z, z, z, kvd, kvd, ln_g, ln_b, w_s, b_s_rows, sinks)


SEQS_PER_STEP = CHUNK // DEC_SEQ


def _sample_mixers_kernel(u_ref, v_ref, za_ref, q_ref, zb_ref, kvd_ref, kv32_ref, ckt_ref, cvt_ref,
                          lng_ref, lnb_ref, coef_ref, bs_ref, sinks_ref,
                          a_ref, b_ref, vn_ref, kwin_ref, vwin_ref):
    ns = SEQS_PER_STEP
    split = lambda x: x.reshape(ns, DEC_SEQ, x.shape[-1])

    vn = _layernorm_v(v_ref, lng_ref, lnb_ref)
    vn_ref[...] = vn
    vn3 = split(vn)
    t_row = lax.broadcasted_iota(jnp.int32, (DEC_SEQ, A_WIDTH), 0)
    s_acc = jnp.broadcast_to(bs_ref[...][None], (ns, DEC_SEQ, A_WIDTH))
    for s in range(DEC_SEQ):
        coef = jnp.where(t_row >= s, coef_ref[s], 0.0)
        s_acc = s_acc + vn3[:, s:s + 1, :] * coef[None]
    a3 = split(u_ref[...].astype(F32)) * s_acc * split(za_ref[...].astype(F32))
    a_ref[...] = a3.reshape(CHUNK, A_WIDTH).astype(BF16)

    rows_q = Q_PER_KV * DEC_SEQ
    lane3 = lax.broadcasted_iota(jnp.int32, (ns, DEC_SEQ, LANES), 2)
    low3 = lane3 < HEAD_DIM
    r_idx = lax.broadcasted_iota(jnp.int32, (ns, rows_q, LANES), 1)
    l_idx = lax.broadcasted_iota(jnp.int32, (ns, rows_q, LANES), 2)
    s_idx = lax.broadcasted_iota(jnp.int32, (ns, rows_q, LANES), 0)
    t_q = r_idx & (DEC_SEQ - 1)
    mask_cache = l_idx > t_q
    mask_new = ((l_idx >> 3) == s_idx) & ((l_idx & (DEC_SEQ - 1)) <= t_q)
    head_of_row = lax.broadcasted_iota(jnp.int32, (1, rows_q, 1), 1) >> 3
    q3 = split(q_ref[...].astype(F32))
    zb3 = split(zb_ref[...].astype(F32))
    for kh in range(N_KV_HEADS):
        pieces = []
        for m in (2 * kh, 2 * kh + 1):
            q_m = q3[:, :, m * LANES:(m + 1) * LANES]
            pieces += [jnp.where(low3, q_m, 0.0), jnp.where(low3, 0.0, q_m)]
        lhs = jnp.concatenate(pieces, axis=1).astype(BF16)
        rows = slice(kh * HEAD_DIM, (kh + 1) * HEAD_DIM)
        kt = ckt_ref[:, rows, :].astype(BF16)
        vt = cvt_ref[:, rows, :].astype(BF16)
        kt2 = jnp.concatenate([kt, kt], axis=1)
        vt2 = jnp.concatenate([vt, vt], axis=1)
        k_new = kvd_ref[:, kh * LANES:(kh + 1) * LANES]
        v_new = kvd_ref[:, KVD_WIDTH // 2 + kh * LANES:KVD_WIDTH // 2 + (kh + 1) * LANES]
        s_c = lax.dot_general(lhs, kt2, (((2,), (1,)), ((0,), (0,))), preferred_element_type=F32)
        s_n = lax.dot_general(lhs.reshape(ns * rows_q, LANES), k_new, (((1,), (1,)), ((), ())),
                              preferred_element_type=F32).reshape(ns, rows_q, LANES)
        s_c = jnp.where(mask_cache, s_c, -jnp.inf)
        s_n = jnp.where(mask_new, s_n, -jnp.inf)
        sink = jnp.zeros((1, rows_q, 1), F32)
        for jq in range(Q_PER_KV):
            sink = jnp.where(head_of_row == jq, sinks_ref[Q_PER_KV * kh + jq], sink)
        mx = jnp.maximum(jnp.maximum(jnp.max(s_c, axis=-1, keepdims=True),
                                     jnp.max(s_n, axis=-1, keepdims=True)), sink)
        p_c = jnp.exp(s_c - mx)
        p_n = jnp.exp(s_n - mx)
        denom = (jnp.sum(p_c, axis=-1, keepdims=True) + jnp.sum(p_n, axis=-1, keepdims=True)
                 + jnp.exp(sink - mx))
        o = lax.dot_general(p_c.astype(BF16), vt2, (((2,), (2,)), ((0,), (0,))),
                            preferred_element_type=F32)
        o = o + jnp.dot(p_n.reshape(ns * rows_q, LANES).astype(BF16), v_new,
                        preferred_element_type=F32).reshape(ns, rows_q, LANES)
        o = o * (1.0 / denom)
        for i, m in enumerate((2 * kh, 2 * kh + 1)):
            base = 2 * DEC_SEQ * i
            o_m = jnp.where(low3, o[:, base:base + DEC_SEQ, :], o[:, base + DEC_SEQ:base + 2 * DEC_SEQ, :])
            cols = slice(m * LANES, (m + 1) * LANES)
            b_ref[:, cols] = (o_m * zb3[:, :, cols]).reshape(CHUNK, LANES).astype(BF16)

    keep = WINDOW - DEC_SEQ
    lane_w = lax.broadcasted_iota(jnp.int32, (KV_WIDTH, WINDOW), 1)
    for new_rows, cache_ref, out_ref in ((kv32_ref[:, :KV_WIDTH], ckt_ref, kwin_ref),
                                         (kv32_ref[:, KV_WIDTH:], cvt_ref, vwin_ref)):
        new_t = new_rows.T
        old = pltpu.roll(cache_ref[...].reshape(ns * KV_WIDTH, WINDOW), keep, axis=1)
        for s in range(ns):
            shifted_new = pltpu.roll(new_t, (keep - DEC_SEQ * s) % WINDOW, axis=1)
            out_ref[s] = jnp.where(lane_w < keep, old[s * KV_WIDTH:(s + 1) * KV_WIDTH], shifted_new)


def _sample_mixers(z, kvd, kv32, cache_kt, cache_vt, ln_g, ln_b, coef, b_s8, sinks):
    n = z.shape[0]
    n_seq = cache_kt.shape[0]
    wide = lambda cb: pl.BlockSpec((CHUNK, A_WIDTH), lambda r, cb=cb: (r, cb))
    const2 = lambda shape: pl.BlockSpec(shape, lambda r: (0, 0))
    win = pl.BlockSpec((SEQS_PER_STEP, KV_WIDTH, WINDOW), lambda r: (r, 0, 0))
    return pl.pallas_call(
        _sample_mixers_kernel,
        grid=(n // CHUNK,),
        in_specs=[
            wide(COL_U), wide(COL_V), wide(COL_ZA), wide(COL_Q), wide(COL_ZB),
            pl.BlockSpec((CHUNK, KVD_WIDTH), lambda r: (r, 0)),
            pl.BlockSpec((CHUNK, 2 * KV_WIDTH), lambda r: (r, 0)),
            win, win,
            const2((1, A_WIDTH)), const2((1, A_WIDTH)),
            pl.BlockSpec((DEC_SEQ, DEC_SEQ, A_WIDTH), lambda r: (0, 0, 0)),
            const2((DEC_SEQ, A_WIDTH)),
            pl.BlockSpec(memory_space=pltpu.SMEM),
        ],
        out_specs=[
            pl.BlockSpec((CHUNK, A_WIDTH), lambda r: (r, 0)),
            pl.BlockSpec((CHUNK, B_WIDTH), lambda r: (r, 0)),
            pl.BlockSpec((CHUNK, A_WIDTH), lambda r: (r, 0)),
            win, win,
        ],
        out_shape=[
            jax.ShapeDtypeStruct((n, A_WIDTH), BF16),
            jax.ShapeDtypeStruct((n, B_WIDTH), BF16),
            jax.ShapeDtypeStruct((n, A_WIDTH), F32),
            jax.ShapeDtypeStruct((n_seq, KV_WIDTH, WINDOW), F32),
            jax.ShapeDtypeStruct((n_seq, KV_WIDTH, WINDOW), F32),
        ],
        compiler_params=pltpu.CompilerParams(
            dimension_semantics=("parallel",), vmem_limit_bytes=VMEM_LIMIT),
        name="sample_mixers",
    )(z, z, z, z, z, kvd, kv32, cache_kt, cache_vt, ln_g, ln_b, coef, b_s8, sinks)


def _merge_out_kernel(x_ref, a_ref, b_ref, ga_ref, gb_ref, wa_ref, wb_ref, wo_ref, g_ref, y_ref):
    pa = jnp.dot(a_ref[...], wa_ref[...], preferred_element_type=F32)
    pb = jnp.dot(b_ref[...], wb_ref[...], preferred_element_type=F32)
    merged = ga_ref[...].astype(F32) * pa + gb_ref[...].astype(F32) * pb
    out = jnp.dot(merged.astype(BF16), wo_ref[...], preferred_element_type=F32)
    ms = jnp.mean(out * out, axis=-1, keepdims=True)
    y_ref[...] = x_ref[...] + out * lax.rsqrt(ms + EPS) * g_ref[...]


def _merge_out(x2d, a, b, z, w_a, w_b, w_o, g_post):
    n = x2d.shape[0]
    tm = MERGE_ROWS
    resident = lambda shape: pl.BlockSpec(shape, lambda i: (0, 0), pipeline_mode=pl.Buffered(1))
    return pl.pallas_call(
        _merge_out_kernel,
        grid=(n // tm,),
        in_specs=[
            pl.BlockSpec((tm, D_MODEL), lambda i: (i, 0)),
            pl.BlockSpec((tm, A_WIDTH), lambda i: (i, 0)),
            pl.BlockSpec((tm, B_WIDTH), lambda i: (i, 0)),
            pl.BlockSpec((tm, D_MODEL), lambda i: (i, 0)),
            pl.BlockSpec((tm, D_MODEL), lambda i: (i, 1)),
            resident((A_WIDTH, D_MODEL)),
            resident((B_WIDTH, D_MODEL)),
            resident((D_MODEL, D_MODEL)),
            resident((1, D_MODEL)),
        ],
        out_specs=pl.BlockSpec((tm, D_MODEL), lambda i: (i, 0)),
        out_shape=jax.ShapeDtypeStruct((n, D_MODEL), F32),
        compiler_params=pltpu.CompilerParams(
            dimension_semantics=("parallel",), vmem_limit_bytes=VMEM_LIMIT),
        name="merge_out",
    )(x2d, a, b, z, z, w_a, w_b, w_o, g_post)


def _rope_tables(pos):
    lane = jnp.arange(LANES)
    inv = ROPE_THETA ** (-(2 * (lane % (HEAD_DIM // 2))).astype(F32) / HEAD_DIM)
    ang = pos.astype(F32)[:, None] * inv[None, :]
    sign = jnp.where((lane % HEAD_DIM) < HEAD_DIM // 2, -1.0, 1.0).astype(F32)
    return jnp.cos(ang), jnp.sin(ang) * sign[None, :]


def _window_first(win):
    n = win.shape[1]
    return jnp.transpose(win[0], (0, 2, 3, 1)).reshape(n, KV_WIDTH, WINDOW)


def _window_last(win_t):
    n = win_t.shape[0]
    return jnp.transpose(win_t.reshape(n, N_KV_HEADS, HEAD_DIM, WINDOW), (0, 3, 1, 2))[None]


def kernel(x_prompt, x_sample, cache_k_win, cache_v_win, g_pre, w_in, ln_v_g, ln_v_b, w_spatial,
           b_spatial, sinks, w_proj_a, w_proj_b, w_out, g_post):
    bsz, seq, _ = x_prompt.shape
    dbsz, dseq, _ = x_sample.shape
    assert seq == SEQ and dseq == DEC_SEQ and seq % IN_PROJ_ROWS == 0
    assert (dbsz * dseq) % IN_PROJ_ROWS == 0 and g_pre.shape[0] == 1

    w_in_b = w_in[0].astype(BF16)
    w_a = w_proj_a[0].astype(BF16)
    w_b = w_proj_b[0].astype(BF16)
    w_o = w_out[0].astype(BF16)
    b_s_rows = jnp.repeat(b_spatial[0].T, LANES, axis=1)
    coef = jnp.repeat(jnp.transpose(w_spatial[0][:, :DEC_SEQ, :DEC_SEQ], (2, 1, 0)), LANES, axis=2)
    cos_p, sin_p = _rope_tables(jnp.arange(SEQ))
    cos_s, sin_s = _rope_tables(PAST_LEN + jnp.arange(dseq))

    xp = x_prompt.reshape(bsz * seq, D_MODEL)
    z_p, kvd_p, kv32_p = _in_proj(xp, g_pre, w_in_b, cos_p, sin_p, IN_PROJ_ROWS)
    a_p, b_p = _prompt_mixers(z_p, kvd_p, ln_v_g, ln_v_b, w_spatial[0], b_s_rows, sinks[0])
    y_p = _merge_out(xp, a_p, b_p, z_p, w_a, w_b, w_o, g_post)
    kv_win_p = kv32_p.reshape(bsz, seq, 2 * KV_WIDTH)[:, seq - WINDOW:]
    k_win_p = kv_win_p[..., :KV_WIDTH].reshape(1, bsz, WINDOW, N_KV_HEADS, HEAD_DIM)
    v_win_p = kv_win_p[..., KV_WIDTH:].reshape(1, bsz, WINDOW, N_KV_HEADS, HEAD_DIM)

    xs = x_sample.reshape(dbsz * dseq, D_MODEL)
    z_s, kvd_s, kv32_s = _in_proj(xs, g_pre, w_in_b, cos_s, sin_s, dseq)
    a_s, b_s, vn_s, k_win_t, v_win_t = _sample_mixers(
        z_s, kvd_s, kv32_s, _window_first(cache_k_win), _window_first(cache_v_win),
        ln_v_g, ln_v_b, coef, b_s_rows[:dseq], sinks[0])
    y_s = _merge_out(xs, a_s, b_s, z_s, w_a, w_b, w_o, g_post)

    return (y_p.reshape(bsz, seq, D_MODEL),
            y_s.reshape(dbsz, dseq, D_MODEL),
            k_win_p, v_win_p,
            _window_last(k_win_t), _window_last(v_win_t),
            vn_s.reshape(1, dbsz, dseq, A_WIDTH))
```

```python
import functools

import numpy as np
import jax
import jax.numpy as jnp
from jax import lax
from jax.experimental import pallas as pl
from jax.experimental.pallas import tpu as pltpu

D_MODEL = 2048
SEQ = 2048
DEC_SEQ = 8
PAST_LEN = 8192
CHUNK = 128
A_WIDTH = 1024
A_GROUPS = 8
HEAD_DIM = 64
N_HEADS = 16
N_KV_HEADS = 4
Q_PER_KV = N_HEADS // N_KV_HEADS
B_WIDTH = N_HEADS * HEAD_DIM
KV_WIDTH = N_KV_HEADS * HEAD_DIM
WINDOW = 128
ROPE_THETA = 10000.0
EPS = 1e-6
IN_COLS = 3 * A_WIDTH + 2 * B_WIDTH + 2 * KV_WIDTH + 2 * D_MODEL

LANES = 128
SUBLANES = 8
BF16 = jnp.bfloat16
F32 = jnp.float32

COL_TILE = 512
N_COL_TILES = IN_COLS // COL_TILE
Z_COLS = (N_COL_TILES - 1) * COL_TILE
COL_U, COL_V, COL_ZA, COL_Q, COL_ZB = 4, 5, 6, 7, 8
KVD_WIDTH = 2 * N_KV_HEADS * LANES
SEC_GATE_NORM, SEC_GATE, SEC_GELU, SEC_SILU, SEC_Q, SEC_KV = range(6)


def _walk(tiles_per_step):
    sections = [((11, 12, 13, 14, 15, 16, 17, 18), SEC_GATE), ((0, 1, 2, 3), SEC_GELU),
                ((4, 5), SEC_SILU), ((6, 7), SEC_Q), ((9, 10), SEC_SILU)]
    steps = [(tiles[k:k + tiles_per_step], sec)
             for tiles, sec in sections for k in range(0, len(tiles), tiles_per_step)]
    steps[0] = (steps[0][0], SEC_GATE_NORM)
    steps.append(((8,) * tiles_per_step, SEC_KV))
    return steps


def _lookup(values, j):
    out = jnp.int32(values[-1])
    for k in range(len(values) - 2, -1, -1):
        out = jnp.where(j == k, jnp.int32(values[k]), out)
    return out


IN_PROJ_ROWS = 1024
MERGE_ROWS = 512
VMEM_LIMIT = 56 * 1024 * 1024


def _gelu(x):
    return 0.5 * x * (1.0 + lax.erf(x * np.float32(1.0 / np.sqrt(2.0))))


def _sigmoid(x):
    return 1.0 / (1.0 + jnp.exp(-x))


def _rope(x, cos, sin_signed):
    width = x.shape[1]
    lane = lax.broadcasted_iota(jnp.int32, x.shape, 1)
    first_half = (lane & (HEAD_DIM - 1)) < (HEAD_DIM // 2)
    partner = jnp.where(first_half,
                        pltpu.roll(x, width - HEAD_DIM // 2, axis=1),
                        pltpu.roll(x, HEAD_DIM // 2, axis=1))
    reps = width // LANES
    cos_w = jnp.concatenate([cos] * reps, axis=1) if reps > 1 else cos
    sin_w = jnp.concatenate([sin_signed] * reps, axis=1) if reps > 1 else sin_signed
    return x * cos_w + partner * sin_w


def _dup_heads(x):
    lane = lax.broadcasted_iota(jnp.int32, (x.shape[0], LANES), 1)
    low = lane < HEAD_DIM
    out = []
    for c in range(x.shape[1] // LANES):
        xc = x[:, c * LANES:(c + 1) * LANES]
        sw = pltpu.roll(xc, HEAD_DIM, axis=1)
        out += [jnp.where(low, xc, sw), jnp.where(low, sw, xc)]
    return jnp.concatenate(out, axis=1)


def _in_proj_kernel(x_ref, g_ref, *refs, table_rows, sub, walk):
    n_w = len(walk[0][0])
    w_refs = refs[:n_w]
    cos_ref, sin_ref, z_ref, kvd_ref, kv32_ref, h_scr = refs[n_w:]
    j = pl.program_id(1)
    tm = x_ref.shape[0]

    def tables(rs):
        if table_rows == tm:
            return cos_ref[rs, :], sin_ref[rs, :]
        tile = lambda t: jnp.broadcast_to(t[None], (sub // table_rows, table_rows, LANES)).reshape(sub, LANES)
        return tile(cos_ref[...]), tile(sin_ref[...])

    def run(epilogue, norm=False, n_tiles=n_w):
        for r in range(tm // sub):
            rs = slice(r * sub, (r + 1) * sub)
            if norm:
                x = x_ref[rs, :]
                ms = jnp.mean(x * x, axis=-1, keepdims=True)
                h = (x * lax.rsqrt(ms + EPS) * g_ref[...]).astype(BF16)
                h_scr[rs, :] = h
            else:
                h = h_scr[rs, :]
            for t in range(n_tiles):
                acc = jnp.dot(h, w_refs[t][...], preferred_element_type=F32)
                epilogue(acc, rs, slice(t * COL_TILE, (t + 1) * COL_TILE))

    def gate_epi(acc, rs, cs):
        z_ref[rs, cs] = _sigmoid(acc).astype(BF16)

    def gelu_epi(acc, rs, cs):
        z_ref[rs, cs] = _gelu(acc).astype(BF16)

    def silu_epi(acc, rs, cs):
        z_ref[rs, cs] = (acc * _sigmoid(acc)).astype(BF16)

    def q_epi(acc, rs, cs):
        cos, sin = tables(rs)
        z_ref[rs, cs] = (_rope(acc, cos, sin) * np.float32(HEAD_DIM ** -0.5)).astype(BF16)

    def kv_epi(acc, rs, cs):
        cos, sin = tables(rs)
        k = _rope(acc[:, :KV_WIDTH], cos, sin)
        v = acc[:, KV_WIDTH:]
        kv32_ref[rs, :KV_WIDTH] = k
        kv32_ref[rs, KV_WIDTH:] = v
        kvd_ref[rs, :KVD_WIDTH // 2] = _dup_heads(k).astype(BF16)
        kvd_ref[rs, KVD_WIDTH // 2:] = _dup_heads(v).astype(BF16)

    sec = _lookup(tuple(s for _, s in walk), j)
    pl.when(sec == SEC_GATE_NORM)(lambda: run(gate_epi, norm=True))
    pl.when(sec == SEC_GATE)(lambda: run(gate_epi))
    pl.when(sec == SEC_GELU)(lambda: run(gelu_epi))
    pl.when(sec == SEC_SILU)(lambda: run(silu_epi))
    pl.when(sec == SEC_Q)(lambda: run(q_epi))
    pl.when(sec == SEC_KV)(lambda: run(kv_epi, n_tiles=1))


def _in_proj(x2d, g_pre, w_in_b, cos_t, sin_t, table_rows, tiles_per_step, sub):
    n = x2d.shape[0]
    tm = IN_PROJ_ROWS
    walk = _walk(tiles_per_step)
    n_steps = len(walk)
    if table_rows == tm:
        n_tab = cos_t.shape[0] // tm
        table_spec = pl.BlockSpec((tm, LANES), lambda i, j: (i % n_tab, 0))
    else:
        table_spec = pl.BlockSpec((table_rows, LANES), lambda i, j: (0, 0))
    w_specs = [pl.BlockSpec((D_MODEL, COL_TILE),
                            lambda i, j, t=t: (0, _lookup(tuple(tiles[t] for tiles, _ in walk), j)))
               for t in range(tiles_per_step)]
    return pl.pallas_call(
        functools.partial(_in_proj_kernel, table_rows=table_rows, sub=sub, walk=walk),
        grid=(n // tm, n_steps),
        in_specs=[
            pl.BlockSpec((tm, D_MODEL), lambda i, j: (i, 0)),
            pl.BlockSpec((1, D_MODEL), lambda i, j: (0, 0)),
            *w_specs,
            table_spec, table_spec,
        ],
        out_specs=[
            pl.BlockSpec((tm, tiles_per_step * COL_TILE), lambda i, j: (i, jnp.minimum(j, n_steps - 2))),
            pl.BlockSpec((tm, KVD_WIDTH), lambda i, j: (i, 0)),
            pl.BlockSpec((tm, 2 * KV_WIDTH), lambda i, j: (i, 0)),
        ],
        out_shape=[
            jax.ShapeDtypeStruct((n, Z_COLS), BF16),
            jax.ShapeDtypeStruct((n, KVD_WIDTH), BF16),
            jax.ShapeDtypeStruct((n, 2 * KV_WIDTH), F32),
        ],
        scratch_shapes=[pltpu.VMEM((tm, D_MODEL), BF16)],
        compiler_params=pltpu.CompilerParams(
            dimension_semantics=("parallel", "arbitrary"), vmem_limit_bytes=VMEM_LIMIT),
        name="in_proj",
    )(x2d, g_pre, *([w_in_b] * tiles_per_step), cos_t, sin_t)


def _layernorm_v(v_ref, lng_ref, lnb_ref):
    vg = v_ref[...].astype(F32)
    mu = jnp.mean(vg, axis=-1, keepdims=True)
    xc = vg - mu
    var = jnp.mean(xc * xc, axis=-1, keepdims=True)
    return xc * lax.rsqrt(var + EPS) * lng_ref[...] + lnb_ref[...]


def _mix_block(u, v, za, q, zb, kvd, kvd_prev, first_block, lng, lnb, ws_ref, bs_ref, sinks_ref,
               store_a, store_b):
    row = lax.broadcasted_iota(jnp.int32, (CHUNK, CHUNK), 0)
    col = lax.broadcasted_iota(jnp.int32, (CHUNK, CHUNK), 1)

    vg = v.astype(F32)
    mu = jnp.mean(vg, axis=-1, keepdims=True)
    xc = vg - mu
    var = jnp.mean(xc * xc, axis=-1, keepdims=True)
    vn_b = (xc * lax.rsqrt(var + EPS) * lng + lnb).astype(BF16)
    for g in range(A_GROUPS):
        cols = slice(g * LANES, (g + 1) * LANES)
        w = jnp.where(col <= row, ws_ref[g], 0.0).astype(BF16)
        s = jnp.dot(w, vn_b[:, cols], preferred_element_type=F32) + bs_ref[:, cols]
        store_a(cols, ((u[:, cols].astype(F32) * s) * za[:, cols].astype(F32)).astype(BF16))

    t = lax.broadcasted_iota(jnp.int32, (WINDOW, 2 * WINDOW), 0)
    jj = lax.broadcasted_iota(jnp.int32, (WINDOW, 2 * WINDOW), 1)
    mask = (jj > t) & (jj <= t + WINDOW) & ((jj >= WINDOW) | jnp.logical_not(first_block))
    lane = lax.broadcasted_iota(jnp.int32, (WINDOW, LANES), 1)
    low_half = lane < HEAD_DIM
    for m in range(N_HEADS // 2):
        cols = slice(m * LANES, (m + 1) * LANES)
        kcols = slice((m // 2) * LANES, (m // 2 + 1) * LANES)
        vcols = slice(KVD_WIDTH // 2 + (m // 2) * LANES, KVD_WIDTH // 2 + (m // 2 + 1) * LANES)
        k_c = jnp.concatenate([kvd_prev[:, kcols], kvd[:, kcols]], axis=0)
        v_c = jnp.concatenate([kvd_prev[:, vcols], kvd[:, vcols]], axis=0)
        q_m = q[:, cols]
        outs = []
        for half in range(2):
            sink = sinks_ref[2 * m + half]
            q_h = jnp.where(low_half if half == 0 else ~low_half, q_m, jnp.zeros_like(q_m))
            s = lax.dot_general(q_h, k_c, (((1,), (1,)), ((), ())), preferred_element_type=F32)
            s = jnp.where(mask, s, -jnp.inf)
            mx = jnp.maximum(jnp.max(s, axis=-1, keepdims=True), sink)
            p = jnp.exp(s - mx)
            denom = jnp.sum(p, axis=-1, keepdims=True) + jnp.exp(sink - mx)
            o = jnp.dot(p.astype(BF16), v_c, preferred_element_type=F32)
            outs.append(o * (1.0 / denom))
        o_m = jnp.where(low_half, outs[0], outs[1])
        store_b(cols, (o_m * zb[:, cols].astype(F32)).astype(BF16))


def _merge(x, a, b, ga, gb, wa_ref, wb_ref, wo_ref, g_ref):
    pa = jnp.dot(a, wa_ref[...], preferred_element_type=F32)
    pb = jnp.dot(b, wb_ref[...], preferred_element_type=F32)
    merged = ga.astype(F32) * pa + gb.astype(F32) * pb
    out = jnp.dot(merged.astype(BF16), wo_ref[...], preferred_element_type=F32)
    ms = jnp.mean(out * out, axis=-1, keepdims=True)
    return x + out * lax.rsqrt(ms + EPS) * g_ref[...]


MIX_ROWS = 256
MIX_BLOCKS = MIX_ROWS // CHUNK


def _prompt_mix_merge_kernel(u_ref, v_ref, za_ref, q_ref, zb_ref, kvd_ref, kvdp_ref, x_ref, ga_ref,
                             gb_ref, lng_ref, lnb_ref, ws_ref, bs_ref, sinks_ref, wa_ref, wb_ref,
                             wo_ref, g_ref, y_ref, a_scr, b_scr):
    i = pl.program_id(0)
    last_tile = pl.num_programs(0) - 2
    tile = jnp.minimum(i, last_tile)
    slot = i % 2

    @pl.when(i == 0)
    def _():
        a_scr[1] = jnp.zeros(a_scr.shape[1:], BF16)
        b_scr[1] = jnp.zeros(b_scr.shape[1:], BF16)

    for blk in range(MIX_BLOCKS):
        rows = slice(blk * CHUNK, (blk + 1) * CHUNK)
        first_block = ((tile * MIX_BLOCKS + blk) % (SEQ // WINDOW)) == 0
        kvd_prev = kvdp_ref[...] if blk == 0 else kvd_ref[(blk - 1) * CHUNK:blk * CHUNK, :]

        def store_a(cols, val, rows=rows):
            a_scr[slot, rows, cols] = val

        def store_b(cols, val, rows=rows):
            b_scr[slot, rows, cols] = val

        _mix_block(u_ref[rows, :], v_ref[rows, :], za_ref[rows, :], q_ref[rows, :], zb_ref[rows, :],
                   kvd_ref[rows, :], kvd_prev, first_block, lng_ref[...], lnb_ref[...], ws_ref, bs_ref,
                   sinks_ref, store_a, store_b)

    y_ref[...] = _merge(x_ref[...], a_scr[1 - slot], b_scr[1 - slot], ga_ref[...], gb_ref[...],
                        wa_ref, wb_ref, wo_ref, g_ref)


def _prompt_mix_merge(x2d, z, kvd, ln_g, ln_b, w_s, b_s_rows, sinks, w_a, w_b, w_o, g_post):
    n = z.shape[0]
    tm = MIX_ROWS
    n_tiles = n // tm
    mix_tile = lambda i: jnp.minimum(i, n_tiles - 1)
    merge_tile = lambda i: jnp.maximum(i - 1, 0)
    wide = lambda cb: pl.BlockSpec((tm, A_WIDTH), lambda i, cb=cb: (mix_tile(i), cb))
    const2 = lambda shape: pl.BlockSpec(shape, lambda i: (0, 0))
    resident = lambda shape: pl.BlockSpec(shape, lambda i: (0, 0), pipeline_mode=pl.Buffered(1))
    return pl.pallas_call(
        _prompt_mix_merge_kernel,
        grid=(n_tiles + 1,),
        in_specs=[
            wide(COL_U), wide(COL_V), wide(COL_ZA), wide(COL_Q), wide(COL_ZB),
            pl.BlockSpec((tm, KVD_WIDTH), lambda i: (mix_tile(i), 0)),
            pl.BlockSpec((CHUNK, KVD_WIDTH), lambda i: (jnp.maximum(mix_tile(i) * MIX_BLOCKS - 1, 0), 0)),
            pl.BlockSpec((tm, D_MODEL), lambda i: (merge_tile(i), 0)),
            pl.BlockSpec((tm, D_MODEL), lambda i: (merge_tile(i), 0)),
            pl.BlockSpec((tm, D_MODEL), lambda i: (merge_tile(i), 1)),
            const2((1, A_WIDTH)), const2((1, A_WIDTH)),
            pl.BlockSpec((A_GROUPS, CHUNK, CHUNK), lambda i: (0, 0, 0)),
            const2((CHUNK, A_WIDTH)),
            pl.BlockSpec(memory_space=pltpu.SMEM),
            resident((A_WIDTH, D_MODEL)),
            resident((B_WIDTH, D_MODEL)),
            resident((D_MODEL, D_MODEL)),
            resident((1, D_MODEL)),
        ],
        out_specs=pl.BlockSpec((tm, D_MODEL), lambda i: (merge_tile(i), 0)),
        out_shape=jax.ShapeDtypeStruct((n, D_MODEL), F32),
        scratch_shapes=[pltpu.VMEM((2, tm, A_WIDTH), BF16), pltpu.VMEM((2, tm, B_WIDTH), BF16)],
        compiler_params=pltpu.CompilerParams(
            dimension_semantics=("arbitrary",), vmem_limit_bytes=VMEM_LIMIT),
        name="prompt_mix_merge",
    )(z, z, z, z, z, kvd, kvd, x2d, z, z, ln_g, ln_b, w_s, b_s_rows, sinks, w_a, w_b, w_o, g_post)


SEQS_PER_STEP = CHUNK // DEC_SEQ


def _sample_mixers_kernel(u_ref, v_ref, za_ref, q_ref, zb_ref, kvd_ref, kv32_ref, ckt_ref, cvt_ref,
                          lng_ref, lnb_ref, coef_ref, bs_ref, sinks_ref,
                          a_ref, b_ref, vn_ref, kwin_ref, vwin_ref):
    ns = SEQS_PER_STEP
    split = lambda x: x.reshape(ns, DEC_SEQ, x.shape[-1])

    vn = _layernorm_v(v_ref, lng_ref, lnb_ref)
    vn_ref[...] = vn
    vn3 = split(vn)
    t_row = lax.broadcasted_iota(jnp.int32, (DEC_SEQ, A_WIDTH), 0)
    s_acc = jnp.broadcast_to(bs_ref[...][None], (ns, DEC_SEQ, A_WIDTH))
    for s in range(DEC_SEQ):
        coef = jnp.where(t_row >= s, coef_ref[s], 0.0)
        s_acc = s_acc + vn3[:, s:s + 1, :] * coef[None]
    a3 = split(u_ref[...].astype(F32)) * s_acc * split(za_ref[...].astype(F32))
    a_ref[...] = a3.reshape(CHUNK, A_WIDTH).astype(BF16)

    rows_q = Q_PER_KV * DEC_SEQ
    lane3 = lax.broadcasted_iota(jnp.int32, (ns, DEC_SEQ, LANES), 2)
    low3 = lane3 < HEAD_DIM
    r_idx = lax.broadcasted_iota(jnp.int32, (ns, rows_q, LANES), 1)
    l_idx = lax.broadcasted_iota(jnp.int32, (ns, rows_q, LANES), 2)
    s_idx = lax.broadcasted_iota(jnp.int32, (ns, rows_q, LANES), 0)
    t_q = r_idx & (DEC_SEQ - 1)
    mask_cache = l_idx > t_q
    mask_new = ((l_idx >> 3) == s_idx) & ((l_idx & (DEC_SEQ - 1)) <= t_q)
    head_of_row = lax.broadcasted_iota(jnp.int32, (1, rows_q, 1), 1) >> 3
    q3 = split(q_ref[...].astype(F32))
    zb3 = split(zb_ref[...].astype(F32))
    for kh in range(N_KV_HEADS):
        pieces = []
        for m in (2 * kh, 2 * kh + 1):
            q_m = q3[:, :, m * LANES:(m + 1) * LANES]
            pieces += [jnp.where(low3, q_m, 0.0), jnp.where(low3, 0.0, q_m)]
        lhs = jnp.concatenate(pieces, axis=1).astype(BF16)
        rows = slice(kh * HEAD_DIM, (kh + 1) * HEAD_DIM)
        kt = ckt_ref[:, rows, :].astype(BF16)
        vt = cvt_ref[:, rows, :].astype(BF16)
        kt2 = jnp.concatenate([kt, kt], axis=1)
        vt2 = jnp.concatenate([vt, vt], axis=1)
        k_new = kvd_ref[:, kh * LANES:(kh + 1) * LANES]
        v_new = kvd_ref[:, KVD_WIDTH // 2 + kh * LANES:KVD_WIDTH // 2 + (kh + 1) * LANES]
        s_c = lax.dot_general(lhs, kt2, (((2,), (1,)), ((0,), (0,))), preferred_element_type=F32)
        s_n = lax.dot_general(lhs.reshape(ns * rows_q, LANES), k_new, (((1,), (1,)), ((), ())),
                              preferred_element_type=F32).reshape(ns, rows_q, LANES)
        s_c = jnp.where(mask_cache, s_c, -jnp.inf)
        s_n = jnp.where(mask_new, s_n, -jnp.inf)
        sink = jnp.zeros((1, rows_q, 1), F32)
        for jq in range(Q_PER_KV):
            sink = jnp.where(head_of_row == jq, sinks_ref[Q_PER_KV * kh + jq], sink)
        mx = jnp.maximum(jnp.maximum(jnp.max(s_c, axis=-1, keepdims=True),
                                     jnp.max(s_n, axis=-1, keepdims=True)), sink)
        p_c = jnp.exp(s_c - mx)
        p_n = jnp.exp(s_n - mx)
        denom = (jnp.sum(p_c, axis=-1, keepdims=True) + jnp.sum(p_n, axis=-1, keepdims=True)
                 + jnp.exp(sink - mx))
        o = lax.dot_general(p_c.astype(BF16), vt2, (((2,), (2,)), ((0,), (0,))),
                            preferred_element_type=F32)
        o = o + jnp.dot(p_n.reshape(ns * rows_q, LANES).astype(BF16), v_new,
                        preferred_element_type=F32).reshape(ns, rows_q, LANES)
        o = o * (1.0 / denom)
        for i, m in enumerate((2 * kh, 2 * kh + 1)):
            base = 2 * DEC_SEQ * i
            o_m = jnp.where(low3, o[:, base:base + DEC_SEQ, :], o[:, base + DEC_SEQ:base + 2 * DEC_SEQ, :])
            cols = slice(m * LANES, (m + 1) * LANES)
            b_ref[:, cols] = (o_m * zb3[:, :, cols]).reshape(CHUNK, LANES).astype(BF16)

    keep = WINDOW - DEC_SEQ
    lane_w = lax.broadcasted_iota(jnp.int32, (KV_WIDTH, WINDOW), 1)
    for new_rows, cache_ref, out_ref in ((kv32_ref[:, :KV_WIDTH], ckt_ref, kwin_ref),
                                         (kv32_ref[:, KV_WIDTH:], cvt_ref, vwin_ref)):
        new_t = new_rows.T
        old = pltpu.roll(cache_ref[...].reshape(ns * KV_WIDTH, WINDOW), keep, axis=1)
        for s in range(ns):
            shifted_new = pltpu.roll(new_t, (keep - DEC_SEQ * s) % WINDOW, axis=1)
            out_ref[s] = jnp.where(lane_w < keep, old[s * KV_WIDTH:(s + 1) * KV_WIDTH], shifted_new)


def _sample_mixers(z, kvd, kv32, cache_kt, cache_vt, ln_g, ln_b, coef, b_s8, sinks):
    n = z.shape[0]
    n_seq = cache_kt.shape[0]
    wide = lambda cb: pl.BlockSpec((CHUNK, A_WIDTH), lambda r, cb=cb: (r, cb))
    const2 = lambda shape: pl.BlockSpec(shape, lambda r: (0, 0))
    win = pl.BlockSpec((SEQS_PER_STEP, KV_WIDTH, WINDOW), lambda r: (r, 0, 0))
    return pl.pallas_call(
        _sample_mixers_kernel,
        grid=(n // CHUNK,),
        in_specs=[
            wide(COL_U), wide(COL_V), wide(COL_ZA), wide(COL_Q), wide(COL_ZB),
            pl.BlockSpec((CHUNK, KVD_WIDTH), lambda r: (r, 0)),
            pl.BlockSpec((CHUNK, 2 * KV_WIDTH), lambda r: (r, 0)),
            win, win,
            const2((1, A_WIDTH)), const2((1, A_WIDTH)),
            pl.BlockSpec((DEC_SEQ, DEC_SEQ, A_WIDTH), lambda r: (0, 0, 0)),
            const2((DEC_SEQ, A_WIDTH)),
            pl.BlockSpec(memory_space=pltpu.SMEM),
        ],
        out_specs=[
            pl.BlockSpec((CHUNK, A_WIDTH), lambda r: (r, 0)),
            pl.BlockSpec((CHUNK, B_WIDTH), lambda r: (r, 0)),
            pl.BlockSpec((CHUNK, A_WIDTH), lambda r: (r, 0)),
            win, win,
        ],
        out_shape=[
            jax.ShapeDtypeStruct((n, A_WIDTH), BF16),
            jax.ShapeDtypeStruct((n, B_WIDTH), BF16),
            jax.ShapeDtypeStruct((n, A_WIDTH), F32),
            jax.ShapeDtypeStruct((n_seq, KV_WIDTH, WINDOW), F32),
            jax.ShapeDtypeStruct((n_seq, KV_WIDTH, WINDOW), F32),
        ],
        compiler_params=pltpu.CompilerParams(
            dimension_semantics=("parallel",), vmem_limit_bytes=VMEM_LIMIT),
        name="sample_mixers",
    )(z, z, z, z, z, kvd, kv32, cache_kt, cache_vt, ln_g, ln_b, coef, b_s8, sinks)


def _merge_out_kernel(x_ref, a_ref, b_ref, ga_ref, gb_ref, wa_ref, wb_ref, wo_ref, g_ref, y_ref):
    y_ref[...] = _merge(x_ref[...], a_ref[...], b_ref[...], ga_ref[...], gb_ref[...],
                        wa_ref, wb_ref, wo_ref, g_ref)


def _merge_out(x2d, a, b, z, w_a, w_b, w_o, g_post):
    n = x2d.shape[0]
    tm = MERGE_ROWS
    resident = lambda shape: pl.BlockSpec(shape, lambda i: (0, 0), pipeline_mode=pl.Buffered(1))
    return pl.pallas_call(
        _merge_out_kernel,
        grid=(n // tm,),
        in_specs=[
            pl.BlockSpec((tm, D_MODEL), lambda i: (i, 0)),
            pl.BlockSpec((tm, A_WIDTH), lambda i: (i, 0)),
            pl.BlockSpec((tm, B_WIDTH), lambda i: (i, 0)),
            pl.BlockSpec((tm, D_MODEL), lambda i: (i, 0)),
            pl.BlockSpec((tm, D_MODEL), lambda i: (i, 1)),
            resident((A_WIDTH, D_MODEL)),
            resident((B_WIDTH, D_MODEL)),
            resident((D_MODEL, D_MODEL)),
            resident((1, D_MODEL)),
        ],
        out_specs=pl.BlockSpec((tm, D_MODEL), lambda i: (i, 0)),
        out_shape=jax.ShapeDtypeStruct((n, D_MODEL), F32),
        compiler_params=pltpu.CompilerParams(
            dimension_semantics=("parallel",), vmem_limit_bytes=VMEM_LIMIT),
        name="merge_out",
    )(x2d, a, b, z, z, w_a, w_b, w_o, g_post)


def _rope_tables(pos):
    lane = jnp.arange(LANES)
    inv = ROPE_THETA ** (-(2 * (lane % (HEAD_DIM // 2))).astype(F32) / HEAD_DIM)
    ang = pos.astype(F32)[:, None] * inv[None, :]
    sign = jnp.where((lane % HEAD_DIM) < HEAD_DIM // 2, -1.0, 1.0).astype(F32)
    return jnp.cos(ang), jnp.sin(ang) * sign[None, :]


def _window_first(win):
    n = win.shape[1]
    return jnp.transpose(win[0], (0, 2, 3, 1)).reshape(n, KV_WIDTH, WINDOW)


def _window_last(win_t):
    n = win_t.shape[0]
    return jnp.transpose(win_t.reshape(n, N_KV_HEADS, HEAD_DIM, WINDOW), (0, 3, 1, 2))[None]


def kernel(x_prompt, x_sample, cache_k_win, cache_v_win, g_pre, w_in, ln_v_g, ln_v_b, w_spatial,
           b_spatial, sinks, w_proj_a, w_proj_b, w_out, g_post):
    bsz, seq, _ = x_prompt.shape
    dbsz, dseq, _ = x_sample.shape
    assert seq == SEQ and dseq == DEC_SEQ and seq % IN_PROJ_ROWS == 0
    assert (dbsz * dseq) % IN_PROJ_ROWS == 0 and g_pre.shape[0] == 1

    w_in_b = w_in[0].astype(BF16)
    w_a = w_proj_a[0].astype(BF16)
    w_b = w_proj_b[0].astype(BF16)
    w_o = w_out[0].astype(BF16)
    b_s_rows = jnp.repeat(b_spatial[0].T, LANES, axis=1)
    coef = jnp.repeat(jnp.transpose(w_spatial[0][:, :DEC_SEQ, :DEC_SEQ], (2, 1, 0)), LANES, axis=2)
    cos_p, sin_p = _rope_tables(jnp.arange(SEQ))
    cos_s, sin_s = _rope_tables(PAST_LEN + jnp.arange(dseq))

    xp = x_prompt.reshape(bsz * seq, D_MODEL)
    z_p, kvd_p, kv32_p = _in_proj(xp, g_pre, w_in_b, cos_p, sin_p, IN_PROJ_ROWS, 2, 256)
    y_p = _prompt_mix_merge(xp, z_p, kvd_p, ln_v_g, ln_v_b, w_spatial[0], b_s_rows, sinks[0],
                            w_a, w_b, w_o, g_post)
    kv_win_p = kv32_p.reshape(bsz, seq, 2 * KV_WIDTH)[:, seq - WINDOW:]
    k_win_p = kv_win_p[..., :KV_WIDTH].reshape(1, bsz, WINDOW, N_KV_HEADS, HEAD_DIM)
    v_win_p = kv_win_p[..., KV_WIDTH:].reshape(1, bsz, WINDOW, N_KV_HEADS, HEAD_DIM)

    xs = x_sample.reshape(dbsz * dseq, D_MODEL)
    z_s, kvd_s, kv32_s = _in_proj(xs, g_pre, w_in_b, cos_s, sin_s, dseq, 1, 512)
    a_s, b_s, vn_s, k_win_t, v_win_t = _sample_mixers(
        z_s, kvd_s, kv32_s, _window_first(cache_k_win), _window_first(cache_v_win),
        ln_v_g, ln_v_b, coef, b_s_rows[:dseq], sinks[0])
    y_s = _merge_out(xs, a_s, b_s, z_s, w_a, w_b, w_o, g_post)

    return (y_p.reshape(bsz, seq, D_MODEL),
            y_s.reshape(dbsz, dseq, D_MODEL),
            k_win_p, v_win_p,
            _window_last(k_win_t), _window_last(v_win_t),
            vn_s.reshape(1, dbsz, dseq, A_WIDTH))
```

```python
import functools

import numpy as np
import jax
import jax.numpy as jnp
from jax import lax
from jax.experimental import pallas as pl
from jax.experimental.pallas import tpu as pltpu

D_MODEL = 2048
SEQ = 2048
DEC_SEQ = 8
PAST_LEN = 8192
CHUNK = 128
A_WIDTH = 1024
A_GROUPS = 8
HEAD_DIM = 64
N_HEADS = 16
N_KV_HEADS = 4
Q_PER_KV = N_HEADS // N_KV_HEADS
B_WIDTH = N_HEADS * HEAD_DIM
KV_WIDTH = N_KV_HEADS * HEAD_DIM
WINDOW = 128
ROPE_THETA = 10000.0
EPS = 1e-6
IN_COLS = 3 * A_WIDTH + 2 * B_WIDTH + 2 * KV_WIDTH + 2 * D_MODEL

LANES = 128
SUBLANES = 8
BF16 = jnp.bfloat16
F32 = jnp.float32

COL_TILE = 512
N_COL_TILES = IN_COLS // COL_TILE
Z_BLOCK = 2 * COL_TILE
Z_COLS = (N_COL_TILES - 1) * COL_TILE
COL_GA, COL_GB, COL_U, COL_V, COL_ZA, COL_Q, COL_ZB = 0, 2, 4, 5, 6, 7, 8
KVD_WIDTH = 2 * N_KV_HEADS * LANES
SEC_GATE_NORM, SEC_GATE, SEC_GELU, SEC_SILU, SEC_Q, SEC_KV = range(6)


def _walk(tiles_per_step):
    sections = [((11, 12, 13, 14, 15, 16, 17, 18), SEC_GATE), ((0, 1, 2, 3), SEC_GELU),
                ((4, 5), SEC_SILU), ((6, 7), SEC_Q), ((9, 10), SEC_SILU)]
    steps = [(tiles[k:k + tiles_per_step], sec)
             for tiles, sec in sections for k in range(0, len(tiles), tiles_per_step)]
    steps[0] = (steps[0][0], SEC_GATE_NORM)
    steps.append(((8,) * tiles_per_step, SEC_KV))
    return steps


def _lookup(values, j):
    out = jnp.int32(values[-1])
    for k in range(len(values) - 2, -1, -1):
        out = jnp.where(j == k, jnp.int32(values[k]), out)
    return out


IN_PROJ_ROWS = 1024
MERGE_ROWS = 512
VMEM_LIMIT = 56 * 1024 * 1024


def _gelu(x):
    return 0.5 * x * (1.0 + lax.erf(x * np.float32(1.0 / np.sqrt(2.0))))


def _sigmoid(x):
    return 1.0 / (1.0 + jnp.exp(-x))


def _rope(x, cos, sin_signed):
    width = x.shape[1]
    lane = lax.broadcasted_iota(jnp.int32, x.shape, 1)
    first_half = (lane & (HEAD_DIM - 1)) < (HEAD_DIM // 2)
    partner = jnp.where(first_half,
                        pltpu.roll(x, width - HEAD_DIM // 2, axis=1),
                        pltpu.roll(x, HEAD_DIM // 2, axis=1))
    reps = width // LANES
    cos_w = jnp.concatenate([cos] * reps, axis=1) if reps > 1 else cos
    sin_w = jnp.concatenate([sin_signed] * reps, axis=1) if reps > 1 else sin_signed
    return x * cos_w + partner * sin_w


def _dup_heads(x):
    lane = lax.broadcasted_iota(jnp.int32, (x.shape[0], LANES), 1)
    low = lane < HEAD_DIM
    out = []
    for c in range(x.shape[1] // LANES):
        xc = x[:, c * LANES:(c + 1) * LANES]
        sw = pltpu.roll(xc, HEAD_DIM, axis=1)
        out += [jnp.where(low, xc, sw), jnp.where(low, sw, xc)]
    return jnp.concatenate(out, axis=1)


def _in_proj_kernel(x_ref, g_ref, *refs, table_rows, sub, walk):
    n_w = len(walk[0][0])
    w_refs = refs[:n_w]
    cos_ref, sin_ref, z_ref, kvd_ref, kv32_ref, h_scr = refs[n_w:]
    j = pl.program_id(1)
    tm = x_ref.shape[0]

    def tables(rs):
        if table_rows == tm:
            return cos_ref[rs, :], sin_ref[rs, :]
        tile = lambda t: jnp.broadcast_to(t[None], (sub // table_rows, table_rows, LANES)).reshape(sub, LANES)
        return tile(cos_ref[...]), tile(sin_ref[...])

    def run(epilogue, norm=False, n_tiles=n_w):
        for r in range(tm // sub):
            rs = slice(r * sub, (r + 1) * sub)
            if norm:
                x = x_ref[rs, :]
                ms = jnp.mean(x * x, axis=-1, keepdims=True)
                h = (x * lax.rsqrt(ms + EPS) * g_ref[...]).astype(BF16)
                h_scr[rs, :] = h
            else:
                h = h_scr[rs, :]
            for t in range(n_tiles):
                acc = jnp.dot(h, w_refs[t][...], preferred_element_type=F32)
                epilogue(acc, rs, slice(t * COL_TILE, (t + 1) * COL_TILE))

    def gate_epi(acc, rs, cs):
        z_ref[rs, cs] = _sigmoid(acc).astype(BF16)

    def gelu_epi(acc, rs, cs):
        z_ref[rs, cs] = _gelu(acc).astype(BF16)

    def silu_epi(acc, rs, cs):
        z_ref[rs, cs] = (acc * _sigmoid(acc)).astype(BF16)

    def q_epi(acc, rs, cs):
        cos, sin = tables(rs)
        z_ref[rs, cs] = (_rope(acc, cos, sin) * np.float32(HEAD_DIM ** -0.5)).astype(BF16)

    def kv_epi(acc, rs, cs):
        cos, sin = tables(rs)
        k = _rope(acc[:, :KV_WIDTH], cos, sin)
        v = acc[:, KV_WIDTH:]
        kv32_ref[rs, :KV_WIDTH] = k
        kv32_ref[rs, KV_WIDTH:] = v
        kvd_ref[rs, :KVD_WIDTH // 2] = _dup_heads(k).astype(BF16)
        kvd_ref[rs, KVD_WIDTH // 2:] = _dup_heads(v).astype(BF16)

    sec = _lookup(tuple(s for _, s in walk), j)
    pl.when(sec == SEC_GATE_NORM)(lambda: run(gate_epi, norm=True))
    pl.when(sec == SEC_GATE)(lambda: run(gate_epi))
    pl.when(sec == SEC_GELU)(lambda: run(gelu_epi))
    pl.when(sec == SEC_SILU)(lambda: run(silu_epi))
    pl.when(sec == SEC_Q)(lambda: run(q_epi))
    pl.when(sec == SEC_KV)(lambda: run(kv_epi, n_tiles=1))


def _in_proj(x2d, g_pre, w_in_b, cos_t, sin_t, table_rows, tiles_per_step, sub):
    n = x2d.shape[0]
    tm = IN_PROJ_ROWS
    walk = _walk(tiles_per_step)
    n_steps = len(walk)
    if table_rows == tm:
        n_tab = cos_t.shape[0] // tm
        table_spec = pl.BlockSpec((tm, LANES), lambda i, j: (i % n_tab, 0))
    else:
        table_spec = pl.BlockSpec((table_rows, LANES), lambda i, j: (0, 0))
    w_specs = [pl.BlockSpec((None, D_MODEL, COL_TILE),
                            lambda i, j, t=t: (_lookup(tuple(tiles[t] for tiles, _ in walk), j), 0, 0))
               for t in range(tiles_per_step)]
    steps_per_block = Z_BLOCK // (tiles_per_step * COL_TILE)

    def z_index(i, j):
        s = jnp.minimum(j, n_steps - 2)
        return (s // steps_per_block, i, s % steps_per_block)

    return pl.pallas_call(
        functools.partial(_in_proj_kernel, table_rows=table_rows, sub=sub, walk=walk),
        grid=(n // tm, n_steps),
        in_specs=[
            pl.BlockSpec((tm, D_MODEL), lambda i, j: (i, 0)),
            pl.BlockSpec((1, D_MODEL), lambda i, j: (0, 0)),
            *w_specs,
            table_spec, table_spec,
        ],
        out_specs=[
            pl.BlockSpec((None, tm, tiles_per_step * COL_TILE), z_index),
            pl.BlockSpec((tm, KVD_WIDTH), lambda i, j: (i, 0)),
            pl.BlockSpec((tm, 2 * KV_WIDTH), lambda i, j: (i, 0)),
        ],
        out_shape=[
            jax.ShapeDtypeStruct((Z_COLS // Z_BLOCK, n, Z_BLOCK), BF16),
            jax.ShapeDtypeStruct((n, KVD_WIDTH), BF16),
            jax.ShapeDtypeStruct((n, 2 * KV_WIDTH), F32),
        ],
        scratch_shapes=[pltpu.VMEM((tm, D_MODEL), BF16)],
        compiler_params=pltpu.CompilerParams(
            dimension_semantics=("parallel", "arbitrary"), vmem_limit_bytes=VMEM_LIMIT),
        name="in_proj",
    )(x2d, g_pre, *([w_in_b] * tiles_per_step), cos_t, sin_t)


def _layernorm_v(v_ref, lng_ref, lnb_ref):
    vg = v_ref[...].astype(F32)
    mu = jnp.mean(vg, axis=-1, keepdims=True)
    xc = vg - mu
    var = jnp.mean(xc * xc, axis=-1, keepdims=True)
    return xc * lax.rsqrt(var + EPS) * lng_ref[...] + lnb_ref[...]


def _mix_block(u, v, za, q, zb, kvd, kvd_prev, first_block, lng, lnb, ws_ref, bs_ref, sinks_ref,
               store_a, store_b):
    row = lax.broadcasted_iota(jnp.int32, (CHUNK, CHUNK), 0)
    col = lax.broadcasted_iota(jnp.int32, (CHUNK, CHUNK), 1)

    vg = v.astype(F32)
    mu = jnp.mean(vg, axis=-1, keepdims=True)
    xc = vg - mu
    var = jnp.mean(xc * xc, axis=-1, keepdims=True)
    vn_b = (xc * lax.rsqrt(var + EPS) * lng + lnb).astype(BF16)
    for g in range(A_GROUPS):
        cols = slice(g * LANES, (g + 1) * LANES)
        w = jnp.where(col <= row, ws_ref[g], 0.0).astype(BF16)
        s = jnp.dot(w, vn_b[:, cols], preferred_element_type=F32) + bs_ref[:, cols]
        store_a(cols, ((u[:, cols].astype(F32) * s) * za[:, cols].astype(F32)).astype(BF16))

    t = lax.broadcasted_iota(jnp.int32, (WINDOW, 2 * WINDOW), 0)
    jj = lax.broadcasted_iota(jnp.int32, (WINDOW, 2 * WINDOW), 1)
    mask = (jj > t) & (jj <= t + WINDOW) & ((jj >= WINDOW) | jnp.logical_not(first_block))
    lane = lax.broadcasted_iota(jnp.int32, (WINDOW, LANES), 1)
    low_half = lane < HEAD_DIM
    for m in range(N_HEADS // 2):
        cols = slice(m * LANES, (m + 1) * LANES)
        kcols = slice((m // 2) * LANES, (m // 2 + 1) * LANES)
        vcols = slice(KVD_WIDTH // 2 + (m // 2) * LANES, KVD_WIDTH // 2 + (m // 2 + 1) * LANES)
        k_c = jnp.concatenate([kvd_prev[:, kcols], kvd[:, kcols]], axis=0)
        v_c = jnp.concatenate([kvd_prev[:, vcols], kvd[:, vcols]], axis=0)
        q_m = q[:, cols]
        outs = []
        for half in range(2):
            sink = sinks_ref[2 * m + half]
            q_h = jnp.where(low_half if half == 0 else ~low_half, q_m, jnp.zeros_like(q_m))
            s = lax.dot_general(q_h, k_c, (((1,), (1,)), ((), ())), preferred_element_type=F32)
            s = jnp.where(mask, s, -jnp.inf)
            mx = jnp.maximum(jnp.max(s, axis=-1, keepdims=True), sink)
            p = jnp.exp(s - mx)
            denom = jnp.sum(p, axis=-1, keepdims=True) + jnp.exp(sink - mx)
            o = jnp.dot(p.astype(BF16), v_c, preferred_element_type=F32)
            outs.append(o * (1.0 / denom))
        o_m = jnp.where(low_half, outs[0], outs[1])
        store_b(cols, (o_m * zb[:, cols].astype(F32)).astype(BF16))


def _merge(x, a, b, ga_ref, gb_ref, wa_ref, wb_ref, wo_ref, g_ref):
    gate = lambda ref: jnp.concatenate([ref[c] for c in range(ref.shape[0])], axis=1).astype(F32)
    pa = jnp.dot(a, wa_ref[...], preferred_element_type=F32)
    pb = jnp.dot(b, wb_ref[...], preferred_element_type=F32)
    merged = gate(ga_ref) * pa + gate(gb_ref) * pb
    out = jnp.dot(merged.astype(BF16), wo_ref[...], preferred_element_type=F32)
    ms = jnp.mean(out * out, axis=-1, keepdims=True)
    return x + out * lax.rsqrt(ms + EPS) * g_ref[...]


MIX_ROWS = 256
MIX_BLOCKS = MIX_ROWS // CHUNK


def _prompt_mix_merge_kernel(u_ref, v_ref, za_ref, q_ref, zb_ref, kvd_ref, kvdp_ref, x_ref, ga_ref,
                             gb_ref, lng_ref, lnb_ref, ws_ref, bs_ref, sinks_ref, wa_ref, wb_ref,
                             wo_ref, g_ref, y_ref, a_scr, b_scr):
    i = pl.program_id(0)
    last_tile = pl.num_programs(0) - 2
    tile = jnp.minimum(i, last_tile)
    slot = i % 2

    @pl.when(i == 0)
    def _():
        a_scr[1] = jnp.zeros(a_scr.shape[1:], BF16)
        b_scr[1] = jnp.zeros(b_scr.shape[1:], BF16)

    for blk in range(MIX_BLOCKS):
        rows = slice(blk * CHUNK, (blk + 1) * CHUNK)
        first_block = ((tile * MIX_BLOCKS + blk) % (SEQ // WINDOW)) == 0
        kvd_prev = kvdp_ref[...] if blk == 0 else kvd_ref[(blk - 1) * CHUNK:blk * CHUNK, :]

        def store_a(cols, val, rows=rows):
            a_scr[slot, rows, cols] = val

        def store_b(cols, val, rows=rows):
            b_scr[slot, rows, cols] = val

        _mix_block(u_ref[rows, :], v_ref[rows, :], za_ref[rows, :], q_ref[rows, :], zb_ref[rows, :],
                   kvd_ref[rows, :], kvd_prev, first_block, lng_ref[...], lnb_ref[...], ws_ref, bs_ref,
                   sinks_ref, store_a, store_b)

    y_ref[...] = _merge(x_ref[...], a_scr[1 - slot], b_scr[1 - slot], ga_ref, gb_ref,
                        wa_ref, wb_ref, wo_ref, g_ref)


def _prompt_mix_merge(x2d, z, kvd, ln_g, ln_b, w_s, b_s_rows, sinks, w_a, w_b, w_o, g_post):
    n = x2d.shape[0]
    tm = MIX_ROWS
    n_tiles = n // tm
    mix_tile = lambda i: jnp.minimum(i, n_tiles - 1)
    merge_tile = lambda i: jnp.maximum(i - 1, 0)
    wide = lambda cb: pl.BlockSpec((None, tm, Z_BLOCK), lambda i, cb=cb: (cb, mix_tile(i), 0))
    gate = lambda cb: pl.BlockSpec((D_MODEL // Z_BLOCK, tm, Z_BLOCK),
                                   lambda i, cb=cb: (cb // (D_MODEL // Z_BLOCK), merge_tile(i), 0))
    const2 = lambda shape: pl.BlockSpec(shape, lambda i: (0, 0))
    resident = lambda shape: pl.BlockSpec(shape, lambda i: (0, 0), pipeline_mode=pl.Buffered(1))
    return pl.pallas_call(
        _prompt_mix_merge_kernel,
        grid=(n_tiles + 1,),
        in_specs=[
            wide(COL_U), wide(COL_V), wide(COL_ZA), wide(COL_Q), wide(COL_ZB),
            pl.BlockSpec((tm, KVD_WIDTH), lambda i: (mix_tile(i), 0)),
            pl.BlockSpec((CHUNK, KVD_WIDTH), lambda i: (jnp.maximum(mix_tile(i) * MIX_BLOCKS - 1, 0), 0)),
            pl.BlockSpec((tm, D_MODEL), lambda i: (merge_tile(i), 0)),
            gate(COL_GA), gate(COL_GB),
            const2((1, A_WIDTH)), const2((1, A_WIDTH)),
            pl.BlockSpec((A_GROUPS, CHUNK, CHUNK), lambda i: (0, 0, 0)),
            const2((CHUNK, A_WIDTH)),
            pl.BlockSpec(memory_space=pltpu.SMEM),
            resident((A_WIDTH, D_MODEL)),
            resident((B_WIDTH, D_MODEL)),
            resident((D_MODEL, D_MODEL)),
            resident((1, D_MODEL)),
        ],
        out_specs=pl.BlockSpec((tm, D_MODEL), lambda i: (merge_tile(i), 0)),
        out_shape=jax.ShapeDtypeStruct((n, D_MODEL), F32),
        scratch_shapes=[pltpu.VMEM((2, tm, A_WIDTH), BF16), pltpu.VMEM((2, tm, B_WIDTH), BF16)],
        compiler_params=pltpu.CompilerParams(
            dimension_semantics=("arbitrary",), vmem_limit_bytes=VMEM_LIMIT),
        name="prompt_mix_merge",
    )(z, z, z, z, z, kvd, kvd, x2d, z, z, ln_g, ln_b, w_s, b_s_rows, sinks, w_a, w_b, w_o, g_post)


SEQS_PER_STEP = CHUNK // DEC_SEQ


def _sample_mixers_kernel(u_ref, v_ref, za_ref, q_ref, zb_ref, kvd_ref, kv32_ref, ckt_ref, cvt_ref,
                          lng_ref, lnb_ref, coef_ref, bs_ref, sinks_ref,
                          a_ref, b_ref, vn_ref, kwin_ref, vwin_ref):
    ns = SEQS_PER_STEP
    split = lambda x: x.reshape(ns, DEC_SEQ, x.shape[-1])

    vn = _layernorm_v(v_ref, lng_ref, lnb_ref)
    vn_ref[...] = vn
    vn3 = split(vn)
    t_row = lax.broadcasted_iota(jnp.int32, (DEC_SEQ, A_WIDTH), 0)
    s_acc = jnp.broadcast_to(bs_ref[...][None], (ns, DEC_SEQ, A_WIDTH))
    for s in range(DEC_SEQ):
        coef = jnp.where(t_row >= s, coef_ref[s], 0.0)
        s_acc = s_acc + vn3[:, s:s + 1, :] * coef[None]
    a3 = split(u_ref[...].astype(F32)) * s_acc * split(za_ref[...].astype(F32))
    a_ref[...] = a3.reshape(CHUNK, A_WIDTH).astype(BF16)

    rows_q = Q_PER_KV * DEC_SEQ
    lane3 = lax.broadcasted_iota(jnp.int32, (ns, DEC_SEQ, LANES), 2)
    low3 = lane3 < HEAD_DIM
    r_idx = lax.broadcasted_iota(jnp.int32, (ns, rows_q, LANES), 1)
    l_idx = lax.broadcasted_iota(jnp.int32, (ns, rows_q, LANES), 2)
    s_idx = lax.broadcasted_iota(jnp.int32, (ns, rows_q, LANES), 0)
    t_q = r_idx & (DEC_SEQ - 1)
    mask_cache = l_idx > t_q
    mask_new = ((l_idx >> 3) == s_idx) & ((l_idx & (DEC_SEQ - 1)) <= t_q)
    head_of_row = lax.broadcasted_iota(jnp.int32, (1, rows_q, 1), 1) >> 3
    q3 = split(q_ref[...].astype(F32))
    zb3 = split(zb_ref[...].astype(F32))
    for kh in range(N_KV_HEADS):
        pieces = []
        for m in (2 * kh, 2 * kh + 1):
            q_m = q3[:, :, m * LANES:(m + 1) * LANES]
            pieces += [jnp.where(low3, q_m, 0.0), jnp.where(low3, 0.0, q_m)]
        lhs = jnp.concatenate(pieces, axis=1).astype(BF16)
        rows = slice(kh * HEAD_DIM, (kh + 1) * HEAD_DIM)
        kt = ckt_ref[:, rows, :].astype(BF16)
        vt = cvt_ref[:, rows, :].astype(BF16)
        kt2 = jnp.concatenate([kt, kt], axis=1)
        vt2 = jnp.concatenate([vt, vt], axis=1)
        k_new = kvd_ref[:, kh * LANES:(kh + 1) * LANES]
        v_new = kvd_ref[:, KVD_WIDTH // 2 + kh * LANES:KVD_WIDTH // 2 + (kh + 1) * LANES]
        s_c = lax.dot_general(lhs, kt2, (((2,), (1,)), ((0,), (0,))), preferred_element_type=F32)
        s_n = lax.dot_general(lhs.reshape(ns * rows_q, LANES), k_new, (((1,), (1,)), ((), ())),
                              preferred_element_type=F32).reshape(ns, rows_q, LANES)
        s_c = jnp.where(mask_cache, s_c, -jnp.inf)
        s_n = jnp.where(mask_new, s_n, -jnp.inf)
        sink = jnp.zeros((1, rows_q, 1), F32)
        for jq in range(Q_PER_KV):
            sink = jnp.where(head_of_row == jq, sinks_ref[Q_PER_KV * kh + jq], sink)
        mx = jnp.maximum(jnp.maximum(jnp.max(s_c, axis=-1, keepdims=True),
                                     jnp.max(s_n, axis=-1, keepdims=True)), sink)
        p_c = jnp.exp(s_c - mx)
        p_n = jnp.exp(s_n - mx)
        denom = (jnp.sum(p_c, axis=-1, keepdims=True) + jnp.sum(p_n, axis=-1, keepdims=True)
                 + jnp.exp(sink - mx))
        o = lax.dot_general(p_c.astype(BF16), vt2, (((2,), (2,)), ((0,), (0,))),
                            preferred_element_type=F32)
        o = o + jnp.dot(p_n.reshape(ns * rows_q, LANES).astype(BF16), v_new,
                        preferred_element_type=F32).reshape(ns, rows_q, LANES)
        o = o * (1.0 / denom)
        for i, m in enumerate((2 * kh, 2 * kh + 1)):
            base = 2 * DEC_SEQ * i
            o_m = jnp.where(low3, o[:, base:base + DEC_SEQ, :], o[:, base + DEC_SEQ:base + 2 * DEC_SEQ, :])
            cols = slice(m * LANES, (m + 1) * LANES)
            b_ref[:, cols] = (o_m * zb3[:, :, cols]).reshape(CHUNK, LANES).astype(BF16)

    keep = WINDOW - DEC_SEQ
    lane_w = lax.broadcasted_iota(jnp.int32, (KV_WIDTH, WINDOW), 1)
    for new_rows, cache_ref, out_ref in ((kv32_ref[:, :KV_WIDTH], ckt_ref, kwin_ref),
                                         (kv32_ref[:, KV_WIDTH:], cvt_ref, vwin_ref)):
        new_t = new_rows.T
        old = pltpu.roll(cache_ref[...].reshape(ns * KV_WIDTH, WINDOW), keep, axis=1)
        for s in range(ns):
            shifted_new = pltpu.roll(new_t, (keep - DEC_SEQ * s) % WINDOW, axis=1)
            out_ref[s] = jnp.where(lane_w < keep, old[s * KV_WIDTH:(s + 1) * KV_WIDTH], shifted_new)


def _sample_mixers(z, kvd, kv32, cache_kt, cache_vt, ln_g, ln_b, coef, b_s8, sinks):
    n = kvd.shape[0]
    n_seq = cache_kt.shape[0]
    wide = lambda cb: pl.BlockSpec((None, CHUNK, Z_BLOCK), lambda r, cb=cb: (cb, r, 0))
    const2 = lambda shape: pl.BlockSpec(shape, lambda r: (0, 0))
    win = pl.BlockSpec((SEQS_PER_STEP, KV_WIDTH, WINDOW), lambda r: (r, 0, 0))
    return pl.pallas_call(
        _sample_mixers_kernel,
        grid=(n // CHUNK,),
        in_specs=[
            wide(COL_U), wide(COL_V), wide(COL_ZA), wide(COL_Q), wide(COL_ZB),
            pl.BlockSpec((CHUNK, KVD_WIDTH), lambda r: (r, 0)),
            pl.BlockSpec((CHUNK, 2 * KV_WIDTH), lambda r: (r, 0)),
            win, win,
            const2((1, A_WIDTH)), const2((1, A_WIDTH)),
            pl.BlockSpec((DEC_SEQ, DEC_SEQ, A_WIDTH), lambda r: (0, 0, 0)),
            const2((DEC_SEQ, A_WIDTH)),
            pl.BlockSpec(memory_space=pltpu.SMEM),
        ],
        out_specs=[
            pl.BlockSpec((CHUNK, A_WIDTH), lambda r: (r, 0)),
            pl.BlockSpec((CHUNK, B_WIDTH), lambda r: (r, 0)),
            pl.BlockSpec((CHUNK, A_WIDTH), lambda r: (r, 0)),
            win, win,
        ],
        out_shape=[
            jax.ShapeDtypeStruct((n, A_WIDTH), BF16),
            jax.ShapeDtypeStruct((n, B_WIDTH), BF16),
            jax.ShapeDtypeStruct((n, A_WIDTH), F32),
            jax.ShapeDtypeStruct((n_seq, KV_WIDTH, WINDOW), F32),
            jax.ShapeDtypeStruct((n_seq, KV_WIDTH, WINDOW), F32),
        ],
        compiler_params=pltpu.CompilerParams(
            dimension_semantics=("parallel",), vmem_limit_bytes=VMEM_LIMIT),
        name="sample_mixers",
    )(z, z, z, z, z, kvd, kv32, cache_kt, cache_vt, ln_g, ln_b, coef, b_s8, sinks)


def _merge_out_kernel(x_ref, a_ref, b_ref, ga_ref, gb_ref, wa_ref, wb_ref, wo_ref, g_ref, y_ref):
    y_ref[...] = _merge(x_ref[...], a_ref[...], b_ref[...], ga_ref, gb_ref,
                        wa_ref, wb_ref, wo_ref, g_ref)


def _merge_out(x2d, a, b, z, w_a, w_b, w_o, g_post):
    n = x2d.shape[0]
    tm = MERGE_ROWS
    resident = lambda shape: pl.BlockSpec(shape, lambda i: (0, 0), pipeline_mode=pl.Buffered(1))
    return pl.pallas_call(
        _merge_out_kernel,
        grid=(n // tm,),
        in_specs=[
            pl.BlockSpec((tm, D_MODEL), lambda i: (i, 0)),
            pl.BlockSpec((tm, A_WIDTH), lambda i: (i, 0)),
            pl.BlockSpec((tm, B_WIDTH), lambda i: (i, 0)),
            pl.BlockSpec((D_MODEL // Z_BLOCK, tm, Z_BLOCK), lambda i: (COL_GA // (D_MODEL // Z_BLOCK), i, 0)),
            pl.BlockSpec((D_MODEL // Z_BLOCK, tm, Z_BLOCK), lambda i: (COL_GB // (D_MODEL // Z_BLOCK), i, 0)),
            resident((A_WIDTH, D_MODEL)),
            resident((B_WIDTH, D_MODEL)),
            resident((D_MODEL, D_MODEL)),
            resident((1, D_MODEL)),
        ],
        out_specs=pl.BlockSpec((tm, D_MODEL), lambda i: (i, 0)),
        out_shape=jax.ShapeDtypeStruct((n, D_MODEL), F32),
        compiler_params=pltpu.CompilerParams(
            dimension_semantics=("parallel",), vmem_limit_bytes=VMEM_LIMIT),
        name="merge_out",
    )(x2d, a, b, z, z, w_a, w_b, w_o, g_post)


def _cast_tile_kernel(w_ref, o_ref):
    o_ref[...] = w_ref[...].astype(BF16)


def _cast_column_tiles(w):
    rows, cols = w.shape
    return pl.pallas_call(
        _cast_tile_kernel,
        grid=(cols // COL_TILE,),
        in_specs=[pl.BlockSpec((rows, COL_TILE), lambda t: (0, t))],
        out_specs=pl.BlockSpec((None, rows, COL_TILE), lambda t: (t, 0, 0)),
        out_shape=jax.ShapeDtypeStruct((cols // COL_TILE, rows, COL_TILE), BF16),
        compiler_params=pltpu.CompilerParams(
            dimension_semantics=("parallel",), vmem_limit_bytes=VMEM_LIMIT),
        name="cast_column_tiles",
    )(w)


def _rope_tables(pos):
    lane = jnp.arange(LANES)
    inv = ROPE_THETA ** (-(2 * (lane % (HEAD_DIM // 2))).astype(F32) / HEAD_DIM)
    ang = pos.astype(F32)[:, None] * inv[None, :]
    sign = jnp.where((lane % HEAD_DIM) < HEAD_DIM // 2, -1.0, 1.0).astype(F32)
    return jnp.cos(ang), jnp.sin(ang) * sign[None, :]


def _window_first(win):
    n = win.shape[1]
    return jnp.transpose(win[0], (0, 2, 3, 1)).reshape(n, KV_WIDTH, WINDOW)


def _window_last(win_t):
    n = win_t.shape[0]
    return jnp.transpose(win_t.reshape(n, N_KV_HEADS, HEAD_DIM, WINDOW), (0, 3, 1, 2))[None]


def kernel(x_prompt, x_sample, cache_k_win, cache_v_win, g_pre, w_in, ln_v_g, ln_v_b, w_spatial,
           b_spatial, sinks, w_proj_a, w_proj_b, w_out, g_post):
    bsz, seq, _ = x_prompt.shape
    dbsz, dseq, _ = x_sample.shape
    assert seq == SEQ and dseq == DEC_SEQ and seq % IN_PROJ_ROWS == 0
    assert (dbsz * dseq) % IN_PROJ_ROWS == 0 and g_pre.shape[0] == 1

    w_in_b = _cast_column_tiles(w_in[0])
    w_a = w_proj_a[0].astype(BF16)
    w_b = w_proj_b[0].astype(BF16)
    w_o = w_out[0].astype(BF16)
    b_s_rows = jnp.repeat(b_spatial[0].T, LANES, axis=1)
    coef = jnp.repeat(jnp.transpose(w_spatial[0][:, :DEC_SEQ, :DEC_SEQ], (2, 1, 0)), LANES, axis=2)
    cos_p, sin_p = _rope_tables(jnp.arange(SEQ))
    cos_s, sin_s = _rope_tables(PAST_LEN + jnp.arange(dseq))

    xp = x_prompt.reshape(bsz * seq, D_MODEL)
    z_p, kvd_p, kv32_p = _in_proj(xp, g_pre, w_in_b, cos_p, sin_p, IN_PROJ_ROWS, 2, 256)
    y_p = _prompt_mix_merge(xp, z_p, kvd_p, ln_v_g, ln_v_b, w_spatial[0], b_s_rows, sinks[0],
                            w_a, w_b, w_o, g_post)
    kv_win_p = kv32_p.reshape(bsz, seq, 2 * KV_WIDTH)[:, seq - WINDOW:]
    k_win_p = kv_win_p[..., :KV_WIDTH].reshape(1, bsz, WINDOW, N_KV_HEADS, HEAD_DIM)
    v_win_p = kv_win_p[..., KV_WIDTH:].reshape(1, bsz, WINDOW, N_KV_HEADS, HEAD_DIM)

    xs = x_sample.reshape(dbsz * dseq, D_MODEL)
    z_s, kvd_s, kv32_s = _in_proj(xs, g_pre, w_in_b, cos_s, sin_s, dseq, 1, 512)
    a_s, b_s, vn_s, k_win_t, v_win_t = _sample_mixers(
        z_s, kvd_s, kv32_s, _window_first(cache_k_win), _window_first(cache_v_win),
        ln_v_g, ln_v_b, coef, b_s_rows[:dseq], sinks[0])
    y_s = _merge_out(xs, a_s, b_s, z_s, w_a, w_b, w_o, g_post)

    return (y_p.reshape(bsz, seq, D_MODEL),
            y_s.reshape(dbsz, dseq, D_MODEL),
            k_win_p, v_win_p,
            _window_last(k_win_t), _window_last(v_win_t),
            vn_s.reshape(1, dbsz, dseq, A_WIDTH))
```

```python
import functools

import numpy as np
import jax
import jax.numpy as jnp
from jax import lax
from jax.experimental import pallas as pl
from jax.experimental.pallas import tpu as pltpu

D_MODEL = 2048
SEQ = 2048
DEC_SEQ = 8
PAST_LEN = 8192
CHUNK = 128
A_WIDTH = 1024
A_GROUPS = 8
HEAD_DIM = 64
N_HEADS = 16
N_KV_HEADS = 4
Q_PER_KV = N_HEADS // N_KV_HEADS
B_WIDTH = N_HEADS * HEAD_DIM
KV_WIDTH = N_KV_HEADS * HEAD_DIM
WINDOW = 128
ROPE_THETA = 10000.0
EPS = 1e-6
IN_COLS = 3 * A_WIDTH + 2 * B_WIDTH + 2 * KV_WIDTH + 2 * D_MODEL

LANES = 128
SUBLANES = 8
BF16 = jnp.bfloat16
F32 = jnp.float32

COL_TILE = 512
N_COL_TILES = IN_COLS // COL_TILE
Z_BLOCK = 2 * COL_TILE
Z_COLS = (N_COL_TILES - 1) * COL_TILE
COL_GA, COL_GB, COL_U, COL_V, COL_ZA, COL_Q, COL_ZB = 0, 2, 4, 5, 6, 7, 8
KVD_WIDTH = 2 * N_KV_HEADS * LANES
SEC_GATE_NORM, SEC_GATE, SEC_GELU, SEC_SILU, SEC_Q, SEC_KV = range(6)


def _walk(tiles_per_step):
    sections = [((11, 12, 13, 14, 15, 16, 17, 18), SEC_GATE), ((0, 1, 2, 3), SEC_GELU),
                ((4, 5), SEC_SILU), ((6, 7), SEC_Q), ((9, 10), SEC_SILU)]
    steps = [(tiles[k:k + tiles_per_step], sec)
             for tiles, sec in sections for k in range(0, len(tiles), tiles_per_step)]
    steps[0] = (steps[0][0], SEC_GATE_NORM)
    steps.append(((8,) * tiles_per_step, SEC_KV))
    return steps


def _lookup(values, j):
    out = jnp.int32(values[-1])
    for k in range(len(values) - 2, -1, -1):
        out = jnp.where(j == k, jnp.int32(values[k]), out)
    return out


IN_PROJ_ROWS = 1024
MERGE_ROWS = 512
VMEM_LIMIT = 56 * 1024 * 1024


def _gelu(x):
    return 0.5 * x * (1.0 + lax.erf(x * np.float32(1.0 / np.sqrt(2.0))))


def _sigmoid(x):
    return 1.0 / (1.0 + jnp.exp(-x))


def _rope(x, cos, sin_signed):
    width = x.shape[1]
    lane = lax.broadcasted_iota(jnp.int32, x.shape, 1)
    first_half = (lane & (HEAD_DIM - 1)) < (HEAD_DIM // 2)
    partner = jnp.where(first_half,
                        pltpu.roll(x, width - HEAD_DIM // 2, axis=1),
                        pltpu.roll(x, HEAD_DIM // 2, axis=1))
    reps = width // LANES
    cos_w = jnp.concatenate([cos] * reps, axis=1) if reps > 1 else cos
    sin_w = jnp.concatenate([sin_signed] * reps, axis=1) if reps > 1 else sin_signed
    return x * cos_w + partner * sin_w


def _dup_heads(x):
    lane = lax.broadcasted_iota(jnp.int32, (x.shape[0], LANES), 1)
    low = lane < HEAD_DIM
    out = []
    for c in range(x.shape[1] // LANES):
        xc = x[:, c * LANES:(c + 1) * LANES]
        sw = pltpu.roll(xc, HEAD_DIM, axis=1)
        out += [jnp.where(low, xc, sw), jnp.where(low, sw, xc)]
    return jnp.concatenate(out, axis=1)


def _in_proj_kernel(x_ref, g_ref, *refs, table_rows, sub, walk):
    n_w = len(walk[0][0])
    w_refs = refs[:n_w]
    cos_ref, sin_ref, z_ref, kvd_ref, kv32_ref, h_scr = refs[n_w:]
    j = pl.program_id(1)
    tm = x_ref.shape[0]

    def tables(rs):
        if table_rows == tm:
            return cos_ref[rs, :], sin_ref[rs, :]
        tile = lambda t: jnp.broadcast_to(t[None], (sub // table_rows, table_rows, LANES)).reshape(sub, LANES)
        return tile(cos_ref[...]), tile(sin_ref[...])

    def run(epilogue, norm=False, n_tiles=n_w):
        for r in range(tm // sub):
            rs = slice(r * sub, (r + 1) * sub)
            if norm:
                x = x_ref[rs, :]
                ms = jnp.mean(x * x, axis=-1, keepdims=True)
                h = (x * lax.rsqrt(ms + EPS) * g_ref[...]).astype(BF16)
                h_scr[rs, :] = h
            else:
                h = h_scr[rs, :]
            for t in range(n_tiles):
                acc = jnp.dot(h, w_refs[t][...], preferred_element_type=F32)
                epilogue(acc, rs, slice(t * COL_TILE, (t + 1) * COL_TILE))

    def gate_epi(acc, rs, cs):
        z_ref[rs, cs] = acc.astype(BF16)

    def gelu_epi(acc, rs, cs):
        z_ref[rs, cs] = _gelu(acc).astype(BF16)

    def silu_epi(acc, rs, cs):
        z_ref[rs, cs] = (acc * _sigmoid(acc)).astype(BF16)

    def q_epi(acc, rs, cs):
        cos, sin = tables(rs)
        z_ref[rs, cs] = (_rope(acc, cos, sin) * np.float32(HEAD_DIM ** -0.5)).astype(BF16)

    def kv_epi(acc, rs, cs):
        cos, sin = tables(rs)
        k = _rope(acc[:, :KV_WIDTH], cos, sin)
        v = acc[:, KV_WIDTH:]
        kv32_ref[rs, :KV_WIDTH] = k
        kv32_ref[rs, KV_WIDTH:] = v
        kvd_ref[rs, :KVD_WIDTH // 2] = _dup_heads(k).astype(BF16)
        kvd_ref[rs, KVD_WIDTH // 2:] = _dup_heads(v).astype(BF16)

    sec = _lookup(tuple(s for _, s in walk), j)
    pl.when(sec == SEC_GATE_NORM)(lambda: run(gate_epi, norm=True))
    pl.when(sec == SEC_GATE)(lambda: run(gate_epi))
    pl.when(sec == SEC_GELU)(lambda: run(gelu_epi))
    pl.when(sec == SEC_SILU)(lambda: run(silu_epi))
    pl.when(sec == SEC_Q)(lambda: run(q_epi))
    pl.when(sec == SEC_KV)(lambda: run(kv_epi, n_tiles=1))


def _in_proj(x2d, g_pre, w_in_b, cos_t, sin_t, table_rows, tiles_per_step, sub):
    n = x2d.shape[0]
    tm = IN_PROJ_ROWS
    walk = _walk(tiles_per_step)
    n_steps = len(walk)
    if table_rows == tm:
        n_tab = cos_t.shape[0] // tm
        table_spec = pl.BlockSpec((tm, LANES), lambda i, j: (i % n_tab, 0))
    else:
        table_spec = pl.BlockSpec((table_rows, LANES), lambda i, j: (0, 0))
    w_specs = [pl.BlockSpec((None, D_MODEL, COL_TILE),
                            lambda i, j, t=t: (_lookup(tuple(tiles[t] for tiles, _ in walk), j), 0, 0))
               for t in range(tiles_per_step)]
    steps_per_block = Z_BLOCK // (tiles_per_step * COL_TILE)

    def z_index(i, j):
        s = jnp.minimum(j, n_steps - 2)
        return (s // steps_per_block, i, s % steps_per_block)

    return pl.pallas_call(
        functools.partial(_in_proj_kernel, table_rows=table_rows, sub=sub, walk=walk),
        grid=(n // tm, n_steps),
        in_specs=[
            pl.BlockSpec((tm, D_MODEL), lambda i, j: (i, 0)),
            pl.BlockSpec((1, D_MODEL), lambda i, j: (0, 0)),
            *w_specs,
            table_spec, table_spec,
        ],
        out_specs=[
            pl.BlockSpec((None, tm, tiles_per_step * COL_TILE), z_index),
            pl.BlockSpec((tm, KVD_WIDTH), lambda i, j: (i, 0)),
            pl.BlockSpec((tm, 2 * KV_WIDTH), lambda i, j: (i, 0)),
        ],
        out_shape=[
            jax.ShapeDtypeStruct((Z_COLS // Z_BLOCK, n, Z_BLOCK), BF16),
            jax.ShapeDtypeStruct((n, KVD_WIDTH), BF16),
            jax.ShapeDtypeStruct((n, 2 * KV_WIDTH), F32),
        ],
        scratch_shapes=[pltpu.VMEM((tm, D_MODEL), BF16)],
        compiler_params=pltpu.CompilerParams(
            dimension_semantics=("parallel", "arbitrary"), vmem_limit_bytes=VMEM_LIMIT),
        name="in_proj",
    )(x2d, g_pre, *([w_in_b] * tiles_per_step), cos_t, sin_t)


def _layernorm_v(v_ref, lng_ref, lnb_ref):
    vg = v_ref[...].astype(F32)
    mu = jnp.mean(vg, axis=-1, keepdims=True)
    xc = vg - mu
    var = jnp.mean(xc * xc, axis=-1, keepdims=True)
    return xc * lax.rsqrt(var + EPS) * lng_ref[...] + lnb_ref[...]


def _mix_block(u, v, za, q, zb, kvd, kvd_prev, first_block, lng, lnb, ws_ref, bs_ref, sinks_ref,
               store_a, store_b):
    row = lax.broadcasted_iota(jnp.int32, (CHUNK, CHUNK), 0)
    col = lax.broadcasted_iota(jnp.int32, (CHUNK, CHUNK), 1)

    vg = v.astype(F32)
    mu = jnp.mean(vg, axis=-1, keepdims=True)
    xc = vg - mu
    var = jnp.mean(xc * xc, axis=-1, keepdims=True)
    vn_b = (xc * lax.rsqrt(var + EPS) * lng + lnb).astype(BF16)
    for g in range(A_GROUPS):
        cols = slice(g * LANES, (g + 1) * LANES)
        w = jnp.where(col <= row, ws_ref[g], 0.0).astype(BF16)
        s = jnp.dot(w, vn_b[:, cols], preferred_element_type=F32) + bs_ref[:, cols]
        store_a(cols, ((u[:, cols].astype(F32) * s) * za[:, cols].astype(F32)).astype(BF16))

    t = lax.broadcasted_iota(jnp.int32, (WINDOW, 2 * WINDOW), 0)
    jj = lax.broadcasted_iota(jnp.int32, (WINDOW, 2 * WINDOW), 1)
    mask = (jj > t) & (jj <= t + WINDOW) & ((jj >= WINDOW) | jnp.logical_not(first_block))
    lane = lax.broadcasted_iota(jnp.int32, (WINDOW, LANES), 1)
    low_half = lane < HEAD_DIM
    for m in range(N_HEADS // 2):
        cols = slice(m * LANES, (m + 1) * LANES)
        kcols = slice((m // 2) * LANES, (m // 2 + 1) * LANES)
        vcols = slice(KVD_WIDTH // 2 + (m // 2) * LANES, KVD_WIDTH // 2 + (m // 2 + 1) * LANES)
        k_c = jnp.concatenate([kvd_prev[:, kcols], kvd[:, kcols]], axis=0)
        v_c = jnp.concatenate([kvd_prev[:, vcols], kvd[:, vcols]], axis=0)
        q_m = q[:, cols]
        outs = []
        for half in range(2):
            sink = sinks_ref[2 * m + half]
            q_h = jnp.where(low_half if half == 0 else ~low_half, q_m, jnp.zeros_like(q_m))
            s = lax.dot_general(q_h, k_c, (((1,), (1,)), ((), ())), preferred_element_type=F32)
            s = jnp.where(mask, s, -jnp.inf)
            mx = jnp.maximum(jnp.max(s, axis=-1, keepdims=True), sink)
            p = jnp.exp(s - mx)
            denom = jnp.sum(p, axis=-1, keepdims=True) + jnp.exp(sink - mx)
            o = jnp.dot(p.astype(BF16), v_c, preferred_element_type=F32)
            outs.append(o * (1.0 / denom))
        o_m = jnp.where(low_half, outs[0], outs[1])
        store_b(cols, (o_m * zb[:, cols].astype(F32)).astype(BF16))


def _merge(x, a, b, ga_ref, gb_ref, wa_ref, wb_ref, wo_ref, g_ref):
    gate = lambda ref: _sigmoid(jnp.concatenate([ref[c] for c in range(ref.shape[0])], axis=1).astype(F32))
    pa = jnp.dot(a, wa_ref[...], preferred_element_type=F32)
    pb = jnp.dot(b, wb_ref[...], preferred_element_type=F32)
    merged = gate(ga_ref) * pa + gate(gb_ref) * pb
    out = jnp.dot(merged.astype(BF16), wo_ref[...], preferred_element_type=F32)
    ms = jnp.mean(out * out, axis=-1, keepdims=True)
    return x + out * lax.rsqrt(ms + EPS) * g_ref[...]


MIX_ROWS = 256
MIX_BLOCKS = MIX_ROWS // CHUNK


def _prompt_mix_merge_kernel(u_ref, v_ref, za_ref, q_ref, zb_ref, kvd_ref, kvdp_ref, x_ref, ga_ref,
                             gb_ref, lng_ref, lnb_ref, ws_ref, bs_ref, sinks_ref, wa_ref, wb_ref,
                             wo_ref, g_ref, y_ref, a_scr, b_scr):
    i = pl.program_id(0)
    last_tile = pl.num_programs(0) - 2
    tile = jnp.minimum(i, last_tile)
    slot = i % 2

    @pl.when(i == 0)
    def _():
        a_scr[1] = jnp.zeros(a_scr.shape[1:], BF16)
        b_scr[1] = jnp.zeros(b_scr.shape[1:], BF16)

    for blk in range(MIX_BLOCKS):
        rows = slice(blk * CHUNK, (blk + 1) * CHUNK)
        first_block = ((tile * MIX_BLOCKS + blk) % (SEQ // WINDOW)) == 0
        kvd_prev = kvdp_ref[...] if blk == 0 else kvd_ref[(blk - 1) * CHUNK:blk * CHUNK, :]

        def store_a(cols, val, rows=rows):
            a_scr[slot, rows, cols] = val

        def store_b(cols, val, rows=rows):
            b_scr[slot, rows, cols] = val

        _mix_block(u_ref[rows, :], v_ref[rows, :], za_ref[rows, :], q_ref[rows, :], zb_ref[rows, :],
                   kvd_ref[rows, :], kvd_prev, first_block, lng_ref[...], lnb_ref[...], ws_ref, bs_ref,
                   sinks_ref, store_a, store_b)

    y_ref[...] = _merge(x_ref[...], a_scr[1 - slot], b_scr[1 - slot], ga_ref, gb_ref,
                        wa_ref, wb_ref, wo_ref, g_ref)


def _prompt_mix_merge(x2d, z, kvd, ln_g, ln_b, w_s, b_s_rows, sinks, w_a, w_b, w_o, g_post):
    n = x2d.shape[0]
    tm = MIX_ROWS
    n_tiles = n // tm
    mix_tile = lambda i: jnp.minimum(i, n_tiles - 1)
    merge_tile = lambda i: jnp.maximum(i - 1, 0)
    wide = lambda cb: pl.BlockSpec((None, tm, Z_BLOCK), lambda i, cb=cb: (cb, mix_tile(i), 0))
    gate = lambda cb: pl.BlockSpec((D_MODEL // Z_BLOCK, tm, Z_BLOCK),
                                   lambda i, cb=cb: (cb // (D_MODEL // Z_BLOCK), merge_tile(i), 0))
    const2 = lambda shape: pl.BlockSpec(shape, lambda i: (0, 0))
    resident = lambda shape: pl.BlockSpec(shape, lambda i: (0, 0), pipeline_mode=pl.Buffered(1))
    return pl.pallas_call(
        _prompt_mix_merge_kernel,
        grid=(n_tiles + 1,),
        in_specs=[
            wide(COL_U), wide(COL_V), wide(COL_ZA), wide(COL_Q), wide(COL_ZB),
            pl.BlockSpec((tm, KVD_WIDTH), lambda i: (mix_tile(i), 0)),
            pl.BlockSpec((CHUNK, KVD_WIDTH), lambda i: (jnp.maximum(mix_tile(i) * MIX_BLOCKS - 1, 0), 0)),
            pl.BlockSpec((tm, D_MODEL), lambda i: (merge_tile(i), 0)),
            gate(COL_GA), gate(COL_GB),
            const2((1, A_WIDTH)), const2((1, A_WIDTH)),
            pl.BlockSpec((A_GROUPS, CHUNK, CHUNK), lambda i: (0, 0, 0)),
            const2((CHUNK, A_WIDTH)),
            pl.BlockSpec(memory_space=pltpu.SMEM),
            resident((A_WIDTH, D_MODEL)),
            resident((B_WIDTH, D_MODEL)),
            resident((D_MODEL, D_MODEL)),
            resident((1, D_MODEL)),
        ],
        out_specs=pl.BlockSpec((tm, D_MODEL), lambda i: (merge_tile(i), 0)),
        out_shape=jax.ShapeDtypeStruct((n, D_MODEL), F32),
        scratch_shapes=[pltpu.VMEM((2, tm, A_WIDTH), BF16), pltpu.VMEM((2, tm, B_WIDTH), BF16)],
        compiler_params=pltpu.CompilerParams(
            dimension_semantics=("arbitrary",), vmem_limit_bytes=VMEM_LIMIT),
        name="prompt_mix_merge",
    )(z, z, z, z, z, kvd, kvd, x2d, z, z, ln_g, ln_b, w_s, b_s_rows, sinks, w_a, w_b, w_o, g_post)


SEQS_PER_STEP = CHUNK // DEC_SEQ


def _sample_mixers_kernel(u_ref, v_ref, za_ref, q_ref, zb_ref, kvd_ref, kv32_ref, ckt_ref, cvt_ref,
                          lng_ref, lnb_ref, coef_ref, bs_ref, sinks_ref,
                          a_ref, b_ref, vn_ref, kwin_ref, vwin_ref):
    ns = SEQS_PER_STEP
    split = lambda x: x.reshape(ns, DEC_SEQ, x.shape[-1])

    vn = _layernorm_v(v_ref, lng_ref, lnb_ref)
    vn_ref[...] = vn
    vn3 = split(vn)
    t_row = lax.broadcasted_iota(jnp.int32, (DEC_SEQ, A_WIDTH), 0)
    s_acc = jnp.broadcast_to(bs_ref[...][None], (ns, DEC_SEQ, A_WIDTH))
    for s in range(DEC_SEQ):
        coef = jnp.where(t_row >= s, coef_ref[s], 0.0)
        s_acc = s_acc + vn3[:, s:s + 1, :] * coef[None]
    a3 = split(u_ref[...].astype(F32)) * s_acc * split(za_ref[...].astype(F32))
    a_ref[...] = a3.reshape(CHUNK, A_WIDTH).astype(BF16)

    rows_q = Q_PER_KV * DEC_SEQ
    lane3 = lax.broadcasted_iota(jnp.int32, (ns, DEC_SEQ, LANES), 2)
    low3 = lane3 < HEAD_DIM
    r_idx = lax.broadcasted_iota(jnp.int32, (ns, rows_q, LANES), 1)
    l_idx = lax.broadcasted_iota(jnp.int32, (ns, rows_q, LANES), 2)
    s_idx = lax.broadcasted_iota(jnp.int32, (ns, rows_q, LANES), 0)
    t_q = r_idx & (DEC_SEQ - 1)
    mask_cache = l_idx > t_q
    mask_new = ((l_idx >> 3) == s_idx) & ((l_idx & (DEC_SEQ - 1)) <= t_q)
    head_of_row = lax.broadcasted_iota(jnp.int32, (1, rows_q, 1), 1) >> 3
    q3 = split(q_ref[...].astype(F32))
    zb3 = split(zb_ref[...].astype(F32))
    for kh in range(N_KV_HEADS):
        pieces = []
        for m in (2 * kh, 2 * kh + 1):
            q_m = q3[:, :, m * LANES:(m + 1) * LANES]
            pieces += [jnp.where(low3, q_m, 0.0), jnp.where(low3, 0.0, q_m)]
        lhs = jnp.concatenate(pieces, axis=1).astype(BF16)
        rows = slice(kh * HEAD_DIM, (kh + 1) * HEAD_DIM)
        kt = ckt_ref[:, rows, :].astype(BF16)
        vt = cvt_ref[:, rows, :].astype(BF16)
        kt2 = jnp.concatenate([kt, kt], axis=1)
        vt2 = jnp.concatenate([vt, vt], axis=1)
        k_new = kvd_ref[:, kh * LANES:(kh + 1) * LANES]
        v_new = kvd_ref[:, KVD_WIDTH // 2 + kh * LANES:KVD_WIDTH // 2 + (kh + 1) * LANES]
        s_c = lax.dot_general(lhs, kt2, (((2,), (1,)), ((0,), (0,))), preferred_element_type=F32)
        s_n = lax.dot_general(lhs.reshape(ns * rows_q, LANES), k_new, (((1,), (1,)), ((), ())),
                              preferred_element_type=F32).reshape(ns, rows_q, LANES)
        s_c = jnp.where(mask_cache, s_c, -jnp.inf)
        s_n = jnp.where(mask_new, s_n, -jnp.inf)
        sink = jnp.zeros((1, rows_q, 1), F32)
        for jq in range(Q_PER_KV):
            sink = jnp.where(head_of_row == jq, sinks_ref[Q_PER_KV * kh + jq], sink)
        mx = jnp.maximum(jnp.maximum(jnp.max(s_c, axis=-1, keepdims=True),
                                     jnp.max(s_n, axis=-1, keepdims=True)), sink)
        p_c = jnp.exp(s_c - mx)
        p_n = jnp.exp(s_n - mx)
        denom = (jnp.sum(p_c, axis=-1, keepdims=True) + jnp.sum(p_n, axis=-1, keepdims=True)
                 + jnp.exp(sink - mx))
        o = lax.dot_general(p_c.astype(BF16), vt2, (((2,), (2,)), ((0,), (0,))),
                            preferred_element_type=F32)
        o = o + jnp.dot(p_n.reshape(ns * rows_q, LANES).astype(BF16), v_new,
                        preferred_element_type=F32).reshape(ns, rows_q, LANES)
        o = o * (1.0 / denom)
        for i, m in enumerate((2 * kh, 2 * kh + 1)):
            base = 2 * DEC_SEQ * i
            o_m = jnp.where(low3, o[:, base:base + DEC_SEQ, :], o[:, base + DEC_SEQ:base + 2 * DEC_SEQ, :])
            cols = slice(m * LANES, (m + 1) * LANES)
            b_ref[:, cols] = (o_m * zb3[:, :, cols]).reshape(CHUNK, LANES).astype(BF16)

    keep = WINDOW - DEC_SEQ
    lane_w = lax.broadcasted_iota(jnp.int32, (KV_WIDTH, WINDOW), 1)
    for new_rows, cache_ref, out_ref in ((kv32_ref[:, :KV_WIDTH], ckt_ref, kwin_ref),
                                         (kv32_ref[:, KV_WIDTH:], cvt_ref, vwin_ref)):
        new_t = new_rows.T
        old = pltpu.roll(cache_ref[...].reshape(ns * KV_WIDTH, WINDOW), keep, axis=1)
        for s in range(ns):
            shifted_new = pltpu.roll(new_t, (keep - DEC_SEQ * s) % WINDOW, axis=1)
            out_ref[s] = jnp.where(lane_w < keep, old[s * KV_WIDTH:(s + 1) * KV_WIDTH], shifted_new)


def _sample_mixers(z, kvd, kv32, cache_kt, cache_vt, ln_g, ln_b, coef, b_s8, sinks):
    n = kvd.shape[0]
    n_seq = cache_kt.shape[0]
    wide = lambda cb: pl.BlockSpec((None, CHUNK, Z_BLOCK), lambda r, cb=cb: (cb, r, 0))
    const2 = lambda shape: pl.BlockSpec(shape, lambda r: (0, 0))
    win = pl.BlockSpec((SEQS_PER_STEP, KV_WIDTH, WINDOW), lambda r: (r, 0, 0))
    return pl.pallas_call(
        _sample_mixers_kernel,
        grid=(n // CHUNK,),
        in_specs=[
            wide(COL_U), wide(COL_V), wide(COL_ZA), wide(COL_Q), wide(COL_ZB),
            pl.BlockSpec((CHUNK, KVD_WIDTH), lambda r: (r, 0)),
            pl.BlockSpec((CHUNK, 2 * KV_WIDTH), lambda r: (r, 0)),
            win, win,
            const2((1, A_WIDTH)), const2((1, A_WIDTH)),
            pl.BlockSpec((DEC_SEQ, DEC_SEQ, A_WIDTH), lambda r: (0, 0, 0)),
            const2((DEC_SEQ, A_WIDTH)),
            pl.BlockSpec(memory_space=pltpu.SMEM),
        ],
        out_specs=[
            pl.BlockSpec((CHUNK, A_WIDTH), lambda r: (r, 0)),
            pl.BlockSpec((CHUNK, B_WIDTH), lambda r: (r, 0)),
            pl.BlockSpec((CHUNK, A_WIDTH), lambda r: (r, 0)),
            win, win,
        ],
        out_shape=[
            jax.ShapeDtypeStruct((n, A_WIDTH), BF16),
            jax.ShapeDtypeStruct((n, B_WIDTH), BF16),
            jax.ShapeDtypeStruct((n, A_WIDTH), F32),
            jax.ShapeDtypeStruct((n_seq, KV_WIDTH, WINDOW), F32),
            jax.ShapeDtypeStruct((n_seq, KV_WIDTH, WINDOW), F32),
        ],
        compiler_params=pltpu.CompilerParams(
            dimension_semantics=("parallel",), vmem_limit_bytes=VMEM_LIMIT),
        name="sample_mixers",
    )(z, z, z, z, z, kvd, kv32, cache_kt, cache_vt, ln_g, ln_b, coef, b_s8, sinks)


def _merge_out_kernel(x_ref, a_ref, b_ref, ga_ref, gb_ref, wa_ref, wb_ref, wo_ref, g_ref, y_ref):
    y_ref[...] = _merge(x_ref[...], a_ref[...], b_ref[...], ga_ref, gb_ref,
                        wa_ref, wb_ref, wo_ref, g_ref)


def _merge_out(x2d, a, b, z, w_a, w_b, w_o, g_post):
    n = x2d.shape[0]
    tm = MERGE_ROWS
    resident = lambda shape: pl.BlockSpec(shape, lambda i: (0, 0), pipeline_mode=pl.Buffered(1))
    return pl.pallas_call(
        _merge_out_kernel,
        grid=(n // tm,),
        in_specs=[
            pl.BlockSpec((tm, D_MODEL), lambda i: (i, 0)),
            pl.BlockSpec((tm, A_WIDTH), lambda i: (i, 0)),
            pl.BlockSpec((tm, B_WIDTH), lambda i: (i, 0)),
            pl.BlockSpec((D_MODEL // Z_BLOCK, tm, Z_BLOCK), lambda i: (COL_GA // (D_MODEL // Z_BLOCK), i, 0)),
            pl.BlockSpec((D_MODEL // Z_BLOCK, tm, Z_BLOCK), lambda i: (COL_GB // (D_MODEL // Z_BLOCK), i, 0)),
            resident((A_WIDTH, D_MODEL)),
            resident((B_WIDTH, D_MODEL)),
            resident((D_MODEL, D_MODEL)),
            resident((1, D_MODEL)),
        ],
        out_specs=pl.BlockSpec((tm, D_MODEL), lambda i: (i, 0)),
        out_shape=jax.ShapeDtypeStruct((n, D_MODEL), F32),
        compiler_params=pltpu.CompilerParams(
            dimension_semantics=("parallel",), vmem_limit_bytes=VMEM_LIMIT),
        name="merge_out",
    )(x2d, a, b, z, z, w_a, w_b, w_o, g_post)


def _cast_tile_kernel(w_ref, o_ref):
    o_ref[...] = w_ref[...].astype(BF16)


def _cast_column_tiles(w):
    rows, cols = w.shape
    return pl.pallas_call(
        _cast_tile_kernel,
        grid=(cols // COL_TILE,),
        in_specs=[pl.BlockSpec((rows, COL_TILE), lambda t: (0, t))],
        out_specs=pl.BlockSpec((None, rows, COL_TILE), lambda t: (t, 0, 0)),
        out_shape=jax.ShapeDtypeStruct((cols // COL_TILE, rows, COL_TILE), BF16),
        compiler_params=pltpu.CompilerParams(
            dimension_semantics=("parallel",), vmem_limit_bytes=VMEM_LIMIT),
        name="cast_column_tiles",
    )(w)


def _rope_tables(pos):
    lane = jnp.arange(LANES)
    inv = ROPE_THETA ** (-(2 * (lane % (HEAD_DIM // 2))).astype(F32) / HEAD_DIM)
    ang = pos.astype(F32)[:, None] * inv[None, :]
    sign = jnp.where((lane % HEAD_DIM) < HEAD_DIM // 2, -1.0, 1.0).astype(F32)
    return jnp.cos(ang), jnp.sin(ang) * sign[None, :]


def _window_first(win):
    n = win.shape[1]
    return jnp.transpose(win[0], (0, 2, 3, 1)).reshape(n, KV_WIDTH, WINDOW)


def _window_last(win_t):
    n = win_t.shape[0]
    return jnp.transpose(win_t.reshape(n, N_KV_HEADS, HEAD_DIM, WINDOW), (0, 3, 1, 2))[None]


def kernel(x_prompt, x_sample, cache_k_win, cache_v_win, g_pre, w_in, ln_v_g, ln_v_b, w_spatial,
           b_spatial, sinks, w_proj_a, w_proj_b, w_out, g_post):
    bsz, seq, _ = x_prompt.shape
    dbsz, dseq, _ = x_sample.shape
    assert seq == SEQ and dseq == DEC_SEQ and seq % IN_PROJ_ROWS == 0
    assert (dbsz * dseq) % IN_PROJ_ROWS == 0 and g_pre.shape[0] == 1

    w_in_b = _cast_column_tiles(w_in[0])
    w_a = w_proj_a[0].astype(BF16)
    w_b = w_proj_b[0].astype(BF16)
    w_o = w_out[0].astype(BF16)
    b_s_rows = jnp.repeat(b_spatial[0].T, LANES, axis=1)
    coef = jnp.repeat(jnp.transpose(w_spatial[0][:, :DEC_SEQ, :DEC_SEQ], (2, 1, 0)), LANES, axis=2)
    cos_p, sin_p = _rope_tables(jnp.arange(SEQ))
    cos_s, sin_s = _rope_tables(PAST_LEN + jnp.arange(dseq))

    xp = x_prompt.reshape(bsz * seq, D_MODEL)
    z_p, kvd_p, kv32_p = _in_proj(xp, g_pre, w_in_b, cos_p, sin_p, IN_PROJ_ROWS, 2, 256)
    y_p = _prompt_mix_merge(xp, z_p, kvd_p, ln_v_g, ln_v_b, w_spatial[0], b_s_rows, sinks[0],
                            w_a, w_b, w_o, g_post)
    kv_win_p = kv32_p.reshape(bsz, seq, 2 * KV_WIDTH)[:, seq - WINDOW:]
    k_win_p = kv_win_p[..., :KV_WIDTH].reshape(1, bsz, WINDOW, N_KV_HEADS, HEAD_DIM)
    v_win_p = kv_win_p[..., KV_WIDTH:].reshape(1, bsz, WINDOW, N_KV_HEADS, HEAD_DIM)

    xs = x_sample.reshape(dbsz * dseq, D_MODEL)
    z_s, kvd_s, kv32_s = _in_proj(xs, g_pre, w_in_b, cos_s, sin_s, dseq, 2, 1024)
    a_s, b_s, vn_s, k_win_t, v_win_t = _sample_mixers(
        z_s, kvd_s, kv32_s, _window_first(cache_k_win), _window_first(cache_v_win),
        ln_v_g, ln_v_b, coef, b_s_rows[:dseq], sinks[0])
    y_s = _merge_out(xs, a_s, b_s, z_s, w_a, w_b, w_o, g_post)

    return (y_p.reshape(bsz, seq, D_MODEL),
            y_s.reshape(dbsz, dseq, D_MODEL),
            k_win_p, v_win_p,
            _window_last(k_win_t), _window_last(v_win_t),
            vn_s.reshape(1, dbsz, dseq, A_WIDTH))
```

```python
import functools

import numpy as np
import jax
import jax.numpy as jnp
from jax import lax
from jax.experimental import pallas as pl
from jax.experimental.pallas import tpu as pltpu

D_MODEL = 2048
SEQ = 2048
DEC_SEQ = 8
PAST_LEN = 8192
CHUNK = 128
A_WIDTH = 1024
A_GROUPS = 8
HEAD_DIM = 64
N_HEADS = 16
N_KV_HEADS = 4
Q_PER_KV = N_HEADS // N_KV_HEADS
B_WIDTH = N_HEADS * HEAD_DIM
KV_WIDTH = N_KV_HEADS * HEAD_DIM
WINDOW = 128
ROPE_THETA = 10000.0
EPS = 1e-6
IN_COLS = 3 * A_WIDTH + 2 * B_WIDTH + 2 * KV_WIDTH + 2 * D_MODEL

LANES = 128
SUBLANES = 8
BF16 = jnp.bfloat16
F32 = jnp.float32

COL_TILE = 512
N_COL_TILES = IN_COLS // COL_TILE
Z_BLOCK = 2 * COL_TILE
Z_COLS = (N_COL_TILES - 1) * COL_TILE
COL_GA, COL_GB, COL_U, COL_V, COL_ZA, COL_Q, COL_ZB = 0, 2, 4, 5, 6, 7, 8
KVD_WIDTH = 2 * N_KV_HEADS * LANES
SEC_GATE_NORM, SEC_GATE, SEC_GELU, SEC_SILU, SEC_Q, SEC_KV = range(6)


def _walk(tiles_per_step):
    sections = [((11, 12, 13, 14, 15, 16, 17, 18), SEC_GATE), ((0, 1, 2, 3), SEC_GELU),
                ((4, 5), SEC_SILU), ((6, 7), SEC_Q), ((9, 10), SEC_SILU)]
    steps = [(tiles[k:k + tiles_per_step], sec)
             for tiles, sec in sections for k in range(0, len(tiles), tiles_per_step)]
    steps = [(tiles, sec, s) for s, (tiles, sec) in enumerate(steps)]
    steps[0] = (steps[0][0], SEC_GATE_NORM, 0)
    mid = len(steps) // 2
    steps.insert(mid, ((8,) * tiles_per_step, SEC_KV, steps[mid - 1][2]))
    return steps


def _lookup(values, j):
    out = jnp.int32(values[-1])
    for k in range(len(values) - 2, -1, -1):
        out = jnp.where(j == k, jnp.int32(values[k]), out)
    return out


IN_PROJ_ROWS = 1024
MERGE_ROWS = 512
VMEM_LIMIT = 56 * 1024 * 1024


def _gelu(x):
    return 0.5 * x * (1.0 + lax.erf(x * np.float32(1.0 / np.sqrt(2.0))))


def _sigmoid(x):
    return 1.0 / (1.0 + jnp.exp(-x))


def _rope(x, cos, sin_signed):
    width = x.shape[1]
    lane = lax.broadcasted_iota(jnp.int32, x.shape, 1)
    first_half = (lane & (HEAD_DIM - 1)) < (HEAD_DIM // 2)
    partner = jnp.where(first_half,
                        pltpu.roll(x, width - HEAD_DIM // 2, axis=1),
                        pltpu.roll(x, HEAD_DIM // 2, axis=1))
    reps = width // LANES
    cos_w = jnp.concatenate([cos] * reps, axis=1) if reps > 1 else cos
    sin_w = jnp.concatenate([sin_signed] * reps, axis=1) if reps > 1 else sin_signed
    return x * cos_w + partner * sin_w


def _dup_heads(x):
    lane = lax.broadcasted_iota(jnp.int32, (x.shape[0], LANES), 1)
    low = lane < HEAD_DIM
    out = []
    for c in range(x.shape[1] // LANES):
        xc = x[:, c * LANES:(c + 1) * LANES]
        sw = pltpu.roll(xc, HEAD_DIM, axis=1)
        out += [jnp.where(low, xc, sw), jnp.where(low, sw, xc)]
    return jnp.concatenate(out, axis=1)


def _in_proj_kernel(x_ref, g_ref, *refs, table_rows, sub, walk):
    n_w = len(walk[0][0])
    w_refs = refs[:n_w]
    cos_ref, sin_ref, z_ref, kvd_ref, kv32_ref, h_scr = refs[n_w:]
    j = pl.program_id(1)
    tm = x_ref.shape[0]

    def tables(rs):
        if table_rows == tm:
            return cos_ref[rs, :], sin_ref[rs, :]
        tile = lambda t: jnp.broadcast_to(t[None], (sub // table_rows, table_rows, LANES)).reshape(sub, LANES)
        return tile(cos_ref[...]), tile(sin_ref[...])

    def run(epilogue, norm=False, n_tiles=n_w):
        for r in range(tm // sub):
            rs = slice(r * sub, (r + 1) * sub)
            if norm:
                x = x_ref[rs, :]
                ms = jnp.mean(x * x, axis=-1, keepdims=True)
                h = (x * lax.rsqrt(ms + EPS) * g_ref[...]).astype(BF16)
                h_scr[rs, :] = h
            else:
                h = h_scr[rs, :]
            for t in range(n_tiles):
                acc = jnp.dot(h, w_refs[t][...], preferred_element_type=F32)
                epilogue(acc, rs, slice(t * COL_TILE, (t + 1) * COL_TILE))

    def gate_epi(acc, rs, cs):
        z_ref[rs, cs] = _sigmoid(acc).astype(BF16)

    def gelu_epi(acc, rs, cs):
        z_ref[rs, cs] = _gelu(acc).astype(BF16)

    def silu_epi(acc, rs, cs):
        z_ref[rs, cs] = (acc * _sigmoid(acc)).astype(BF16)

    def q_epi(acc, rs, cs):
        cos, sin = tables(rs)
        z_ref[rs, cs] = (_rope(acc, cos, sin) * np.float32(HEAD_DIM ** -0.5)).astype(BF16)

    def kv_epi(acc, rs, cs):
        cos, sin = tables(rs)
        k = _rope(acc[:, :KV_WIDTH], cos, sin)
        v = acc[:, KV_WIDTH:]
        kv32_ref[rs, :KV_WIDTH] = k
        kv32_ref[rs, KV_WIDTH:] = v
        kvd_ref[rs, :KVD_WIDTH // 2] = _dup_heads(k).astype(BF16)
        kvd_ref[rs, KVD_WIDTH // 2:] = _dup_heads(v).astype(BF16)

    sec = _lookup(tuple(sec for _, sec, _ in walk), j)
    pl.when(sec == SEC_GATE_NORM)(lambda: run(gate_epi, norm=True))
    pl.when(sec == SEC_GATE)(lambda: run(gate_epi))
    pl.when(sec == SEC_GELU)(lambda: run(gelu_epi))
    pl.when(sec == SEC_SILU)(lambda: run(silu_epi))
    pl.when(sec == SEC_Q)(lambda: run(q_epi))
    pl.when(sec == SEC_KV)(lambda: run(kv_epi, n_tiles=1))


def _in_proj(x2d, g_pre, w_in_b, cos_t, sin_t, table_rows, tiles_per_step, sub):
    n = x2d.shape[0]
    tm = IN_PROJ_ROWS
    walk = _walk(tiles_per_step)
    n_steps = len(walk)
    if table_rows == tm:
        n_tab = cos_t.shape[0] // tm
        table_spec = pl.BlockSpec((tm, LANES), lambda i, j: (i % n_tab, 0))
    else:
        table_spec = pl.BlockSpec((table_rows, LANES), lambda i, j: (0, 0))
    w_specs = [pl.BlockSpec((None, D_MODEL, COL_TILE),
                            lambda i, j, t=t: (_lookup(tuple(tiles[t] for tiles, _, _ in walk), j), 0, 0))
               for t in range(tiles_per_step)]
    steps_per_block = Z_BLOCK // (tiles_per_step * COL_TILE)

    def z_index(i, j):
        s = _lookup(tuple(z_step for _, _, z_step in walk), j)
        return (s // steps_per_block, i, s % steps_per_block)

    return pl.pallas_call(
        functools.partial(_in_proj_kernel, table_rows=table_rows, sub=sub, walk=walk),
        grid=(n // tm, n_steps),
        in_specs=[
            pl.BlockSpec((tm, D_MODEL), lambda i, j: (i, 0)),
            pl.BlockSpec((1, D_MODEL), lambda i, j: (0, 0)),
            *w_specs,
            table_spec, table_spec,
        ],
        out_specs=[
            pl.BlockSpec((None, tm, tiles_per_step * COL_TILE), z_index),
            pl.BlockSpec((tm, KVD_WIDTH), lambda i, j: (i, 0)),
            pl.BlockSpec((tm, 2 * KV_WIDTH), lambda i, j: (i, 0)),
        ],
        out_shape=[
            jax.ShapeDtypeStruct((Z_COLS // Z_BLOCK, n, Z_BLOCK), BF16),
            jax.ShapeDtypeStruct((n, KVD_WIDTH), BF16),
            jax.ShapeDtypeStruct((n, 2 * KV_WIDTH), F32),
        ],
        scratch_shapes=[pltpu.VMEM((tm, D_MODEL), BF16)],
        compiler_params=pltpu.CompilerParams(
            dimension_semantics=("parallel", "arbitrary"), vmem_limit_bytes=VMEM_LIMIT),
        name="in_proj",
    )(x2d, g_pre, *([w_in_b] * tiles_per_step), cos_t, sin_t)


def _layernorm_v(v_ref, lng_ref, lnb_ref):
    vg = v_ref[...].astype(F32)
    mu = jnp.mean(vg, axis=-1, keepdims=True)
    xc = vg - mu
    var = jnp.mean(xc * xc, axis=-1, keepdims=True)
    return xc * lax.rsqrt(var + EPS) * lng_ref[...] + lnb_ref[...]


def _mix_block(u, v, za, q, zb, kvd, kvd_prev, first_block, lng, lnb, ws_ref, bs_ref, sinks_ref,
               store_a, store_b):
    row = lax.broadcasted_iota(jnp.int32, (CHUNK, CHUNK), 0)
    col = lax.broadcasted_iota(jnp.int32, (CHUNK, CHUNK), 1)

    vg = v.astype(F32)
    mu = jnp.mean(vg, axis=-1, keepdims=True)
    xc = vg - mu
    var = jnp.mean(xc * xc, axis=-1, keepdims=True)
    vn_b = (xc * lax.rsqrt(var + EPS) * lng + lnb).astype(BF16)
    for g in range(A_GROUPS):
        cols = slice(g * LANES, (g + 1) * LANES)
        w = jnp.where(col <= row, ws_ref[g], 0.0).astype(BF16)
        s = jnp.dot(w, vn_b[:, cols], preferred_element_type=F32) + bs_ref[:, cols]
        store_a(cols, ((u[:, cols].astype(F32) * s) * za[:, cols].astype(F32)).astype(BF16))

    t = lax.broadcasted_iota(jnp.int32, (WINDOW, 2 * WINDOW), 0)
    jj = lax.broadcasted_iota(jnp.int32, (WINDOW, 2 * WINDOW), 1)
    mask = (jj > t) & (jj <= t + WINDOW) & ((jj >= WINDOW) | jnp.logical_not(first_block))
    lane = lax.broadcasted_iota(jnp.int32, (WINDOW, LANES), 1)
    low_half = lane < HEAD_DIM
    for m in range(N_HEADS // 2):
        cols = slice(m * LANES, (m + 1) * LANES)
        kcols = slice((m // 2) * LANES, (m // 2 + 1) * LANES)
        vcols = slice(KVD_WIDTH // 2 + (m // 2) * LANES, KVD_WIDTH // 2 + (m // 2 + 1) * LANES)
        k_c = jnp.concatenate([kvd_prev[:, kcols], kvd[:, kcols]], axis=0)
        v_c = jnp.concatenate([kvd_prev[:, vcols], kvd[:, vcols]], axis=0)
        q_m = q[:, cols]
        outs = []
        for half in range(2):
            sink = sinks_ref[2 * m + half]
            q_h = jnp.where(low_half if half == 0 else ~low_half, q_m, jnp.zeros_like(q_m))
            s = lax.dot_general(q_h, k_c, (((1,), (1,)), ((), ())), preferred_element_type=F32)
            s = jnp.where(mask, s, -jnp.inf)
            mx = jnp.maximum(jnp.max(s, axis=-1, keepdims=True), sink)
            p = jnp.exp(s - mx)
            denom = jnp.sum(p, axis=-1, keepdims=True) + jnp.exp(sink - mx)
            o = jnp.dot(p.astype(BF16), v_c, preferred_element_type=F32)
            outs.append(o * (1.0 / denom))
        o_m = jnp.where(low_half, outs[0], outs[1])
        store_b(cols, (o_m * zb[:, cols].astype(F32)).astype(BF16))


def _merge(x, a, b, ga_ref, gb_ref, wa_ref, wb_ref, wo_ref, g_ref):
    gate = lambda ref: jnp.concatenate([ref[c] for c in range(ref.shape[0])], axis=1).astype(F32)
    pa = jnp.dot(a, wa_ref[...], preferred_element_type=F32)
    pb = jnp.dot(b, wb_ref[...], preferred_element_type=F32)
    merged = gate(ga_ref) * pa + gate(gb_ref) * pb
    out = jnp.dot(merged.astype(BF16), wo_ref[...], preferred_element_type=F32)
    ms = jnp.mean(out * out, axis=-1, keepdims=True)
    return x + out * lax.rsqrt(ms + EPS) * g_ref[...]


MIX_ROWS = 256
MIX_BLOCKS = MIX_ROWS // CHUNK


def _prompt_mix_merge_kernel(u_ref, v_ref, za_ref, q_ref, zb_ref, kvd_ref, kvdp_ref, x_ref, ga_ref,
                             gb_ref, lng_ref, lnb_ref, ws_ref, bs_ref, sinks_ref, wa_ref, wb_ref,
                             wo_ref, g_ref, y_ref, a_scr, b_scr):
    i = pl.program_id(0)
    last_tile = pl.num_programs(0) - 2
    tile = jnp.minimum(i, last_tile)
    slot = i % 2

    @pl.when(i == 0)
    def _():
        a_scr[1] = jnp.zeros(a_scr.shape[1:], BF16)
        b_scr[1] = jnp.zeros(b_scr.shape[1:], BF16)

    for blk in range(MIX_BLOCKS):
        rows = slice(blk * CHUNK, (blk + 1) * CHUNK)
        first_block = ((tile * MIX_BLOCKS + blk) % (SEQ // WINDOW)) == 0
        kvd_prev = kvdp_ref[...] if blk == 0 else kvd_ref[(blk - 1) * CHUNK:blk * CHUNK, :]

        def store_a(cols, val, rows=rows):
            a_scr[slot, rows, cols] = val

        def store_b(cols, val, rows=rows):
            b_scr[slot, rows, cols] = val

        _mix_block(u_ref[rows, :], v_ref[rows, :], za_ref[rows, :], q_ref[rows, :], zb_ref[rows, :],
                   kvd_ref[rows, :], kvd_prev, first_block, lng_ref[...], lnb_ref[...], ws_ref, bs_ref,
                   sinks_ref, store_a, store_b)

    y_ref[...] = _merge(x_ref[...], a_scr[1 - slot], b_scr[1 - slot], ga_ref, gb_ref,
                        wa_ref, wb_ref, wo_ref, g_ref)


def _prompt_mix_merge(x2d, z, kvd, ln_g, ln_b, w_s, b_s_rows, sinks, w_a, w_b, w_o, g_post):
    n = x2d.shape[0]
    tm = MIX_ROWS
    n_tiles = n // tm
    mix_tile = lambda i: jnp.minimum(i, n_tiles - 1)
    merge_tile = lambda i: jnp.maximum(i - 1, 0)
    wide = lambda cb: pl.BlockSpec((None, tm, Z_BLOCK), lambda i, cb=cb: (cb, mix_tile(i), 0))
    gate = lambda cb: pl.BlockSpec((D_MODEL // Z_BLOCK, tm, Z_BLOCK),
                                   lambda i, cb=cb: (cb // (D_MODEL // Z_BLOCK), merge_tile(i), 0))
    const2 = lambda shape: pl.BlockSpec(shape, lambda i: (0, 0))
    resident = lambda shape: pl.BlockSpec(shape, lambda i: (0, 0), pipeline_mode=pl.Buffered(1))
    return pl.pallas_call(
        _prompt_mix_merge_kernel,
        grid=(n_tiles + 1,),
        in_specs=[
            wide(COL_U), wide(COL_V), wide(COL_ZA), wide(COL_Q), wide(COL_ZB),
            pl.BlockSpec((tm, KVD_WIDTH), lambda i: (mix_tile(i), 0)),
            pl.BlockSpec((CHUNK, KVD_WIDTH), lambda i: (jnp.maximum(mix_tile(i) * MIX_BLOCKS - 1, 0), 0)),
            pl.BlockSpec((tm, D_MODEL), lambda i: (merge_tile(i), 0)),
            gate(COL_GA), gate(COL_GB),
            const2((1, A_WIDTH)), const2((1, A_WIDTH)),
            pl.BlockSpec((A_GROUPS, CHUNK, CHUNK), lambda i: (0, 0, 0)),
            const2((CHUNK, A_WIDTH)),
            pl.BlockSpec(memory_space=pltpu.SMEM),
            resident((A_WIDTH, D_MODEL)),
            resident((B_WIDTH, D_MODEL)),
            resident((D_MODEL, D_MODEL)),
            resident((1, D_MODEL)),
        ],
        out_specs=pl.BlockSpec((tm, D_MODEL), lambda i: (merge_tile(i), 0)),
        out_shape=jax.ShapeDtypeStruct((n, D_MODEL), F32),
        scratch_shapes=[pltpu.VMEM((2, tm, A_WIDTH), BF16), pltpu.VMEM((2, tm, B_WIDTH), BF16)],
        compiler_params=pltpu.CompilerParams(
            dimension_semantics=("arbitrary",), vmem_limit_bytes=VMEM_LIMIT),
        name="prompt_mix_merge",
    )(z, z, z, z, z, kvd, kvd, x2d, z, z, ln_g, ln_b, w_s, b_s_rows, sinks, w_a, w_b, w_o, g_post)


SEQS_PER_STEP = CHUNK // DEC_SEQ


def _sample_mixers_kernel(u_ref, v_ref, za_ref, q_ref, zb_ref, kvd_ref, kv32_ref, ckt_ref, cvt_ref,
                          lng_ref, lnb_ref, coef_ref, bs_ref, sinks_ref,
                          a_ref, b_ref, vn_ref, kwin_ref, vwin_ref):
    ns = SEQS_PER_STEP
    split = lambda x: x.reshape(ns, DEC_SEQ, x.shape[-1])

    vn = _layernorm_v(v_ref, lng_ref, lnb_ref)
    vn_ref[...] = vn
    vn3 = split(vn)
    t_row = lax.broadcasted_iota(jnp.int32, (DEC_SEQ, A_WIDTH), 0)
    s_acc = jnp.broadcast_to(bs_ref[...][None], (ns, DEC_SEQ, A_WIDTH))
    for s in range(DEC_SEQ):
        coef = jnp.where(t_row >= s, coef_ref[s], 0.0)
        s_acc = s_acc + vn3[:, s:s + 1, :] * coef[None]
    a3 = split(u_ref[...].astype(F32)) * s_acc * split(za_ref[...].astype(F32))
    a_ref[...] = a3.reshape(CHUNK, A_WIDTH).astype(BF16)

    rows_q = Q_PER_KV * DEC_SEQ
    lane3 = lax.broadcasted_iota(jnp.int32, (ns, DEC_SEQ, LANES), 2)
    low3 = lane3 < HEAD_DIM
    r_idx = lax.broadcasted_iota(jnp.int32, (ns, rows_q, LANES), 1)
    l_idx = lax.broadcasted_iota(jnp.int32, (ns, rows_q, LANES), 2)
    s_idx = lax.broadcasted_iota(jnp.int32, (ns, rows_q, LANES), 0)
    t_q = r_idx & (DEC_SEQ - 1)
    mask_cache = l_idx > t_q
    mask_new = ((l_idx >> 3) == s_idx) & ((l_idx & (DEC_SEQ - 1)) <= t_q)
    head_of_row = lax.broadcasted_iota(jnp.int32, (1, rows_q, 1), 1) >> 3
    q3 = split(q_ref[...].astype(F32))
    zb3 = split(zb_ref[...].astype(F32))
    for kh in range(N_KV_HEADS):
        pieces = []
        for m in (2 * kh, 2 * kh + 1):
            q_m = q3[:, :, m * LANES:(m + 1) * LANES]
            pieces += [jnp.where(low3, q_m, 0.0), jnp.where(low3, 0.0, q_m)]
        lhs = jnp.concatenate(pieces, axis=1).astype(BF16)
        rows = slice(kh * HEAD_DIM, (kh + 1) * HEAD_DIM)
        kt = ckt_ref[:, rows, :].astype(BF16)
        vt = cvt_ref[:, rows, :].astype(BF16)
        kt2 = jnp.concatenate([kt, kt], axis=1)
        vt2 = jnp.concatenate([vt, vt], axis=1)
        k_new = kvd_ref[:, kh * LANES:(kh + 1) * LANES]
        v_new = kvd_ref[:, KVD_WIDTH // 2 + kh * LANES:KVD_WIDTH // 2 + (kh + 1) * LANES]
        s_c = lax.dot_general(lhs, kt2, (((2,), (1,)), ((0,), (0,))), preferred_element_type=F32)
        s_n = lax.dot_general(lhs.reshape(ns * rows_q, LANES), k_new, (((1,), (1,)), ((), ())),
                              preferred_element_type=F32).reshape(ns, rows_q, LANES)
        s_c = jnp.where(mask_cache, s_c, -jnp.inf)
        s_n = jnp.where(mask_new, s_n, -jnp.inf)
        sink = jnp.zeros((1, rows_q, 1), F32)
        for jq in range(Q_PER_KV):
            sink = jnp.where(head_of_row == jq, sinks_ref[Q_PER_KV * kh + jq], sink)
        mx = jnp.maximum(jnp.maximum(jnp.max(s_c, axis=-1, keepdims=True),
                                     jnp.max(s_n, axis=-1, keepdims=True)), sink)
        p_c = jnp.exp(s_c - mx)
        p_n = jnp.exp(s_n - mx)
        denom = (jnp.sum(p_c, axis=-1, keepdims=True) + jnp.sum(p_n, axis=-1, keepdims=True)
                 + jnp.exp(sink - mx))
        o = lax.dot_general(p_c.astype(BF16), vt2, (((2,), (2,)), ((0,), (0,))),
                            preferred_element_type=F32)
        o = o + jnp.dot(p_n.reshape(ns * rows_q, LANES).astype(BF16), v_new,
                        preferred_element_type=F32).reshape(ns, rows_q, LANES)
        o = o * (1.0 / denom)
        for i, m in enumerate((2 * kh, 2 * kh + 1)):
            base = 2 * DEC_SEQ * i
            o_m = jnp.where(low3, o[:, base:base + DEC_SEQ, :], o[:, base + DEC_SEQ:base + 2 * DEC_SEQ, :])
            cols = slice(m * LANES, (m + 1) * LANES)
            b_ref[:, cols] = (o_m * zb3[:, :, cols]).reshape(CHUNK, LANES).astype(BF16)

    keep = WINDOW - DEC_SEQ
    lane_w = lax.broadcasted_iota(jnp.int32, (KV_WIDTH, WINDOW), 1)
    for new_rows, cache_ref, out_ref in ((kv32_ref[:, :KV_WIDTH], ckt_ref, kwin_ref),
                                         (kv32_ref[:, KV_WIDTH:], cvt_ref, vwin_ref)):
        new_t = new_rows.T
        old = pltpu.roll(cache_ref[...].reshape(ns * KV_WIDTH, WINDOW), keep, axis=1)
        for s in range(ns):
            shifted_new = pltpu.roll(new_t, (keep - DEC_SEQ * s) % WINDOW, axis=1)
            out_ref[s] = jnp.where(lane_w < keep, old[s * KV_WIDTH:(s + 1) * KV_WIDTH], shifted_new)


def _sample_mixers(z, kvd, kv32, cache_kt, cache_vt, ln_g, ln_b, coef, b_s8, sinks):
    n = kvd.shape[0]
    n_seq = cache_kt.shape[0]
    wide = lambda cb: pl.BlockSpec((None, CHUNK, Z_BLOCK), lambda r, cb=cb: (cb, r, 0))
    const2 = lambda shape: pl.BlockSpec(shape, lambda r: (0, 0))
    win = pl.BlockSpec((SEQS_PER_STEP, KV_WIDTH, WINDOW), lambda r: (r, 0, 0))
    return pl.pallas_call(
        _sample_mixers_kernel,
        grid=(n // CHUNK,),
        in_specs=[
            wide(COL_U), wide(COL_V), wide(COL_ZA), wide(COL_Q), wide(COL_ZB),
            pl.BlockSpec((CHUNK, KVD_WIDTH), lambda r: (r, 0)),
            pl.BlockSpec((CHUNK, 2 * KV_WIDTH), lambda r: (r, 0)),
            win, win,
            const2((1, A_WIDTH)), const2((1, A_WIDTH)),
            pl.BlockSpec((DEC_SEQ, DEC_SEQ, A_WIDTH), lambda r: (0, 0, 0)),
            const2((DEC_SEQ, A_WIDTH)),
            pl.BlockSpec(memory_space=pltpu.SMEM),
        ],
        out_specs=[
            pl.BlockSpec((CHUNK, A_WIDTH), lambda r: (r, 0)),
            pl.BlockSpec((CHUNK, B_WIDTH), lambda r: (r, 0)),
            pl.BlockSpec((CHUNK, A_WIDTH), lambda r: (r, 0)),
            win, win,
        ],
        out_shape=[
            jax.ShapeDtypeStruct((n, A_WIDTH), BF16),
            jax.ShapeDtypeStruct((n, B_WIDTH), BF16),
            jax.ShapeDtypeStruct((n, A_WIDTH), F32),
            jax.ShapeDtypeStruct((n_seq, KV_WIDTH, WINDOW), F32),
            jax.ShapeDtypeStruct((n_seq, KV_WIDTH, WINDOW), F32),
        ],
        compiler_params=pltpu.CompilerParams(
            dimension_semantics=("parallel",), vmem_limit_bytes=VMEM_LIMIT),
        name="sample_mixers",
    )(z, z, z, z, z, kvd, kv32, cache_kt, cache_vt, ln_g, ln_b, coef, b_s8, sinks)


def _merge_out_kernel(x_ref, a_ref, b_ref, ga_ref, gb_ref, wa_ref, wb_ref, wo_ref, g_ref, y_ref):
    y_ref[...] = _merge(x_ref[...], a_ref[...], b_ref[...], ga_ref, gb_ref,
                        wa_ref, wb_ref, wo_ref, g_ref)


def _merge_out(x2d, a, b, z, w_a, w_b, w_o, g_post):
    n = x2d.shape[0]
    tm = MERGE_ROWS
    resident = lambda shape: pl.BlockSpec(shape, lambda i: (0, 0), pipeline_mode=pl.Buffered(1))
    return pl.pallas_call(
        _merge_out_kernel,
        grid=(n // tm,),
        in_specs=[
            pl.BlockSpec((tm, D_MODEL), lambda i: (i, 0)),
            pl.BlockSpec((tm, A_WIDTH), lambda i: (i, 0)),
            pl.BlockSpec((tm, B_WIDTH), lambda i: (i, 0)),
            pl.BlockSpec((D_MODEL // Z_BLOCK, tm, Z_BLOCK), lambda i: (COL_GA // (D_MODEL // Z_BLOCK), i, 0)),
            pl.BlockSpec((D_MODEL // Z_BLOCK, tm, Z_BLOCK), lambda i: (COL_GB // (D_MODEL // Z_BLOCK), i, 0)),
            resident((A_WIDTH, D_MODEL)),
            resident((B_WIDTH, D_MODEL)),
            resident((D_MODEL, D_MODEL)),
            resident((1, D_MODEL)),
        ],
        out_specs=pl.BlockSpec((tm, D_MODEL), lambda i: (i, 0)),
        out_shape=jax.ShapeDtypeStruct((n, D_MODEL), F32),
        compiler_params=pltpu.CompilerParams(
            dimension_semantics=("parallel",), vmem_limit_bytes=VMEM_LIMIT),
        name="merge_out",
    )(x2d, a, b, z, z, w_a, w_b, w_o, g_post)


def _cast_tile_kernel(w_ref, o_ref):
    o_ref[...] = w_ref[...].astype(BF16)


def _cast_column_tiles(w):
    rows, cols = w.shape
    return pl.pallas_call(
        _cast_tile_kernel,
        grid=(cols // COL_TILE,),
        in_specs=[pl.BlockSpec((rows, COL_TILE), lambda t: (0, t))],
        out_specs=pl.BlockSpec((None, rows, COL_TILE), lambda t: (t, 0, 0)),
        out_shape=jax.ShapeDtypeStruct((cols // COL_TILE, rows, COL_TILE), BF16),
        compiler_params=pltpu.CompilerParams(
            dimension_semantics=("parallel",), vmem_limit_bytes=VMEM_LIMIT),
        name="cast_column_tiles",
    )(w)


def _rope_tables(pos):
    lane = jnp.arange(LANES)
    inv = ROPE_THETA ** (-(2 * (lane % (HEAD_DIM // 2))).astype(F32) / HEAD_DIM)
    ang = pos.astype(F32)[:, None] * inv[None, :]
    sign = jnp.where((lane % HEAD_DIM) < HEAD_DIM // 2, -1.0, 1.0).astype(F32)
    return jnp.cos(ang), jnp.sin(ang) * sign[None, :]


def _window_first(win):
    n = win.shape[1]
    return jnp.transpose(win[0], (0, 2, 3, 1)).reshape(n, KV_WIDTH, WINDOW)


def _window_last(win_t):
    n = win_t.shape[0]
    return jnp.transpose(win_t.reshape(n, N_KV_HEADS, HEAD_DIM, WINDOW), (0, 3, 1, 2))[None]


def kernel(x_prompt, x_sample, cache_k_win, cache_v_win, g_pre, w_in, ln_v_g, ln_v_b, w_spatial,
           b_spatial, sinks, w_proj_a, w_proj_b, w_out, g_post):
    bsz, seq, _ = x_prompt.shape
    dbsz, dseq, _ = x_sample.shape
    assert seq == SEQ and dseq == DEC_SEQ and seq % IN_PROJ_ROWS == 0
    assert (dbsz * dseq) % IN_PROJ_ROWS == 0 and g_pre.shape[0] == 1

    w_in_b = _cast_column_tiles(w_in[0])
    w_a = w_proj_a[0].astype(BF16)
    w_b = w_proj_b[0].astype(BF16)
    w_o = w_out[0].astype(BF16)
    b_s_rows = jnp.repeat(b_spatial[0].T, LANES, axis=1)
    coef = jnp.repeat(jnp.transpose(w_spatial[0][:, :DEC_SEQ, :DEC_SEQ], (2, 1, 0)), LANES, axis=2)
    cos_p, sin_p = _rope_tables(jnp.arange(SEQ))
    cos_s, sin_s = _rope_tables(PAST_LEN + jnp.arange(dseq))

    xp = x_prompt.reshape(bsz * seq, D_MODEL)
    z_p, kvd_p, kv32_p = _in_proj(xp, g_pre, w_in_b, cos_p, sin_p, IN_PROJ_ROWS, 2, 256)
    y_p = _prompt_mix_merge(xp, z_p, kvd_p, ln_v_g, ln_v_b, w_spatial[0], b_s_rows, sinks[0],
                            w_a, w_b, w_o, g_post)
    kv_win_p = kv32_p.reshape(bsz, seq, 2 * KV_WIDTH)[:, seq - WINDOW:]
    k_win_p = kv_win_p[..., :KV_WIDTH].reshape(1, bsz, WINDOW, N_KV_HEADS, HEAD_DIM)
    v_win_p = kv_win_p[..., KV_WIDTH:].reshape(1, bsz, WINDOW, N_KV_HEADS, HEAD_DIM)

    xs = x_sample.reshape(dbsz * dseq, D_MODEL)
    z_s, kvd_s, kv32_s = _in_proj(xs, g_pre, w_in_b, cos_s, sin_s, dseq, 2, 1024)
    a_s, b_s, vn_s, k_win_t, v_win_t = _sample_mixers(
        z_s, kvd_s, kv32_s, _window_first(cache_k_win), _window_first(cache_v_win),
        ln_v_g, ln_v_b, coef, b_s_rows[:dseq], sinks[0])
    y_s = _merge_out(xs, a_s, b_s, z_s, w_a, w_b, w_o, g_post)

    return (y_p.reshape(bsz, seq, D_MODEL),
            y_s.reshape(dbsz, dseq, D_MODEL),
            k_win_p, v_win_p,
            _window_last(k_win_t), _window_last(v_win_t),
            vn_s.reshape(1, dbsz, dseq, A_WIDTH))
```

```python
import functools

import numpy as np
import jax
import jax.numpy as jnp
from jax import lax
from jax.experimental import pallas as pl
from jax.experimental.pallas import tpu as pltpu

D_MODEL = 2048
SEQ = 2048
DEC_SEQ = 8
PAST_LEN = 8192
CHUNK = 128
A_WIDTH = 1024
A_GROUPS = 8
HEAD_DIM = 64
N_HEADS = 16
N_KV_HEADS = 4
Q_PER_KV = N_HEADS // N_KV_HEADS
B_WIDTH = N_HEADS * HEAD_DIM
KV_WIDTH = N_KV_HEADS * HEAD_DIM
WINDOW = 128
ROPE_THETA = 10000.0
EPS = 1e-6
IN_COLS = 3 * A_WIDTH + 2 * B_WIDTH + 2 * KV_WIDTH + 2 * D_MODEL

LANES = 128
SUBLANES = 8
BF16 = jnp.bfloat16
F32 = jnp.float32

COL_TILE = 512
N_COL_TILES = IN_COLS // COL_TILE
Z_BLOCK = 2 * COL_TILE
Z_COLS = (N_COL_TILES - 1) * COL_TILE
COL_GA, COL_GB, COL_U, COL_V, COL_ZA, COL_Q, COL_ZB = 0, 2, 4, 5, 6, 7, 8
KVD_WIDTH = 2 * N_KV_HEADS * LANES
SEC_GATE_NORM, SEC_GATE, SEC_GELU, SEC_SILU, SEC_Q, SEC_KV = range(6)


def _walk(tiles_per_step):
    sections = [((11, 12, 13, 14, 15, 16, 17, 18), SEC_GATE), ((0, 1, 2, 3), SEC_GELU),
                ((4, 5), SEC_SILU), ((6, 7), SEC_Q), ((9, 10), SEC_SILU)]
    steps = [(tiles[k:k + tiles_per_step], sec)
             for tiles, sec in sections for k in range(0, len(tiles), tiles_per_step)]
    steps = [(tiles, sec, s) for s, (tiles, sec) in enumerate(steps)]
    steps[0] = (steps[0][0], SEC_GATE_NORM, 0)
    mid = len(steps) // 2
    steps.insert(mid, ((8,) * tiles_per_step, SEC_KV, steps[mid - 1][2]))
    return steps


def _lookup(values, j):
    out = jnp.int32(values[-1])
    for k in range(len(values) - 2, -1, -1):
        out = jnp.where(j == k, jnp.int32(values[k]), out)
    return out


IN_PROJ_ROWS = 1024
MERGE_ROWS = 512
VMEM_LIMIT = 56 * 1024 * 1024


def _gelu(x):
    return 0.5 * x * (1.0 + lax.erf(x * np.float32(1.0 / np.sqrt(2.0))))


def _sigmoid(x):
    return 1.0 / (1.0 + jnp.exp(-x))


def _rope(x, cos, sin_signed):
    width = x.shape[1]
    lane = lax.broadcasted_iota(jnp.int32, x.shape, 1)
    first_half = (lane & (HEAD_DIM - 1)) < (HEAD_DIM // 2)
    partner = jnp.where(first_half,
                        pltpu.roll(x, width - HEAD_DIM // 2, axis=1),
                        pltpu.roll(x, HEAD_DIM // 2, axis=1))
    reps = width // LANES
    cos_w = jnp.concatenate([cos] * reps, axis=1) if reps > 1 else cos
    sin_w = jnp.concatenate([sin_signed] * reps, axis=1) if reps > 1 else sin_signed
    return x * cos_w + partner * sin_w


def _dup_heads(x):
    lane = lax.broadcasted_iota(jnp.int32, (x.shape[0], LANES), 1)
    low = lane < HEAD_DIM
    out = []
    for c in range(x.shape[1] // LANES):
        xc = x[:, c * LANES:(c + 1) * LANES]
        sw = pltpu.roll(xc, HEAD_DIM, axis=1)
        out += [jnp.where(low, xc, sw), jnp.where(low, sw, xc)]
    return jnp.concatenate(out, axis=1)


def _in_proj_kernel(*refs, table_rows, sub, walk, x_parts):
    n_w = len(walk[0][0])
    x_refs = refs[:x_parts]
    g_ref = refs[x_parts]
    w_refs = refs[x_parts + 1:x_parts + 1 + n_w]
    cos_ref, sin_ref, z_ref, kvd_ref, kv32_ref, h_scr = refs[x_parts + 1 + n_w:]
    j = pl.program_id(1)
    part_rows = x_refs[0].shape[0]
    tm = part_rows * x_parts

    def tables(rs):
        if table_rows == tm:
            return cos_ref[rs, :], sin_ref[rs, :]
        tile = lambda t: jnp.broadcast_to(t[None], (sub // table_rows, table_rows, LANES)).reshape(sub, LANES)
        return tile(cos_ref[...]), tile(sin_ref[...])

    def run(epilogue, norm=False, n_tiles=n_w):
        for r in range(tm // sub):
            rs = slice(r * sub, (r + 1) * sub)
            if norm:
                start = (r * sub) % part_rows
                x = x_refs[(r * sub) // part_rows][start:start + sub, :]
                ms = jnp.mean(x * x, axis=-1, keepdims=True)
                h = (x * lax.rsqrt(ms + EPS) * g_ref[...]).astype(BF16)
                h_scr[rs, :] = h
            else:
                h = h_scr[rs, :]
            for t in range(n_tiles):
                acc = jnp.dot(h, w_refs[t][...], preferred_element_type=F32)
                epilogue(acc, rs, slice(t * COL_TILE, (t + 1) * COL_TILE))

    def gate_epi(acc, rs, cs):
        z_ref[rs, cs] = _sigmoid(acc).astype(BF16)

    def gelu_epi(acc, rs, cs):
        z_ref[rs, cs] = _gelu(acc).astype(BF16)

    def silu_epi(acc, rs, cs):
        z_ref[rs, cs] = (acc * _sigmoid(acc)).astype(BF16)

    def q_epi(acc, rs, cs):
        cos, sin = tables(rs)
        z_ref[rs, cs] = (_rope(acc, cos, sin) * np.float32(HEAD_DIM ** -0.5)).astype(BF16)

    def kv_epi(acc, rs, cs):
        cos, sin = tables(rs)
        k = _rope(acc[:, :KV_WIDTH], cos, sin)
        v = acc[:, KV_WIDTH:]
        kv32_ref[rs, :KV_WIDTH] = k
        kv32_ref[rs, KV_WIDTH:] = v
        kvd_ref[rs, :KVD_WIDTH // 2] = _dup_heads(k).astype(BF16)
        kvd_ref[rs, KVD_WIDTH // 2:] = _dup_heads(v).astype(BF16)

    sec = _lookup(tuple(sec for _, sec, _ in walk), j)
    pl.when(sec == SEC_GATE_NORM)(lambda: run(gate_epi, norm=True))
    pl.when(sec == SEC_GATE)(lambda: run(gate_epi))
    pl.when(sec == SEC_GELU)(lambda: run(gelu_epi))
    pl.when(sec == SEC_SILU)(lambda: run(silu_epi))
    pl.when(sec == SEC_Q)(lambda: run(q_epi))
    pl.when(sec == SEC_KV)(lambda: run(kv_epi, n_tiles=1))


def _in_proj(x2d, g_pre, w_in_b, cos_t, sin_t, table_rows, tiles_per_step, sub, x_parts):
    n = x2d.shape[0]
    tm = IN_PROJ_ROWS
    walk = _walk(tiles_per_step)
    n_steps = len(walk)
    if table_rows == tm:
        n_tab = cos_t.shape[0] // tm
        table_spec = pl.BlockSpec((tm, LANES), lambda i, j: (i % n_tab, 0))
    else:
        table_spec = pl.BlockSpec((table_rows, LANES), lambda i, j: (0, 0))
    w_specs = [pl.BlockSpec((None, D_MODEL, COL_TILE),
                            lambda i, j, t=t: (_lookup(tuple(tiles[t] for tiles, _, _ in walk), j), 0, 0))
               for t in range(tiles_per_step)]
    steps_per_block = Z_BLOCK // (tiles_per_step * COL_TILE)

    def z_index(i, j):
        s = _lookup(tuple(z_step for _, _, z_step in walk), j)
        return (s // steps_per_block, i, s % steps_per_block)

    n_row_tiles = n // tm
    assert sub <= tm // x_parts and x_parts < n_steps

    def x_index(i, j, p):
        tile = jnp.minimum(i + (j >= n_steps - p).astype(jnp.int32), n_row_tiles - 1)
        return (tile * x_parts + p, 0)

    x_specs = [pl.BlockSpec((tm // x_parts, D_MODEL), functools.partial(x_index, p=p))
               for p in range(x_parts)]
    return pl.pallas_call(
        functools.partial(_in_proj_kernel, table_rows=table_rows, sub=sub, walk=walk, x_parts=x_parts),
        grid=(n_row_tiles, n_steps),
        in_specs=[
            *x_specs,
            pl.BlockSpec((1, D_MODEL), lambda i, j: (0, 0)),
            *w_specs,
            table_spec, table_spec,
        ],
        out_specs=[
            pl.BlockSpec((None, tm, tiles_per_step * COL_TILE), z_index),
            pl.BlockSpec((tm, KVD_WIDTH), lambda i, j: (i, 0)),
            pl.BlockSpec((tm, 2 * KV_WIDTH), lambda i, j: (i, 0)),
        ],
        out_shape=[
            jax.ShapeDtypeStruct((Z_COLS // Z_BLOCK, n, Z_BLOCK), BF16),
            jax.ShapeDtypeStruct((n, KVD_WIDTH), BF16),
            jax.ShapeDtypeStruct((n, 2 * KV_WIDTH), F32),
        ],
        scratch_shapes=[pltpu.VMEM((tm, D_MODEL), BF16)],
        compiler_params=pltpu.CompilerParams(
            dimension_semantics=("parallel", "arbitrary"), vmem_limit_bytes=VMEM_LIMIT),
        name="in_proj",
    )(*([x2d] * x_parts), g_pre, *([w_in_b] * tiles_per_step), cos_t, sin_t)


def _layernorm_v(v_ref, lng_ref, lnb_ref):
    vg = v_ref[...].astype(F32)
    mu = jnp.mean(vg, axis=-1, keepdims=True)
    xc = vg - mu
    var = jnp.mean(xc * xc, axis=-1, keepdims=True)
    return xc * lax.rsqrt(var + EPS) * lng_ref[...] + lnb_ref[...]


def _mix_block(u, v, za, q, zb, kvd, kvd_prev, first_block, lng, lnb, ws_ref, bs_ref, sinks_ref,
               store_a, store_b):
    row = lax.broadcasted_iota(jnp.int32, (CHUNK, CHUNK), 0)
    col = lax.broadcasted_iota(jnp.int32, (CHUNK, CHUNK), 1)

    vg = v.astype(F32)
    mu = jnp.mean(vg, axis=-1, keepdims=True)
    xc = vg - mu
    var = jnp.mean(xc * xc, axis=-1, keepdims=True)
    vn_b = (xc * lax.rsqrt(var + EPS) * lng + lnb).astype(BF16)
    for g in range(A_GROUPS):
        cols = slice(g * LANES, (g + 1) * LANES)
        w = jnp.where(col <= row, ws_ref[g], 0.0).astype(BF16)
        s = jnp.dot(w, vn_b[:, cols], preferred_element_type=F32) + bs_ref[:, cols]
        store_a(cols, ((u[:, cols].astype(F32) * s) * za[:, cols].astype(F32)).astype(BF16))

    t = lax.broadcasted_iota(jnp.int32, (WINDOW, 2 * WINDOW), 0)
    jj = lax.broadcasted_iota(jnp.int32, (WINDOW, 2 * WINDOW), 1)
    mask = (jj > t) & (jj <= t + WINDOW) & ((jj >= WINDOW) | jnp.logical_not(first_block))
    lane = lax.broadcasted_iota(jnp.int32, (WINDOW, LANES), 1)
    low_half = lane < HEAD_DIM
    for m in range(N_HEADS // 2):
        cols = slice(m * LANES, (m + 1) * LANES)
        kcols = slice((m // 2) * LANES, (m // 2 + 1) * LANES)
        vcols = slice(KVD_WIDTH // 2 + (m // 2) * LANES, KVD_WIDTH // 2 + (m // 2 + 1) * LANES)
        k_c = jnp.concatenate([kvd_prev[:, kcols], kvd[:, kcols]], axis=0)
        v_c = jnp.concatenate([kvd_prev[:, vcols], kvd[:, vcols]], axis=0)
        q_m = q[:, cols]
        outs = []
        for half in range(2):
            sink = sinks_ref[2 * m + half]
            q_h = jnp.where(low_half if half == 0 else ~low_half, q_m, jnp.zeros_like(q_m))
            s = lax.dot_general(q_h, k_c, (((1,), (1,)), ((), ())), preferred_element_type=F32)
            s = jnp.where(mask, s, -jnp.inf)
            mx = jnp.maximum(jnp.max(s, axis=-1, keepdims=True), sink)
            p = jnp.exp(s - mx)
            denom = jnp.sum(p, axis=-1, keepdims=True) + jnp.exp(sink - mx)
            o = jnp.dot(p.astype(BF16), v_c, preferred_element_type=F32)
            outs.append(o * (1.0 / denom))
        o_m = jnp.where(low_half, outs[0], outs[1])
        store_b(cols, (o_m * zb[:, cols].astype(F32)).astype(BF16))


def _merge(x, a, b, ga_ref, gb_ref, wa_ref, wb_ref, wo_ref, g_ref):
    gate = lambda ref: jnp.concatenate([ref[c] for c in range(ref.shape[0])], axis=1).astype(F32)
    pa = jnp.dot(a, wa_ref[...], preferred_element_type=F32)
    pb = jnp.dot(b, wb_ref[...], preferred_element_type=F32)
    merged = gate(ga_ref) * pa + gate(gb_ref) * pb
    out = jnp.dot(merged.astype(BF16), wo_ref[...], preferred_element_type=F32)
    ms = jnp.mean(out * out, axis=-1, keepdims=True)
    return x + out * lax.rsqrt(ms + EPS) * g_ref[...]


MIX_ROWS = 256
MIX_BLOCKS = MIX_ROWS // CHUNK


def _prompt_mix_merge_kernel(u_ref, v_ref, za_ref, q_ref, zb_ref, kvd_ref, kvdp_ref, x_ref, ga_ref,
                             gb_ref, lng_ref, lnb_ref, ws_ref, bs_ref, sinks_ref, wa_ref, wb_ref,
                             wo_ref, g_ref, y_ref, a_scr, b_scr):
    i = pl.program_id(0)
    last_tile = pl.num_programs(0) - 2
    tile = jnp.minimum(i, last_tile)
    slot = i % 2

    @pl.when(i == 0)
    def _():
        a_scr[1] = jnp.zeros(a_scr.shape[1:], BF16)
        b_scr[1] = jnp.zeros(b_scr.shape[1:], BF16)

    for blk in range(MIX_BLOCKS):
        rows = slice(blk * CHUNK, (blk + 1) * CHUNK)
        first_block = ((tile * MIX_BLOCKS + blk) % (SEQ // WINDOW)) == 0
        kvd_prev = kvdp_ref[...] if blk == 0 else kvd_ref[(blk - 1) * CHUNK:blk * CHUNK, :]

        def store_a(cols, val, rows=rows):
            a_scr[slot, rows, cols] = val

        def store_b(cols, val, rows=rows):
            b_scr[slot, rows, cols] = val

        _mix_block(u_ref[rows, :], v_ref[rows, :], za_ref[rows, :], q_ref[rows, :], zb_ref[rows, :],
                   kvd_ref[rows, :], kvd_prev, first_block, lng_ref[...], lnb_ref[...], ws_ref, bs_ref,
                   sinks_ref, store_a, store_b)

    y_ref[...] = _merge(x_ref[...], a_scr[1 - slot], b_scr[1 - slot], ga_ref, gb_ref,
                        wa_ref, wb_ref, wo_ref, g_ref)


def _prompt_mix_merge(x2d, z, kvd, ln_g, ln_b, w_s, b_s_rows, sinks, w_a, w_b, w_o, g_post):
    n = x2d.shape[0]
    tm = MIX_ROWS
    n_tiles = n // tm
    mix_tile = lambda i: jnp.minimum(i, n_tiles - 1)
    merge_tile = lambda i: jnp.maximum(i - 1, 0)
    wide = lambda cb: pl.BlockSpec((None, tm, Z_BLOCK), lambda i, cb=cb: (cb, mix_tile(i), 0))
    gate = lambda cb: pl.BlockSpec((D_MODEL // Z_BLOCK, tm, Z_BLOCK),
                                   lambda i, cb=cb: (cb // (D_MODEL // Z_BLOCK), merge_tile(i), 0))
    const2 = lambda shape: pl.BlockSpec(shape, lambda i: (0, 0))
    resident = lambda shape: pl.BlockSpec(shape, lambda i: (0, 0), pipeline_mode=pl.Buffered(1))
    return pl.pallas_call(
        _prompt_mix_merge_kernel,
        grid=(n_tiles + 1,),
        in_specs=[
            wide(COL_U), wide(COL_V), wide(COL_ZA), wide(COL_Q), wide(COL_ZB),
            pl.BlockSpec((tm, KVD_WIDTH), lambda i: (mix_tile(i), 0)),
            pl.BlockSpec((CHUNK, KVD_WIDTH), lambda i: (jnp.maximum(mix_tile(i) * MIX_BLOCKS - 1, 0), 0)),
            pl.BlockSpec((tm, D_MODEL), lambda i: (merge_tile(i), 0)),
            gate(COL_GA), gate(COL_GB),
            const2((1, A_WIDTH)), const2((1, A_WIDTH)),
            pl.BlockSpec((A_GROUPS, CHUNK, CHUNK), lambda i: (0, 0, 0)),
            const2((CHUNK, A_WIDTH)),
            pl.BlockSpec(memory_space=pltpu.SMEM),
            resident((A_WIDTH, D_MODEL)),
            resident((B_WIDTH, D_MODEL)),
            resident((D_MODEL, D_MODEL)),
            resident((1, D_MODEL)),
        ],
        out_specs=pl.BlockSpec((tm, D_MODEL), lambda i: (merge_tile(i), 0)),
        out_shape=jax.ShapeDtypeStruct((n, D_MODEL), F32),
        scratch_shapes=[pltpu.VMEM((2, tm, A_WIDTH), BF16), pltpu.VMEM((2, tm, B_WIDTH), BF16)],
        compiler_params=pltpu.CompilerParams(
            dimension_semantics=("arbitrary",), vmem_limit_bytes=VMEM_LIMIT),
        name="prompt_mix_merge",
    )(z, z, z, z, z, kvd, kvd, x2d, z, z, ln_g, ln_b, w_s, b_s_rows, sinks, w_a, w_b, w_o, g_post)


SEQS_PER_STEP = CHUNK // DEC_SEQ


def _sample_mixers_kernel(u_ref, v_ref, za_ref, q_ref, zb_ref, kvd_ref, kv32_ref, ckt_ref, cvt_ref,
                          lng_ref, lnb_ref, coef_ref, bs_ref, sinks_ref,
                          a_ref, b_ref, vn_ref, kwin_ref, vwin_ref):
    ns = SEQS_PER_STEP
    split = lambda x: x.reshape(ns, DEC_SEQ, x.shape[-1])

    vn = _layernorm_v(v_ref, lng_ref, lnb_ref)
    vn_ref[...] = vn
    vn3 = split(vn)
    t_row = lax.broadcasted_iota(jnp.int32, (DEC_SEQ, A_WIDTH), 0)
    s_acc = jnp.broadcast_to(bs_ref[...][None], (ns, DEC_SEQ, A_WIDTH))
    for s in range(DEC_SEQ):
        coef = jnp.where(t_row >= s, coef_ref[s], 0.0)
        s_acc = s_acc + vn3[:, s:s + 1, :] * coef[None]
    a3 = split(u_ref[...].astype(F32)) * s_acc * split(za_ref[...].astype(F32))
    a_ref[...] = a3.reshape(CHUNK, A_WIDTH).astype(BF16)

    rows_q = Q_PER_KV * DEC_SEQ
    lane3 = lax.broadcasted_iota(jnp.int32, (ns, DEC_SEQ, LANES), 2)
    low3 = lane3 < HEAD_DIM
    r_idx = lax.broadcasted_iota(jnp.int32, (ns, rows_q, LANES), 1)
    l_idx = lax.broadcasted_iota(jnp.int32, (ns, rows_q, LANES), 2)
    s_idx = lax.broadcasted_iota(jnp.int32, (ns, rows_q, LANES), 0)
    t_q = r_idx & (DEC_SEQ - 1)
    mask_cache = l_idx > t_q
    mask_new = ((l_idx >> 3) == s_idx) & ((l_idx & (DEC_SEQ - 1)) <= t_q)
    head_of_row = lax.broadcasted_iota(jnp.int32, (1, rows_q, 1), 1) >> 3
    q3 = split(q_ref[...].astype(F32))
    zb3 = split(zb_ref[...].astype(F32))
    for kh in range(N_KV_HEADS):
        pieces = []
        for m in (2 * kh, 2 * kh + 1):
            q_m = q3[:, :, m * LANES:(m + 1) * LANES]
            pieces += [jnp.where(low3, q_m, 0.0), jnp.where(low3, 0.0, q_m)]
        lhs = jnp.concatenate(pieces, axis=1).astype(BF16)
        rows = slice(kh * HEAD_DIM, (kh + 1) * HEAD_DIM)
        kt = ckt_ref[:, rows, :].astype(BF16)
        vt = cvt_ref[:, rows, :].astype(BF16)
        kt2 = jnp.concatenate([kt, kt], axis=1)
        vt2 = jnp.concatenate([vt, vt], axis=1)
        k_new = kvd_ref[:, kh * LANES:(kh + 1) * LANES]
        v_new = kvd_ref[:, KVD_WIDTH // 2 + kh * LANES:KVD_WIDTH // 2 + (kh + 1) * LANES]
        s_c = lax.dot_general(lhs, kt2, (((2,), (1,)), ((0,), (0,))), preferred_element_type=F32)
        s_n = lax.dot_general(lhs.reshape(ns * rows_q, LANES), k_new, (((1,), (1,)), ((), ())),
                              preferred_element_type=F32).reshape(ns, rows_q, LANES)
        s_c = jnp.where(mask_cache, s_c, -jnp.inf)
        s_n = jnp.where(mask_new, s_n, -jnp.inf)
        sink = jnp.zeros((1, rows_q, 1), F32)
        for jq in range(Q_PER_KV):
            sink = jnp.where(head_of_row == jq, sinks_ref[Q_PER_KV * kh + jq], sink)
        mx = jnp.maximum(jnp.maximum(jnp.max(s_c, axis=-1, keepdims=True),
                                     jnp.max(s_n, axis=-1, keepdims=True)), sink)
        p_c = jnp.exp(s_c - mx)
        p_n = jnp.exp(s_n - mx)
        denom = (jnp.sum(p_c, axis=-1, keepdims=True) + jnp.sum(p_n, axis=-1, keepdims=True)
                 + jnp.exp(sink - mx))
        o = lax.dot_general(p_c.astype(BF16), vt2, (((2,), (2,)), ((0,), (0,))),
                            preferred_element_type=F32)
        o = o + jnp.dot(p_n.reshape(ns * rows_q, LANES).astype(BF16), v_new,
                        preferred_element_type=F32).reshape(ns, rows_q, LANES)
        o = o * (1.0 / denom)
        for i, m in enumerate((2 * kh, 2 * kh + 1)):
            base = 2 * DEC_SEQ * i
            o_m = jnp.where(low3, o[:, base:base + DEC_SEQ, :], o[:, base + DEC_SEQ:base + 2 * DEC_SEQ, :])
            cols = slice(m * LANES, (m + 1) * LANES)
            b_ref[:, cols] = (o_m * zb3[:, :, cols]).reshape(CHUNK, LANES).astype(BF16)

    keep = WINDOW - DEC_SEQ
    lane_w = lax.broadcasted_iota(jnp.int32, (KV_WIDTH, WINDOW), 1)
    for new_rows, cache_ref, out_ref in ((kv32_ref[:, :KV_WIDTH], ckt_ref, kwin_ref),
                                         (kv32_ref[:, KV_WIDTH:], cvt_ref, vwin_ref)):
        new_t = new_rows.T
        old = pltpu.roll(cache_ref[...].reshape(ns * KV_WIDTH, WINDOW), keep, axis=1)
        for s in range(ns):
            shifted_new = pltpu.roll(new_t, (keep - DEC_SEQ * s) % WINDOW, axis=1)
            out_ref[s] = jnp.where(lane_w < keep, old[s * KV_WIDTH:(s + 1) * KV_WIDTH], shifted_new)


def _sample_mixers(z, kvd, kv32, cache_kt, cache_vt, ln_g, ln_b, coef, b_s8, sinks):
    n = kvd.shape[0]
    n_seq = cache_kt.shape[0]
    wide = lambda cb: pl.BlockSpec((None, CHUNK, Z_BLOCK), lambda r, cb=cb: (cb, r, 0))
    const2 = lambda shape: pl.BlockSpec(shape, lambda r: (0, 0))
    win = pl.BlockSpec((SEQS_PER_STEP, KV_WIDTH, WINDOW), lambda r: (r, 0, 0))
    return pl.pallas_call(
        _sample_mixers_kernel,
        grid=(n // CHUNK,),
        in_specs=[
            wide(COL_U), wide(COL_V), wide(COL_ZA), wide(COL_Q), wide(COL_ZB),
            pl.BlockSpec((CHUNK, KVD_WIDTH), lambda r: (r, 0)),
            pl.BlockSpec((CHUNK, 2 * KV_WIDTH), lambda r: (r, 0)),
            win, win,
            const2((1, A_WIDTH)), const2((1, A_WIDTH)),
            pl.BlockSpec((DEC_SEQ, DEC_SEQ, A_WIDTH), lambda r: (0, 0, 0)),
            const2((DEC_SEQ, A_WIDTH)),
            pl.BlockSpec(memory_space=pltpu.SMEM),
        ],
        out_specs=[
            pl.BlockSpec((CHUNK, A_WIDTH), lambda r: (r, 0)),
            pl.BlockSpec((CHUNK, B_WIDTH), lambda r: (r, 0)),
            pl.BlockSpec((CHUNK, A_WIDTH), lambda r: (r, 0)),
            win, win,
        ],
        out_shape=[
            jax.ShapeDtypeStruct((n, A_WIDTH), BF16),
            jax.ShapeDtypeStruct((n, B_WIDTH), BF16),
            jax.ShapeDtypeStruct((n, A_WIDTH), F32),
            jax.ShapeDtypeStruct((n_seq, KV_WIDTH, WINDOW), F32),
            jax.ShapeDtypeStruct((n_seq, KV_WIDTH, WINDOW), F32),
        ],
        compiler_params=pltpu.CompilerParams(
            dimension_semantics=("parallel",), vmem_limit_bytes=VMEM_LIMIT),
        name="sample_mixers",
    )(z, z, z, z, z, kvd, kv32, cache_kt, cache_vt, ln_g, ln_b, coef, b_s8, sinks)


def _merge_out_kernel(x_ref, a_ref, b_ref, ga_ref, gb_ref, wa_ref, wb_ref, wo_ref, g_ref, y_ref):
    y_ref[...] = _merge(x_ref[...], a_ref[...], b_ref[...], ga_ref, gb_ref,
                        wa_ref, wb_ref, wo_ref, g_ref)


def _merge_out(x2d, a, b, z, w_a, w_b, w_o, g_post):
    n = x2d.shape[0]
    tm = MERGE_ROWS
    resident = lambda shape: pl.BlockSpec(shape, lambda i: (0, 0), pipeline_mode=pl.Buffered(1))
    return pl.pallas_call(
        _merge_out_kernel,
        grid=(n // tm,),
        in_specs=[
            pl.BlockSpec((tm, D_MODEL), lambda i: (i, 0)),
            pl.BlockSpec((tm, A_WIDTH), lambda i: (i, 0)),
            pl.BlockSpec((tm, B_WIDTH), lambda i: (i, 0)),
            pl.BlockSpec((D_MODEL // Z_BLOCK, tm, Z_BLOCK), lambda i: (COL_GA // (D_MODEL // Z_BLOCK), i, 0)),
            pl.BlockSpec((D_MODEL // Z_BLOCK, tm, Z_BLOCK), lambda i: (COL_GB // (D_MODEL // Z_BLOCK), i, 0)),
            resident((A_WIDTH, D_MODEL)),
            resident((B_WIDTH, D_MODEL)),
            resident((D_MODEL, D_MODEL)),
            resident((1, D_MODEL)),
        ],
        out_specs=pl.BlockSpec((tm, D_MODEL), lambda i: (i, 0)),
        out_shape=jax.ShapeDtypeStruct((n, D_MODEL), F32),
        compiler_params=pltpu.CompilerParams(
            dimension_semantics=("parallel",), vmem_limit_bytes=VMEM_LIMIT),
        name="merge_out",
    )(x2d, a, b, z, z, w_a, w_b, w_o, g_post)


def _cast_tile_kernel(w_ref, o_ref):
    o_ref[...] = w_ref[...].astype(BF16)


def _cast_column_tiles(w):
    rows, cols = w.shape
    return pl.pallas_call(
        _cast_tile_kernel,
        grid=(cols // COL_TILE,),
        in_specs=[pl.BlockSpec((rows, COL_TILE), lambda t: (0, t))],
        out_specs=pl.BlockSpec((None, rows, COL_TILE), lambda t: (t, 0, 0)),
        out_shape=jax.ShapeDtypeStruct((cols // COL_TILE, rows, COL_TILE), BF16),
        compiler_params=pltpu.CompilerParams(
            dimension_semantics=("parallel",), vmem_limit_bytes=VMEM_LIMIT),
        name="cast_column_tiles",
    )(w)


def _rope_tables(pos):
    lane = jnp.arange(LANES)
    inv = ROPE_THETA ** (-(2 * (lane % (HEAD_DIM // 2))).astype(F32) / HEAD_DIM)
    ang = pos.astype(F32)[:, None] * inv[None, :]
    sign = jnp.where((lane % HEAD_DIM) < HEAD_DIM // 2, -1.0, 1.0).astype(F32)
    return jnp.cos(ang), jnp.sin(ang) * sign[None, :]


def _window_first(win):
    n = win.shape[1]
    return jnp.transpose(win[0], (0, 2, 3, 1)).reshape(n, KV_WIDTH, WINDOW)


def _window_last(win_t):
    n = win_t.shape[0]
    return jnp.transpose(win_t.reshape(n, N_KV_HEADS, HEAD_DIM, WINDOW), (0, 3, 1, 2))[None]


def kernel(x_prompt, x_sample, cache_k_win, cache_v_win, g_pre, w_in, ln_v_g, ln_v_b, w_spatial,
           b_spatial, sinks, w_proj_a, w_proj_b, w_out, g_post):
    bsz, seq, _ = x_prompt.shape
    dbsz, dseq, _ = x_sample.shape
    assert seq == SEQ and dseq == DEC_SEQ and seq % IN_PROJ_ROWS == 0
    assert (dbsz * dseq) % IN_PROJ_ROWS == 0 and g_pre.shape[0] == 1

    w_in_b = _cast_column_tiles(w_in[0])
    w_a = w_proj_a[0].astype(BF16)
    w_b = w_proj_b[0].astype(BF16)
    w_o = w_out[0].astype(BF16)
    b_s_rows = jnp.repeat(b_spatial[0].T, LANES, axis=1)
    coef = jnp.repeat(jnp.transpose(w_spatial[0][:, :DEC_SEQ, :DEC_SEQ], (2, 1, 0)), LANES, axis=2)
    cos_p, sin_p = _rope_tables(jnp.arange(SEQ))
    cos_s, sin_s = _rope_tables(PAST_LEN + jnp.arange(dseq))

    xp = x_prompt.reshape(bsz * seq, D_MODEL)
    z_p, kvd_p, kv32_p = _in_proj(xp, g_pre, w_in_b, cos_p, sin_p, IN_PROJ_ROWS, 2, 256, 4)
    y_p = _prompt_mix_merge(xp, z_p, kvd_p, ln_v_g, ln_v_b, w_spatial[0], b_s_rows, sinks[0],
                            w_a, w_b, w_o, g_post)
    kv_win_p = kv32_p.reshape(bsz, seq, 2 * KV_WIDTH)[:, seq - WINDOW:]
    k_win_p = kv_win_p[..., :KV_WIDTH].reshape(1, bsz, WINDOW, N_KV_HEADS, HEAD_DIM)
    v_win_p = kv_win_p[..., KV_WIDTH:].reshape(1, bsz, WINDOW, N_KV_HEADS, HEAD_DIM)

    xs = x_sample.reshape(dbsz * dseq, D_MODEL)
    z_s, kvd_s, kv32_s = _in_proj(xs, g_pre, w_in_b, cos_s, sin_s, dseq, 2, 1024, 1)
    a_s, b_s, vn_s, k_win_t, v_win_t = _sample_mixers(
        z_s, kvd_s, kv32_s, _window_first(cache_k_win), _window_first(cache_v_win),
        ln_v_g, ln_v_b, coef, b_s_rows[:dseq], sinks[0])
    y_s = _merge_out(xs, a_s, b_s, z_s, w_a, w_b, w_o, g_post)

    return (y_p.reshape(bsz, seq, D_MODEL),
            y_s.reshape(dbsz, dseq, D_MODEL),
            k_win_p, v_win_p,
            _window_last(k_win_t), _window_last(v_win_t),
            vn_s.reshape(1, dbsz, dseq, A_WIDTH))
```

```python
import functools

import numpy as np
import jax
import jax.numpy as jnp
from jax import lax
from jax.experimental import pallas as pl
from jax.experimental.pallas import tpu as pltpu

D_MODEL = 2048
SEQ = 2048
DEC_SEQ = 8
PAST_LEN = 8192
CHUNK = 128
A_WIDTH = 1024
A_GROUPS = 8
HEAD_DIM = 64
N_HEADS = 16
N_KV_HEADS = 4
Q_PER_KV = N_HEADS // N_KV_HEADS
B_WIDTH = N_HEADS * HEAD_DIM
KV_WIDTH = N_KV_HEADS * HEAD_DIM
WINDOW = 128
ROPE_THETA = 10000.0
EPS = 1e-6
IN_COLS = 3 * A_WIDTH + 2 * B_WIDTH + 2 * KV_WIDTH + 2 * D_MODEL

LANES = 128
SUBLANES = 8
BF16 = jnp.bfloat16
F32 = jnp.float32

COL_TILE = 512
N_COL_TILES = IN_COLS // COL_TILE
Z_BLOCK = 2 * COL_TILE
Z_COLS = (N_COL_TILES - 1) * COL_TILE
COL_GA, COL_GB, COL_U, COL_V, COL_ZA, COL_Q, COL_ZB = 0, 2, 4, 5, 6, 7, 8
KVD_WIDTH = 2 * N_KV_HEADS * LANES
SEC_GATE_NORM, SEC_GATE, SEC_GELU, SEC_SILU, SEC_Q, SEC_KV = range(6)


def _walk(tiles_per_step):
    sections = [((11, 12, 13, 14, 15, 16, 17, 18), SEC_GATE), ((0, 1, 2, 3), SEC_GELU),
                ((4, 5), SEC_SILU), ((6, 7), SEC_Q), ((9, 10), SEC_SILU)]
    steps = [(tiles[k:k + tiles_per_step], sec)
             for tiles, sec in sections for k in range(0, len(tiles), tiles_per_step)]
    steps = [(tiles, sec, s) for s, (tiles, sec) in enumerate(steps)]
    steps[0] = (steps[0][0], SEC_GATE_NORM, 0)
    mid = len(steps) // 2
    steps.insert(mid, ((8,) * tiles_per_step, SEC_KV, steps[mid - 1][2]))
    return steps


def _lookup(values, j):
    out = jnp.int32(values[-1])
    for k in range(len(values) - 2, -1, -1):
        out = jnp.where(j == k, jnp.int32(values[k]), out)
    return out


IN_PROJ_ROWS = 1024
MERGE_ROWS = 512
VMEM_LIMIT = 56 * 1024 * 1024


def _gelu(x):
    return 0.5 * x * (1.0 + lax.erf(x * np.float32(1.0 / np.sqrt(2.0))))


def _sigmoid(x):
    return 1.0 / (1.0 + jnp.exp(-x))


def _rope(x, cos, sin_signed):
    width = x.shape[1]
    lane = lax.broadcasted_iota(jnp.int32, x.shape, 1)
    first_half = (lane & (HEAD_DIM - 1)) < (HEAD_DIM // 2)
    partner = jnp.where(first_half,
                        pltpu.roll(x, width - HEAD_DIM // 2, axis=1),
                        pltpu.roll(x, HEAD_DIM // 2, axis=1))
    reps = width // LANES
    cos_w = jnp.concatenate([cos] * reps, axis=1) if reps > 1 else cos
    sin_w = jnp.concatenate([sin_signed] * reps, axis=1) if reps > 1 else sin_signed
    return x * cos_w + partner * sin_w


def _dup_heads(x):
    lane = lax.broadcasted_iota(jnp.int32, (x.shape[0], LANES), 1)
    low = lane < HEAD_DIM
    out = []
    for c in range(x.shape[1] // LANES):
        xc = x[:, c * LANES:(c + 1) * LANES]
        sw = pltpu.roll(xc, HEAD_DIM, axis=1)
        out += [jnp.where(low, xc, sw), jnp.where(low, sw, xc)]
    return jnp.concatenate(out, axis=1)


def _in_proj_kernel(*refs, table_rows, sub, walk, x_parts):
    n_w = len(walk[0][0])
    x_refs = refs[:x_parts]
    g_ref = refs[x_parts]
    w_refs = refs[x_parts + 1:x_parts + 1 + n_w]
    cos_ref, sin_ref, z_ref, kvd_ref, kv32_ref, h_scr = refs[x_parts + 1 + n_w:]
    j = pl.program_id(1)
    part_rows = x_refs[0].shape[0]
    tm = part_rows * x_parts

    def tables(rs):
        if table_rows == tm:
            return cos_ref[rs, :], sin_ref[rs, :]
        tile = lambda t: jnp.broadcast_to(t[None], (sub // table_rows, table_rows, LANES)).reshape(sub, LANES)
        return tile(cos_ref[...]), tile(sin_ref[...])

    def run(epilogue, norm=False, n_tiles=n_w):
        for r in range(tm // sub):
            rs = slice(r * sub, (r + 1) * sub)
            if norm:
                start = (r * sub) % part_rows
                x = x_refs[(r * sub) // part_rows][start:start + sub, :]
                ms = jnp.mean(x * x, axis=-1, keepdims=True)
                h = (x * lax.rsqrt(ms + EPS) * g_ref[...]).astype(BF16)
                h_scr[rs, :] = h
            else:
                h = h_scr[rs, :]
            for t in range(n_tiles):
                acc = jnp.dot(h, w_refs[t][...], preferred_element_type=F32)
                epilogue(acc, rs, slice(t * COL_TILE, (t + 1) * COL_TILE))

    def gate_epi(acc, rs, cs):
        z_ref[rs, cs] = _sigmoid(acc).astype(BF16)

    def gelu_epi(acc, rs, cs):
        z_ref[rs, cs] = _gelu(acc).astype(BF16)

    def silu_epi(acc, rs, cs):
        z_ref[rs, cs] = (acc * _sigmoid(acc)).astype(BF16)

    def q_epi(acc, rs, cs):
        cos, sin = tables(rs)
        z_ref[rs, cs] = (_rope(acc, cos, sin) * np.float32(HEAD_DIM ** -0.5)).astype(BF16)

    def kv_epi(acc, rs, cs):
        cos, sin = tables(rs)
        k = _rope(acc[:, :KV_WIDTH], cos, sin)
        v = acc[:, KV_WIDTH:]
        kv32_ref[rs, :KV_WIDTH] = k
        kv32_ref[rs, KV_WIDTH:] = v
        kvd_ref[rs, :KVD_WIDTH // 2] = _dup_heads(k).astype(BF16)
        kvd_ref[rs, KVD_WIDTH // 2:] = _dup_heads(v).astype(BF16)

    sec = _lookup(tuple(sec for _, sec, _ in walk), j)
    pl.when(sec == SEC_GATE_NORM)(lambda: run(gate_epi, norm=True))
    pl.when(sec == SEC_GATE)(lambda: run(gate_epi))
    pl.when(sec == SEC_GELU)(lambda: run(gelu_epi))
    pl.when(sec == SEC_SILU)(lambda: run(silu_epi))
    pl.when(sec == SEC_Q)(lambda: run(q_epi))
    pl.when(sec == SEC_KV)(lambda: run(kv_epi, n_tiles=1))


def _in_proj(x2d, g_pre, w_in_b, cos_t, sin_t, table_rows, tiles_per_step, sub, x_parts):
    n = x2d.shape[0]
    tm = IN_PROJ_ROWS
    walk = _walk(tiles_per_step)
    n_steps = len(walk)
    if table_rows == tm:
        n_tab = cos_t.shape[0] // tm
        table_spec = pl.BlockSpec((tm, LANES), lambda i, j: (i % n_tab, 0))
    else:
        table_spec = pl.BlockSpec((table_rows, LANES), lambda i, j: (0, 0))
    w_specs = [pl.BlockSpec((None, D_MODEL, COL_TILE),
                            lambda i, j, t=t: (_lookup(tuple(tiles[t] for tiles, _, _ in walk), j), 0, 0))
               for t in range(tiles_per_step)]
    steps_per_block = Z_BLOCK // (tiles_per_step * COL_TILE)

    def z_index(i, j):
        s = _lookup(tuple(z_step for _, _, z_step in walk), j)
        return (s // steps_per_block, i, s % steps_per_block)

    n_row_tiles = n // tm
    assert sub <= tm // x_parts and x_parts < n_steps

    def x_index(i, j, p):
        tile = jnp.minimum(i + (j >= n_steps - p).astype(jnp.int32), n_row_tiles - 1)
        return (tile * x_parts + p, 0)

    x_specs = [pl.BlockSpec((tm // x_parts, D_MODEL), functools.partial(x_index, p=p))
               for p in range(x_parts)]
    return pl.pallas_call(
        functools.partial(_in_proj_kernel, table_rows=table_rows, sub=sub, walk=walk, x_parts=x_parts),
        grid=(n_row_tiles, n_steps),
        in_specs=[
            *x_specs,
            pl.BlockSpec((1, D_MODEL), lambda i, j: (0, 0)),
            *w_specs,
            table_spec, table_spec,
        ],
        out_specs=[
            pl.BlockSpec((None, tm, tiles_per_step * COL_TILE), z_index),
            pl.BlockSpec((tm, KVD_WIDTH), lambda i, j: (i, 0)),
            pl.BlockSpec((tm, 2 * KV_WIDTH), lambda i, j: (i, 0)),
        ],
        out_shape=[
            jax.ShapeDtypeStruct((Z_COLS // Z_BLOCK, n, Z_BLOCK), BF16),
            jax.ShapeDtypeStruct((n, KVD_WIDTH), BF16),
            jax.ShapeDtypeStruct((n, 2 * KV_WIDTH), F32),
        ],
        scratch_shapes=[pltpu.VMEM((tm, D_MODEL), BF16)],
        compiler_params=pltpu.CompilerParams(
            dimension_semantics=("parallel", "arbitrary"), vmem_limit_bytes=VMEM_LIMIT),
        name="in_proj",
    )(*([x2d] * x_parts), g_pre, *([w_in_b] * tiles_per_step), cos_t, sin_t)


def _layernorm_v(v_ref, lng_ref, lnb_ref):
    vg = v_ref[...].astype(F32)
    mu = jnp.mean(vg, axis=-1, keepdims=True)
    xc = vg - mu
    var = jnp.mean(xc * xc, axis=-1, keepdims=True)
    return xc * lax.rsqrt(var + EPS) * lng_ref[...] + lnb_ref[...]


def _spatial_block(u, v, za, lng, lnb, ws_ref, bs_ref, store_a):
    row = lax.broadcasted_iota(jnp.int32, (CHUNK, CHUNK), 0)
    col = lax.broadcasted_iota(jnp.int32, (CHUNK, CHUNK), 1)
    vg = v.astype(F32)
    mu = jnp.mean(vg, axis=-1, keepdims=True)
    xc = vg - mu
    var = jnp.mean(xc * xc, axis=-1, keepdims=True)
    vn_b = (xc * lax.rsqrt(var + EPS) * lng + lnb).astype(BF16)
    for g in range(A_GROUPS):
        cols = slice(g * LANES, (g + 1) * LANES)
        w = jnp.where(col <= row, ws_ref[g], 0.0).astype(BF16)
        s = jnp.dot(w, vn_b[:, cols], preferred_element_type=F32) + bs_ref[:, cols]
        store_a(cols, ((u[:, cols].astype(F32) * s) * za[:, cols].astype(F32)).astype(BF16))


def _attention_block(q, zb, kvd, kvd_prev, first_block, sinks_ref, store_b):
    t = lax.broadcasted_iota(jnp.int32, (WINDOW, 2 * WINDOW), 0)
    jj = lax.broadcasted_iota(jnp.int32, (WINDOW, 2 * WINDOW), 1)
    mask = (jj > t) & (jj <= t + WINDOW) & ((jj >= WINDOW) | jnp.logical_not(first_block))
    lane = lax.broadcasted_iota(jnp.int32, (WINDOW, LANES), 1)
    low_half = lane < HEAD_DIM
    for m in range(N_HEADS // 2):
        cols = slice(m * LANES, (m + 1) * LANES)
        kcols = slice((m // 2) * LANES, (m // 2 + 1) * LANES)
        vcols = slice(KVD_WIDTH // 2 + (m // 2) * LANES, KVD_WIDTH // 2 + (m // 2 + 1) * LANES)
        k_c = jnp.concatenate([kvd_prev[:, kcols], kvd[:, kcols]], axis=0)
        v_c = jnp.concatenate([kvd_prev[:, vcols], kvd[:, vcols]], axis=0)
        q_m = q[:, cols]
        outs = []
        for half in range(2):
            sink = sinks_ref[2 * m + half]
            q_h = jnp.where(low_half if half == 0 else ~low_half, q_m, jnp.zeros_like(q_m))
            s = lax.dot_general(q_h, k_c, (((1,), (1,)), ((), ())), preferred_element_type=F32)
            s = jnp.where(mask, s, -jnp.inf)
            mx = jnp.maximum(jnp.max(s, axis=-1, keepdims=True), sink)
            p = jnp.exp(s - mx)
            denom = jnp.sum(p, axis=-1, keepdims=True) + jnp.exp(sink - mx)
            o = jnp.dot(p.astype(BF16), v_c, preferred_element_type=F32)
            outs.append(o * (1.0 / denom))
        o_m = jnp.where(low_half, outs[0], outs[1])
        store_b(cols, (o_m * zb[:, cols].astype(F32)).astype(BF16))


def _merge(x, a, b, ga_ref, gb_ref, wa_ref, wb_ref, wo_ref, g_ref):
    gate = lambda ref: jnp.concatenate([ref[c] for c in range(ref.shape[0])], axis=1).astype(F32)
    pa = jnp.dot(a, wa_ref[...], preferred_element_type=F32)
    pb = jnp.dot(b, wb_ref[...], preferred_element_type=F32)
    merged = gate(ga_ref) * pa + gate(gb_ref) * pb
    out = jnp.dot(merged.astype(BF16), wo_ref[...], preferred_element_type=F32)
    ms = jnp.mean(out * out, axis=-1, keepdims=True)
    return x + out * lax.rsqrt(ms + EPS) * g_ref[...]


MIX_ROWS = 256
MIX_BLOCKS = MIX_ROWS // CHUNK


def _prompt_mix_merge_kernel(u_ref, v_ref, za_ref, q_ref, zb_ref, kvd_ref, kvdp_ref, x_ref, ga_ref,
                             gb_ref, lng_ref, lnb_ref, ws_ref, bs_ref, sinks_ref, wa_ref, wb_ref,
                             wo_ref, g_ref, y_ref, a_scr, b_scr):
    i = pl.program_id(0)
    last_tile = pl.num_programs(0) - 2
    tile = jnp.minimum(i, last_tile)
    slot = i % 2

    @pl.when(i == 0)
    def _():
        a_scr[1] = jnp.zeros(a_scr.shape[1:], BF16)
        b_scr[1] = jnp.zeros(b_scr.shape[1:], BF16)

    def attention(blk):
        rows = slice(blk * CHUNK, (blk + 1) * CHUNK)
        first_block = ((tile * MIX_BLOCKS + blk) % (SEQ // WINDOW)) == 0
        kvd_prev = kvdp_ref[...] if blk == 0 else kvd_ref[(blk - 1) * CHUNK:blk * CHUNK, :]

        def store_b(cols, val):
            b_scr[slot, rows, cols] = val

        _attention_block(q_ref[rows, :], zb_ref[rows, :], kvd_ref[rows, :], kvd_prev, first_block,
                         sinks_ref, store_b)

    def spatial(blk):
        rows = slice(blk * CHUNK, (blk + 1) * CHUNK)

        def store_a(cols, val):
            a_scr[slot, rows, cols] = val

        _spatial_block(u_ref[rows, :], v_ref[rows, :], za_ref[rows, :], lng_ref[...], lnb_ref[...],
                       ws_ref, bs_ref, store_a)

    for blk in range(MIX_BLOCKS):
        attention(blk)
    y_ref[...] = _merge(x_ref[...], a_scr[1 - slot], b_scr[1 - slot], ga_ref, gb_ref,
                        wa_ref, wb_ref, wo_ref, g_ref)
    for blk in range(MIX_BLOCKS):
        spatial(blk)


def _prompt_mix_merge(x2d, z, kvd, ln_g, ln_b, w_s, b_s_rows, sinks, w_a, w_b, w_o, g_post):
    n = x2d.shape[0]
    tm = MIX_ROWS
    n_tiles = n // tm
    mix_tile = lambda i: jnp.minimum(i, n_tiles - 1)
    merge_tile = lambda i: jnp.maximum(i - 1, 0)
    wide = lambda cb: pl.BlockSpec((None, tm, Z_BLOCK), lambda i, cb=cb: (cb, mix_tile(i), 0))
    gate = lambda cb: pl.BlockSpec((D_MODEL // Z_BLOCK, tm, Z_BLOCK),
                                   lambda i, cb=cb: (cb // (D_MODEL // Z_BLOCK), merge_tile(i), 0))
    const2 = lambda shape: pl.BlockSpec(shape, lambda i: (0, 0))
    resident = lambda shape: pl.BlockSpec(shape, lambda i: (0, 0), pipeline_mode=pl.Buffered(1))
    return pl.pallas_call(
        _prompt_mix_merge_kernel,
        grid=(n_tiles + 1,),
        in_specs=[
            wide(COL_U), wide(COL_V), wide(COL_ZA), wide(COL_Q), wide(COL_ZB),
            pl.BlockSpec((tm, KVD_WIDTH), lambda i: (mix_tile(i), 0)),
            pl.BlockSpec((CHUNK, KVD_WIDTH), lambda i: (jnp.maximum(mix_tile(i) * MIX_BLOCKS - 1, 0), 0)),
            pl.BlockSpec((tm, D_MODEL), lambda i: (merge_tile(i), 0)),
            gate(COL_GA), gate(COL_GB),
            const2((1, A_WIDTH)), const2((1, A_WIDTH)),
            pl.BlockSpec((A_GROUPS, CHUNK, CHUNK), lambda i: (0, 0, 0)),
            const2((CHUNK, A_WIDTH)),
            pl.BlockSpec(memory_space=pltpu.SMEM),
            resident((A_WIDTH, D_MODEL)),
            resident((B_WIDTH, D_MODEL)),
            resident((D_MODEL, D_MODEL)),
            resident((1, D_MODEL)),
        ],
        out_specs=pl.BlockSpec((tm, D_MODEL), lambda i: (merge_tile(i), 0)),
        out_shape=jax.ShapeDtypeStruct((n, D_MODEL), F32),
        scratch_shapes=[pltpu.VMEM((2, tm, A_WIDTH), BF16), pltpu.VMEM((2, tm, B_WIDTH), BF16)],
        compiler_params=pltpu.CompilerParams(
            dimension_semantics=("arbitrary",), vmem_limit_bytes=VMEM_LIMIT),
        name="prompt_mix_merge",
    )(z, z, z, z, z, kvd, kvd, x2d, z, z, ln_g, ln_b, w_s, b_s_rows, sinks, w_a, w_b, w_o, g_post)


SEQS_PER_STEP = CHUNK // DEC_SEQ


def _sample_mixers_kernel(u_ref, v_ref, za_ref, q_ref, zb_ref, kvd_ref, kv32_ref, ckt_ref, cvt_ref,
                          lng_ref, lnb_ref, coef_ref, bs_ref, sinks_ref,
                          a_ref, b_ref, vn_ref, kwin_ref, vwin_ref):
    ns = SEQS_PER_STEP
    split = lambda x: x.reshape(ns, DEC_SEQ, x.shape[-1])

    vn = _layernorm_v(v_ref, lng_ref, lnb_ref)
    vn_ref[...] = vn
    vn3 = split(vn)
    t_row = lax.broadcasted_iota(jnp.int32, (DEC_SEQ, A_WIDTH), 0)
    s_acc = jnp.broadcast_to(bs_ref[...][None], (ns, DEC_SEQ, A_WIDTH))
    for s in range(DEC_SEQ):
        coef = jnp.where(t_row >= s, coef_ref[s], 0.0)
        s_acc = s_acc + vn3[:, s:s + 1, :] * coef[None]
    a3 = split(u_ref[...].astype(F32)) * s_acc * split(za_ref[...].astype(F32))
    a_ref[...] = a3.reshape(CHUNK, A_WIDTH).astype(BF16)

    rows_q = Q_PER_KV * DEC_SEQ
    lane3 = lax.broadcasted_iota(jnp.int32, (ns, DEC_SEQ, LANES), 2)
    low3 = lane3 < HEAD_DIM
    r_idx = lax.broadcasted_iota(jnp.int32, (ns, rows_q, LANES), 1)
    l_idx = lax.broadcasted_iota(jnp.int32, (ns, rows_q, LANES), 2)
    s_idx = lax.broadcasted_iota(jnp.int32, (ns, rows_q, LANES), 0)
    t_q = r_idx & (DEC_SEQ - 1)
    mask_cache = l_idx > t_q
    mask_new = ((l_idx >> 3) == s_idx) & ((l_idx & (DEC_SEQ - 1)) <= t_q)
    head_of_row = lax.broadcasted_iota(jnp.int32, (1, rows_q, 1), 1) >> 3
    q3 = split(q_ref[...].astype(F32))
    zb3 = split(zb_ref[...].astype(F32))
    for kh in range(N_KV_HEADS):
        pieces = []
        for m in (2 * kh, 2 * kh + 1):
            q_m = q3[:, :, m * LANES:(m + 1) * LANES]
            pieces += [jnp.where(low3, q_m, 0.0), jnp.where(low3, 0.0, q_m)]
        lhs = jnp.concatenate(pieces, axis=1).astype(BF16)
        rows = slice(kh * HEAD_DIM, (kh + 1) * HEAD_DIM)
        kt = ckt_ref[:, rows, :].astype(BF16)
        vt = cvt_ref[:, rows, :].astype(BF16)
        kt2 = jnp.concatenate([kt, kt], axis=1)
        vt2 = jnp.concatenate([vt, vt], axis=1)
        k_new = kvd_ref[:, kh * LANES:(kh + 1) * LANES]
        v_new = kvd_ref[:, KVD_WIDTH // 2 + kh * LANES:KVD_WIDTH // 2 + (kh + 1) * LANES]
        s_c = lax.dot_general(lhs, kt2, (((2,), (1,)), ((0,), (0,))), preferred_element_type=F32)
        s_n = lax.dot_general(lhs.reshape(ns * rows_q, LANES), k_new, (((1,), (1,)), ((), ())),
                              preferred_element_type=F32).reshape(ns, rows_q, LANES)
        s_c = jnp.where(mask_cache, s_c, -jnp.inf)
        s_n = jnp.where(mask_new, s_n, -jnp.inf)
        sink = jnp.zeros((1, rows_q, 1), F32)
        for jq in range(Q_PER_KV):
            sink = jnp.where(head_of_row == jq, sinks_ref[Q_PER_KV * kh + jq], sink)
        mx = jnp.maximum(jnp.maximum(jnp.max(s_c, axis=-1, keepdims=True),
                                     jnp.max(s_n, axis=-1, keepdims=True)), sink)
        p_c = jnp.exp(s_c - mx)
        p_n = jnp.exp(s_n - mx)
        denom = (jnp.sum(p_c, axis=-1, keepdims=True) + jnp.sum(p_n, axis=-1, keepdims=True)
                 + jnp.exp(sink - mx))
        o = lax.dot_general(p_c.astype(BF16), vt2, (((2,), (2,)), ((0,), (0,))),
                            preferred_element_type=F32)
        o = o + jnp.dot(p_n.reshape(ns * rows_q, LANES).astype(BF16), v_new,
                        preferred_element_type=F32).reshape(ns, rows_q, LANES)
        o = o * (1.0 / denom)
        for i, m in enumerate((2 * kh, 2 * kh + 1)):
            base = 2 * DEC_SEQ * i
            o_m = jnp.where(low3, o[:, base:base + DEC_SEQ, :], o[:, base + DEC_SEQ:base + 2 * DEC_SEQ, :])
            cols = slice(m * LANES, (m + 1) * LANES)
            b_ref[:, cols] = (o_m * zb3[:, :, cols]).reshape(CHUNK, LANES).astype(BF16)

    keep = WINDOW - DEC_SEQ
    lane_w = lax.broadcasted_iota(jnp.int32, (KV_WIDTH, WINDOW), 1)
    for new_rows, cache_ref, out_ref in ((kv32_ref[:, :KV_WIDTH], ckt_ref, kwin_ref),
                                         (kv32_ref[:, KV_WIDTH:], cvt_ref, vwin_ref)):
        new_t = new_rows.T
        old = pltpu.roll(cache_ref[...].reshape(ns * KV_WIDTH, WINDOW), keep, axis=1)
        for s in range(ns):
            shifted_new = pltpu.roll(new_t, (keep - DEC_SEQ * s) % WINDOW, axis=1)
            out_ref[s] = jnp.where(lane_w < keep, old[s * KV_WIDTH:(s + 1) * KV_WIDTH], shifted_new)


def _sample_mixers(z, kvd, kv32, cache_kt, cache_vt, ln_g, ln_b, coef, b_s8, sinks):
    n = kvd.shape[0]
    n_seq = cache_kt.shape[0]
    wide = lambda cb: pl.BlockSpec((None, CHUNK, Z_BLOCK), lambda r, cb=cb: (cb, r, 0))
    const2 = lambda shape: pl.BlockSpec(shape, lambda r: (0, 0))
    win = pl.BlockSpec((SEQS_PER_STEP, KV_WIDTH, WINDOW), lambda r: (r, 0, 0))
    return pl.pallas_call(
        _sample_mixers_kernel,
        grid=(n // CHUNK,),
        in_specs=[
            wide(COL_U), wide(COL_V), wide(COL_ZA), wide(COL_Q), wide(COL_ZB),
            pl.BlockSpec((CHUNK, KVD_WIDTH), lambda r: (r, 0)),
            pl.BlockSpec((CHUNK, 2 * KV_WIDTH), lambda r: (r, 0)),
            win, win,
            const2((1, A_WIDTH)), const2((1, A_WIDTH)),
            pl.BlockSpec((DEC_SEQ, DEC_SEQ, A_WIDTH), lambda r: (0, 0, 0)),
            const2((DEC_SEQ, A_WIDTH)),
            pl.BlockSpec(memory_space=pltpu.SMEM),
        ],
        out_specs=[
            pl.BlockSpec((CHUNK, A_WIDTH), lambda r: (r, 0)),
            pl.BlockSpec((CHUNK, B_WIDTH), lambda r: (r, 0)),
            pl.BlockSpec((CHUNK, A_WIDTH), lambda r: (r, 0)),
            win, win,
        ],
        out_shape=[
            jax.ShapeDtypeStruct((n, A_WIDTH), BF16),
            jax.ShapeDtypeStruct((n, B_WIDTH), BF16),
            jax.ShapeDtypeStruct((n, A_WIDTH), F32),
            jax.ShapeDtypeStruct((n_seq, KV_WIDTH, WINDOW), F32),
            jax.ShapeDtypeStruct((n_seq, KV_WIDTH, WINDOW), F32),
        ],
        compiler_params=pltpu.CompilerParams(
            dimension_semantics=("parallel",), vmem_limit_bytes=VMEM_LIMIT),
        name="sample_mixers",
    )(z, z, z, z, z, kvd, kv32, cache_kt, cache_vt, ln_g, ln_b, coef, b_s8, sinks)


def _merge_out_kernel(x_ref, a_ref, b_ref, ga_ref, gb_ref, wa_ref, wb_ref, wo_ref, g_ref, y_ref):
    y_ref[...] = _merge(x_ref[...], a_ref[...], b_ref[...], ga_ref, gb_ref,
                        wa_ref, wb_ref, wo_ref, g_ref)


def _merge_out(x2d, a, b, z, w_a, w_b, w_o, g_post):
    n = x2d.shape[0]
    tm = MERGE_ROWS
    resident = lambda shape: pl.BlockSpec(shape, lambda i: (0, 0), pipeline_mode=pl.Buffered(1))
    return pl.pallas_call(
        _merge_out_kernel,
        grid=(n // tm,),
        in_specs=[
            pl.BlockSpec((tm, D_MODEL), lambda i: (i, 0)),
            pl.BlockSpec((tm, A_WIDTH), lambda i: (i, 0)),
            pl.BlockSpec((tm, B_WIDTH), lambda i: (i, 0)),
            pl.BlockSpec((D_MODEL // Z_BLOCK, tm, Z_BLOCK), lambda i: (COL_GA // (D_MODEL // Z_BLOCK), i, 0)),
            pl.BlockSpec((D_MODEL // Z_BLOCK, tm, Z_BLOCK), lambda i: (COL_GB // (D_MODEL // Z_BLOCK), i, 0)),
            resident((A_WIDTH, D_MODEL)),
            resident((B_WIDTH, D_MODEL)),
            resident((D_MODEL, D_MODEL)),
            resident((1, D_MODEL)),
        ],
        out_specs=pl.BlockSpec((tm, D_MODEL), lambda i: (i, 0)),
        out_shape=jax.ShapeDtypeStruct((n, D_MODEL), F32),
        compiler_params=pltpu.CompilerParams(
            dimension_semantics=("parallel",), vmem_limit_bytes=VMEM_LIMIT),
        name="merge_out",
    )(x2d, a, b, z, z, w_a, w_b, w_o, g_post)


def _cast_tile_kernel(w_ref, o_ref):
    o_ref[...] = w_ref[...].astype(BF16)


def _cast_column_tiles(w):
    rows, cols = w.shape
    return pl.pallas_call(
        _cast_tile_kernel,
        grid=(cols // COL_TILE,),
        in_specs=[pl.BlockSpec((rows, COL_TILE), lambda t: (0, t))],
        out_specs=pl.BlockSpec((None, rows, COL_TILE), lambda t: (t, 0, 0)),
        out_shape=jax.ShapeDtypeStruct((cols // COL_TILE, rows, COL_TILE), BF16),
        compiler_params=pltpu.CompilerParams(
            dimension_semantics=("parallel",), vmem_limit_bytes=VMEM_LIMIT),
        name="cast_column_tiles",
    )(w)


def _rope_tables(pos):
    lane = jnp.arange(LANES)
    inv = ROPE_THETA ** (-(2 * (lane % (HEAD_DIM // 2))).astype(F32) / HEAD_DIM)
    ang = pos.astype(F32)[:, None] * inv[None, :]
    sign = jnp.where((lane % HEAD_DIM) < HEAD_DIM // 2, -1.0, 1.0).astype(F32)
    return jnp.cos(ang), jnp.sin(ang) * sign[None, :]


def _window_first(win):
    n = win.shape[1]
    return jnp.transpose(win[0], (0, 2, 3, 1)).reshape(n, KV_WIDTH, WINDOW)


def _window_last(win_t):
    n = win_t.shape[0]
    return jnp.transpose(win_t.reshape(n, N_KV_HEADS, HEAD_DIM, WINDOW), (0, 3, 1, 2))[None]


def kernel(x_prompt, x_sample, cache_k_win, cache_v_win, g_pre, w_in, ln_v_g, ln_v_b, w_spatial,
           b_spatial, sinks, w_proj_a, w_proj_b, w_out, g_post):
    bsz, seq, _ = x_prompt.shape
    dbsz, dseq, _ = x_sample.shape
    assert seq == SEQ and dseq == DEC_SEQ and seq % IN_PROJ_ROWS == 0
    assert (dbsz * dseq) % IN_PROJ_ROWS == 0 and g_pre.shape[0] == 1

    w_in_b = _cast_column_tiles(w_in[0])
    w_a = w_proj_a[0].astype(BF16)
    w_b = w_proj_b[0].astype(BF16)
    w_o = w_out[0].astype(BF16)
    b_s_rows = jnp.repeat(b_spatial[0].T, LANES, axis=1)
    coef = jnp.repeat(jnp.transpose(w_spatial[0][:, :DEC_SEQ, :DEC_SEQ], (2, 1, 0)), LANES, axis=2)
    cos_p, sin_p = _rope_tables(jnp.arange(SEQ))
    cos_s, sin_s = _rope_tables(PAST_LEN + jnp.arange(dseq))

    xp = x_prompt.reshape(bsz * seq, D_MODEL)
    z_p, kvd_p, kv32_p = _in_proj(xp, g_pre, w_in_b, cos_p, sin_p, IN_PROJ_ROWS, 2, 256, 4)
    y_p = _prompt_mix_merge(xp, z_p, kvd_p, ln_v_g, ln_v_b, w_spatial[0], b_s_rows, sinks[0],
                            w_a, w_b, w_o, g_post)
    kv_win_p = kv32_p.reshape(bsz, seq, 2 * KV_WIDTH)[:, seq - WINDOW:]
    k_win_p = kv_win_p[..., :KV_WIDTH].reshape(1, bsz, WINDOW, N_KV_HEADS, HEAD_DIM)
    v_win_p = kv_win_p[..., KV_WIDTH:].reshape(1, bsz, WINDOW, N_KV_HEADS, HEAD_DIM)

    xs = x_sample.reshape(dbsz * dseq, D_MODEL)
    z_s, kvd_s, kv32_s = _in_proj(xs, g_pre, w_in_b, cos_s, sin_s, dseq, 2, 1024, 1)
    a_s, b_s, vn_s, k_win_t, v_win_t = _sample_mixers(
        z_s, kvd_s, kv32_s, _window_first(cache_k_win), _window_first(cache_v_win),
        ln_v_g, ln_v_b, coef, b_s_rows[:dseq], sinks[0])
    y_s = _merge_out(xs, a_s, b_s, z_s, w_a, w_b, w_o, g_post)

    return (y_p.reshape(bsz, seq, D_MODEL),
            y_s.reshape(dbsz, dseq, D_MODEL),
            k_win_p, v_win_p,
            _window_last(k_win_t), _window_last(v_win_t),
            vn_s.reshape(1, dbsz, dseq, A_WIDTH))
```

```python
import functools

import numpy as np
import jax
import jax.numpy as jnp
from jax import lax
from jax.experimental import pallas as pl
from jax.experimental.pallas import tpu as pltpu

D_MODEL = 2048
SEQ = 2048
DEC_SEQ = 8
PAST_LEN = 8192
CHUNK = 128
A_WIDTH = 1024
A_GROUPS = 8
HEAD_DIM = 64
N_HEADS = 16
N_KV_HEADS = 4
Q_PER_KV = N_HEADS // N_KV_HEADS
B_WIDTH = N_HEADS * HEAD_DIM
KV_WIDTH = N_KV_HEADS * HEAD_DIM
WINDOW = 128
ROPE_THETA = 10000.0
EPS = 1e-6
IN_COLS = 3 * A_WIDTH + 2 * B_WIDTH + 2 * KV_WIDTH + 2 * D_MODEL

LANES = 128
SUBLANES = 8
BF16 = jnp.bfloat16
F32 = jnp.float32

COL_TILE = 512
N_COL_TILES = IN_COLS // COL_TILE
Z_BLOCK = 2 * COL_TILE
Z_COLS = (N_COL_TILES - 1) * COL_TILE
COL_GA, COL_GB, COL_U, COL_V, COL_ZA, COL_Q, COL_ZB = 0, 2, 4, 5, 6, 7, 8
KVD_WIDTH = 2 * N_KV_HEADS * LANES
SEC_GATE_NORM, SEC_GATE, SEC_GELU, SEC_SILU, SEC_Q, SEC_KV = range(6)


def _walk(tiles_per_step):
    sections = [((11, 12, 13, 14, 15, 16, 17, 18), SEC_GATE), ((0, 1, 2, 3), SEC_GELU),
                ((4, 5), SEC_SILU), ((6, 7), SEC_Q), ((9, 10), SEC_SILU)]
    steps = [(tiles[k:k + tiles_per_step], sec)
             for tiles, sec in sections for k in range(0, len(tiles), tiles_per_step)]
    steps = [(tiles, sec, s) for s, (tiles, sec) in enumerate(steps)]
    steps[0] = (steps[0][0], SEC_GATE_NORM, 0)
    mid = len(steps) // 2
    steps.insert(mid, ((8,) * tiles_per_step, SEC_KV, steps[mid - 1][2]))
    return steps


def _lookup(values, j):
    out = jnp.int32(values[-1])
    for k in range(len(values) - 2, -1, -1):
        out = jnp.where(j == k, jnp.int32(values[k]), out)
    return out


IN_PROJ_ROWS = 1024
MERGE_ROWS = 512
VMEM_LIMIT = 56 * 1024 * 1024


def _gelu(x):
    return 0.5 * x * (1.0 + lax.erf(x * np.float32(1.0 / np.sqrt(2.0))))


def _sigmoid(x):
    return 1.0 / (1.0 + jnp.exp(-x))


def _rope(x, cos, sin_signed):
    width = x.shape[1]
    lane = lax.broadcasted_iota(jnp.int32, x.shape, 1)
    first_half = (lane & (HEAD_DIM - 1)) < (HEAD_DIM // 2)
    partner = jnp.where(first_half,
                        pltpu.roll(x, width - HEAD_DIM // 2, axis=1),
                        pltpu.roll(x, HEAD_DIM // 2, axis=1))
    reps = width // LANES
    cos_w = jnp.concatenate([cos] * reps, axis=1) if reps > 1 else cos
    sin_w = jnp.concatenate([sin_signed] * reps, axis=1) if reps > 1 else sin_signed
    return x * cos_w + partner * sin_w


def _dup_heads(x):
    lane = lax.broadcasted_iota(jnp.int32, (x.shape[0], LANES), 1)
    low = lane < HEAD_DIM
    out = []
    for c in range(x.shape[1] // LANES):
        xc = x[:, c * LANES:(c + 1) * LANES]
        sw = pltpu.roll(xc, HEAD_DIM, axis=1)
        out += [jnp.where(low, xc, sw), jnp.where(low, sw, xc)]
    return jnp.concatenate(out, axis=1)


def _in_proj_kernel(*refs, table_rows, sub, walk, x_parts, n_cast):
    n_w = len(walk[0][0])
    x_refs = refs[:x_parts]
    g_ref = refs[x_parts]
    w_refs = refs[x_parts + 1:x_parts + 1 + n_w]
    cos_ref, sin_ref = refs[x_parts + 1 + n_w:x_parts + 3 + n_w]
    cast_in = refs[x_parts + 3 + n_w:x_parts + 3 + n_w + n_cast]
    z_ref, kvd_ref, kv32_ref = refs[x_parts + 3 + n_w + n_cast:x_parts + 6 + n_w + n_cast]
    cast_out = refs[x_parts + 6 + n_w + n_cast:x_parts + 6 + n_w + 2 * n_cast]
    h_scr = refs[-1]
    j = pl.program_id(1)
    part_rows = x_refs[0].shape[0]
    tm = part_rows * x_parts

    def tables(rs):
        if table_rows == tm:
            return cos_ref[rs, :], sin_ref[rs, :]
        tile = lambda t: jnp.broadcast_to(t[None], (sub // table_rows, table_rows, LANES)).reshape(sub, LANES)
        return tile(cos_ref[...]), tile(sin_ref[...])

    def run(epilogue, norm=False, n_tiles=n_w):
        for r in range(tm // sub):
            rs = slice(r * sub, (r + 1) * sub)
            if norm:
                start = (r * sub) % part_rows
                x = x_refs[(r * sub) // part_rows][start:start + sub, :]
                ms = jnp.mean(x * x, axis=-1, keepdims=True)
                h = (x * lax.rsqrt(ms + EPS) * g_ref[...]).astype(BF16)
                h_scr[rs, :] = h
            else:
                h = h_scr[rs, :]
            for t in range(n_tiles):
                acc = jnp.dot(h, w_refs[t][...], preferred_element_type=F32)
                epilogue(acc, rs, slice(t * COL_TILE, (t + 1) * COL_TILE))

    def gate_epi(acc, rs, cs):
        z_ref[rs, cs] = _sigmoid(acc).astype(BF16)

    def gelu_epi(acc, rs, cs):
        z_ref[rs, cs] = _gelu(acc).astype(BF16)

    def silu_epi(acc, rs, cs):
        z_ref[rs, cs] = (acc * _sigmoid(acc)).astype(BF16)

    def q_epi(acc, rs, cs):
        cos, sin = tables(rs)
        z_ref[rs, cs] = (_rope(acc, cos, sin) * np.float32(HEAD_DIM ** -0.5)).astype(BF16)

    def kv_epi(acc, rs, cs):
        cos, sin = tables(rs)
        k = _rope(acc[:, :KV_WIDTH], cos, sin)
        v = acc[:, KV_WIDTH:]
        kv32_ref[rs, :KV_WIDTH] = k
        kv32_ref[rs, KV_WIDTH:] = v
        kvd_ref[rs, :KVD_WIDTH // 2] = _dup_heads(k).astype(BF16)
        kvd_ref[rs, KVD_WIDTH // 2:] = _dup_heads(v).astype(BF16)

    sec = _lookup(tuple(sec for _, sec, _ in walk), j)
    def first_step():
        run(gate_epi, norm=True)
        for src, dst in zip(cast_in, cast_out):
            dst[...] = src[...].astype(BF16)

    pl.when(sec == SEC_GATE_NORM)(first_step)
    pl.when(sec == SEC_GATE)(lambda: run(gate_epi))
    pl.when(sec == SEC_GELU)(lambda: run(gelu_epi))
    pl.when(sec == SEC_SILU)(lambda: run(silu_epi))
    pl.when(sec == SEC_Q)(lambda: run(q_epi))
    pl.when(sec == SEC_KV)(lambda: run(kv_epi, n_tiles=1))


def _in_proj(x2d, g_pre, w_in_b, cos_t, sin_t, table_rows, tiles_per_step, sub, x_parts, cast_along=()):
    n = x2d.shape[0]
    tm = IN_PROJ_ROWS
    walk = _walk(tiles_per_step)
    n_steps = len(walk)
    if table_rows == tm:
        n_tab = cos_t.shape[0] // tm
        table_spec = pl.BlockSpec((tm, LANES), lambda i, j: (i % n_tab, 0))
    else:
        table_spec = pl.BlockSpec((table_rows, LANES), lambda i, j: (0, 0))
    w_specs = [pl.BlockSpec((None, D_MODEL, COL_TILE),
                            lambda i, j, t=t: (_lookup(tuple(tiles[t] for tiles, _, _ in walk), j), 0, 0))
               for t in range(tiles_per_step)]
    steps_per_block = Z_BLOCK // (tiles_per_step * COL_TILE)

    def z_index(i, j):
        s = _lookup(tuple(z_step for _, _, z_step in walk), j)
        return (s // steps_per_block, i, s % steps_per_block)

    n_row_tiles = n // tm
    assert sub <= tm // x_parts and x_parts < n_steps

    def x_index(i, j, p):
        tile = jnp.minimum(i + (j >= n_steps - p).astype(jnp.int32), n_row_tiles - 1)
        return (tile * x_parts + p, 0)

    x_specs = [pl.BlockSpec((tm // x_parts, D_MODEL), functools.partial(x_index, p=p))
               for p in range(x_parts)]
    cast_specs = [pl.BlockSpec((w.shape[0] // n_row_tiles, w.shape[1]), lambda i, j: (i, 0)) for w in cast_along]
    outs = pl.pallas_call(
        functools.partial(_in_proj_kernel, table_rows=table_rows, sub=sub, walk=walk, x_parts=x_parts,
                          n_cast=len(cast_along)),
        grid=(n_row_tiles, n_steps),
        in_specs=[
            *x_specs,
            pl.BlockSpec((1, D_MODEL), lambda i, j: (0, 0)),
            *w_specs,
            table_spec, table_spec,
            *cast_specs,
        ],
        out_specs=[
            pl.BlockSpec((None, tm, tiles_per_step * COL_TILE), z_index),
            pl.BlockSpec((tm, KVD_WIDTH), lambda i, j: (i, 0)),
            pl.BlockSpec((tm, 2 * KV_WIDTH), lambda i, j: (i, 0)),
            *cast_specs,
        ],
        out_shape=[
            jax.ShapeDtypeStruct((Z_COLS // Z_BLOCK, n, Z_BLOCK), BF16),
            jax.ShapeDtypeStruct((n, KVD_WIDTH), BF16),
            jax.ShapeDtypeStruct((n, 2 * KV_WIDTH), F32),
            *[jax.ShapeDtypeStruct(w.shape, BF16) for w in cast_along],
        ],
        scratch_shapes=[pltpu.VMEM((tm, D_MODEL), BF16)],
        compiler_params=pltpu.CompilerParams(
            dimension_semantics=("parallel", "arbitrary"), vmem_limit_bytes=VMEM_LIMIT),
        name="in_proj",
    )(*([x2d] * x_parts), g_pre, *([w_in_b] * tiles_per_step), cos_t, sin_t, *cast_along)
    return outs[:3], outs[3:]


def _layernorm_v(v_ref, lng_ref, lnb_ref):
    vg = v_ref[...].astype(F32)
    mu = jnp.mean(vg, axis=-1, keepdims=True)
    xc = vg - mu
    var = jnp.mean(xc * xc, axis=-1, keepdims=True)
    return xc * lax.rsqrt(var + EPS) * lng_ref[...] + lnb_ref[...]


def _spatial_block(u, v, za, lng, lnb, ws_ref, bs_ref, store_a):
    row = lax.broadcasted_iota(jnp.int32, (CHUNK, CHUNK), 0)
    col = lax.broadcasted_iota(jnp.int32, (CHUNK, CHUNK), 1)
    vg = v.astype(F32)
    mu = jnp.mean(vg, axis=-1, keepdims=True)
    xc = vg - mu
    var = jnp.mean(xc * xc, axis=-1, keepdims=True)
    vn_b = (xc * lax.rsqrt(var + EPS) * lng + lnb).astype(BF16)
    for g in range(A_GROUPS):
        cols = slice(g * LANES, (g + 1) * LANES)
        w = jnp.where(col <= row, ws_ref[g], 0.0).astype(BF16)
        s = jnp.dot(w, vn_b[:, cols], preferred_element_type=F32) + bs_ref[:, cols]
        store_a(cols, ((u[:, cols].astype(F32) * s) * za[:, cols].astype(F32)).astype(BF16))


def _attention_block(q, zb, kvd, kvd_prev, first_block, sinks_ref, store_b):
    t = lax.broadcasted_iota(jnp.int32, (WINDOW, 2 * WINDOW), 0)
    jj = lax.broadcasted_iota(jnp.int32, (WINDOW, 2 * WINDOW), 1)
    mask = (jj > t) & (jj <= t + WINDOW) & ((jj >= WINDOW) | jnp.logical_not(first_block))
    lane = lax.broadcasted_iota(jnp.int32, (WINDOW, LANES), 1)
    low_half = lane < HEAD_DIM
    for m in range(N_HEADS // 2):
        cols = slice(m * LANES, (m + 1) * LANES)
        kcols = slice((m // 2) * LANES, (m // 2 + 1) * LANES)
        vcols = slice(KVD_WIDTH // 2 + (m // 2) * LANES, KVD_WIDTH // 2 + (m // 2 + 1) * LANES)
        k_c = jnp.concatenate([kvd_prev[:, kcols], kvd[:, kcols]], axis=0)
        v_c = jnp.concatenate([kvd_prev[:, vcols], kvd[:, vcols]], axis=0)
        q_m = q[:, cols]
        outs = []
        for half in range(2):
            sink = sinks_ref[2 * m + half]
            q_h = jnp.where(low_half if half == 0 else ~low_half, q_m, jnp.zeros_like(q_m))
            s = lax.dot_general(q_h, k_c, (((1,), (1,)), ((), ())), preferred_element_type=F32)
            s = jnp.where(mask, s, -jnp.inf)
            mx = jnp.maximum(jnp.max(s, axis=-1, keepdims=True), sink)
            p = jnp.exp(s - mx)
            denom = jnp.sum(p, axis=-1, keepdims=True) + jnp.exp(sink - mx)
            o = jnp.dot(p.astype(BF16), v_c, preferred_element_type=F32)
            outs.append(o * (1.0 / denom))
        o_m = jnp.where(low_half, outs[0], outs[1])
        store_b(cols, (o_m * zb[:, cols].astype(F32)).astype(BF16))


def _merge(x, a, b, ga_ref, gb_ref, wa_ref, wb_ref, wo_ref, g_ref):
    gate = lambda ref: jnp.concatenate([ref[c] for c in range(ref.shape[0])], axis=1).astype(F32)
    pa = jnp.dot(a, wa_ref[...], preferred_element_type=F32)
    pb = jnp.dot(b, wb_ref[...], preferred_element_type=F32)
    merged = gate(ga_ref) * pa + gate(gb_ref) * pb
    out = jnp.dot(merged.astype(BF16), wo_ref[...], preferred_element_type=F32)
    ms = jnp.mean(out * out, axis=-1, keepdims=True)
    return x + out * lax.rsqrt(ms + EPS) * g_ref[...]


MIX_ROWS = 256
MIX_BLOCKS = MIX_ROWS // CHUNK


def _prompt_mix_merge_kernel(u_ref, v_ref, za_ref, q_ref, zb_ref, kvd_ref, kvdp_ref, x_ref, ga_ref,
                             gb_ref, lng_ref, lnb_ref, ws_ref, bs_ref, sinks_ref, wa_ref, wb_ref,
                             wo_ref, g_ref, y_ref, a_scr, b_scr):
    i = pl.program_id(0)
    last_tile = pl.num_programs(0) - 2
    tile = jnp.minimum(i, last_tile)
    slot = i % 2

    @pl.when(i == 0)
    def _():
        a_scr[1] = jnp.zeros(a_scr.shape[1:], BF16)
        b_scr[1] = jnp.zeros(b_scr.shape[1:], BF16)

    def attention(blk):
        rows = slice(blk * CHUNK, (blk + 1) * CHUNK)
        first_block = ((tile * MIX_BLOCKS + blk) % (SEQ // WINDOW)) == 0
        kvd_prev = kvdp_ref[...] if blk == 0 else kvd_ref[(blk - 1) * CHUNK:blk * CHUNK, :]

        def store_b(cols, val):
            b_scr[slot, rows, cols] = val

        _attention_block(q_ref[rows, :], zb_ref[rows, :], kvd_ref[rows, :], kvd_prev, first_block,
                         sinks_ref, store_b)

    def spatial(blk):
        rows = slice(blk * CHUNK, (blk + 1) * CHUNK)

        def store_a(cols, val):
            a_scr[slot, rows, cols] = val

        _spatial_block(u_ref[rows, :], v_ref[rows, :], za_ref[rows, :], lng_ref[...], lnb_ref[...],
                       ws_ref, bs_ref, store_a)

    for blk in range(MIX_BLOCKS):
        attention(blk)
    y_ref[...] = _merge(x_ref[...], a_scr[1 - slot], b_scr[1 - slot], ga_ref, gb_ref,
                        wa_ref, wb_ref, wo_ref, g_ref)
    for blk in range(MIX_BLOCKS):
        spatial(blk)


def _prompt_mix_merge(x2d, z, kvd, ln_g, ln_b, w_s, b_s_rows, sinks, w_a, w_b, w_o, g_post):
    n = x2d.shape[0]
    tm = MIX_ROWS
    n_tiles = n // tm
    mix_tile = lambda i: jnp.minimum(i, n_tiles - 1)
    merge_tile = lambda i: jnp.maximum(i - 1, 0)
    wide = lambda cb: pl.BlockSpec((None, tm, Z_BLOCK), lambda i, cb=cb: (cb, mix_tile(i), 0))
    gate = lambda cb: pl.BlockSpec((D_MODEL // Z_BLOCK, tm, Z_BLOCK),
                                   lambda i, cb=cb: (cb // (D_MODEL // Z_BLOCK), merge_tile(i), 0))
    const2 = lambda shape: pl.BlockSpec(shape, lambda i: (0, 0))
    resident = lambda shape: pl.BlockSpec(shape, lambda i: (0, 0), pipeline_mode=pl.Buffered(1))
    return pl.pallas_call(
        _prompt_mix_merge_kernel,
        grid=(n_tiles + 1,),
        in_specs=[
            wide(COL_U), wide(COL_V), wide(COL_ZA), wide(COL_Q), wide(COL_ZB),
            pl.BlockSpec((tm, KVD_WIDTH), lambda i: (mix_tile(i), 0)),
            pl.BlockSpec((CHUNK, KVD_WIDTH), lambda i: (jnp.maximum(mix_tile(i) * MIX_BLOCKS - 1, 0), 0)),
            pl.BlockSpec((tm, D_MODEL), lambda i: (merge_tile(i), 0)),
            gate(COL_GA), gate(COL_GB),
            const2((1, A_WIDTH)), const2((1, A_WIDTH)),
            pl.BlockSpec((A_GROUPS, CHUNK, CHUNK), lambda i: (0, 0, 0)),
            const2((CHUNK, A_WIDTH)),
            pl.BlockSpec(memory_space=pltpu.SMEM),
            resident((A_WIDTH, D_MODEL)),
            resident((B_WIDTH, D_MODEL)),
            resident((D_MODEL, D_MODEL)),
            resident((1, D_MODEL)),
        ],
        out_specs=pl.BlockSpec((tm, D_MODEL), lambda i: (merge_tile(i), 0)),
        out_shape=jax.ShapeDtypeStruct((n, D_MODEL), F32),
        scratch_shapes=[pltpu.VMEM((2, tm, A_WIDTH), BF16), pltpu.VMEM((2, tm, B_WIDTH), BF16)],
        compiler_params=pltpu.CompilerParams(
            dimension_semantics=("arbitrary",), vmem_limit_bytes=VMEM_LIMIT),
        name="prompt_mix_merge",
    )(z, z, z, z, z, kvd, kvd, x2d, z, z, ln_g, ln_b, w_s, b_s_rows, sinks, w_a, w_b, w_o, g_post)


SEQS_PER_STEP = CHUNK // DEC_SEQ


def _sample_mixers_kernel(u_ref, v_ref, za_ref, q_ref, zb_ref, kvd_ref, kv32_ref, ckt_ref, cvt_ref,
                          lng_ref, lnb_ref, coef_ref, bs_ref, sinks_ref,
                          a_ref, b_ref, vn_ref, kwin_ref, vwin_ref):
    ns = SEQS_PER_STEP
    split = lambda x: x.reshape(ns, DEC_SEQ, x.shape[-1])

    vn = _layernorm_v(v_ref, lng_ref, lnb_ref)
    vn_ref[...] = vn
    vn3 = split(vn)
    t_row = lax.broadcasted_iota(jnp.int32, (DEC_SEQ, A_WIDTH), 0)
    s_acc = jnp.broadcast_to(bs_ref[...][None], (ns, DEC_SEQ, A_WIDTH))
    for s in range(DEC_SEQ):
        coef = jnp.where(t_row >= s, coef_ref[s], 0.0)
        s_acc = s_acc + vn3[:, s:s + 1, :] * coef[None]
    a3 = split(u_ref[...].astype(F32)) * s_acc * split(za_ref[...].astype(F32))
    a_ref[...] = a3.reshape(CHUNK, A_WIDTH).astype(BF16)

    rows_q = Q_PER_KV * DEC_SEQ
    lane3 = lax.broadcasted_iota(jnp.int32, (ns, DEC_SEQ, LANES), 2)
    low3 = lane3 < HEAD_DIM
    r_idx = lax.broadcasted_iota(jnp.int32, (ns, rows_q, LANES), 1)
    l_idx = lax.broadcasted_iota(jnp.int32, (ns, rows_q, LANES), 2)
    s_idx = lax.broadcasted_iota(jnp.int32, (ns, rows_q, LANES), 0)
    t_q = r_idx & (DEC_SEQ - 1)
    mask_cache = l_idx > t_q
    mask_new = ((l_idx >> 3) == s_idx) & ((l_idx & (DEC_SEQ - 1)) <= t_q)
    head_of_row = lax.broadcasted_iota(jnp.int32, (1, rows_q, 1), 1) >> 3
    q3 = split(q_ref[...].astype(F32))
    zb3 = split(zb_ref[...].astype(F32))
    for kh in range(N_KV_HEADS):
        pieces = []
        for m in (2 * kh, 2 * kh + 1):
            q_m = q3[:, :, m * LANES:(m + 1) * LANES]
            pieces += [jnp.where(low3, q_m, 0.0), jnp.where(low3, 0.0, q_m)]
        lhs = jnp.concatenate(pieces, axis=1).astype(BF16)
        rows = slice(kh * HEAD_DIM, (kh + 1) * HEAD_DIM)
        kt = ckt_ref[:, rows, :].astype(BF16)
        vt = cvt_ref[:, rows, :].astype(BF16)
        kt2 = jnp.concatenate([kt, kt], axis=1)
        vt2 = jnp.concatenate([vt, vt], axis=1)
        k_new = kvd_ref[:, kh * LANES:(kh + 1) * LANES]
        v_new = kvd_ref[:, KVD_WIDTH // 2 + kh * LANES:KVD_WIDTH // 2 + (kh + 1) * LANES]
        s_c = lax.dot_general(lhs, kt2, (((2,), (1,)), ((0,), (0,))), preferred_element_type=F32)
        s_n = lax.dot_general(lhs.reshape(ns * rows_q, LANES), k_new, (((1,), (1,)), ((), ())),
                              preferred_element_type=F32).reshape(ns, rows_q, LANES)
        s_c = jnp.where(mask_cache, s_c, -jnp.inf)
        s_n = jnp.where(mask_new, s_n, -jnp.inf)
        sink = jnp.zeros((1, rows_q, 1), F32)
        for jq in range(Q_PER_KV):
            sink = jnp.where(head_of_row == jq, sinks_ref[Q_PER_KV * kh + jq], sink)
        mx = jnp.maximum(jnp.maximum(jnp.max(s_c, axis=-1, keepdims=True),
                                     jnp.max(s_n, axis=-1, keepdims=True)), sink)
        p_c = jnp.exp(s_c - mx)
        p_n = jnp.exp(s_n - mx)
        denom = (jnp.sum(p_c, axis=-1, keepdims=True) + jnp.sum(p_n, axis=-1, keepdims=True)
                 + jnp.exp(sink - mx))
        o = lax.dot_general(p_c.astype(BF16), vt2, (((2,), (2,)), ((0,), (0,))),
                            preferred_element_type=F32)
        o = o + jnp.dot(p_n.reshape(ns * rows_q, LANES).astype(BF16), v_new,
                        preferred_element_type=F32).reshape(ns, rows_q, LANES)
        o = o * (1.0 / denom)
        for i, m in enumerate((2 * kh, 2 * kh + 1)):
            base = 2 * DEC_SEQ * i
            o_m = jnp.where(low3, o[:, base:base + DEC_SEQ, :], o[:, base + DEC_SEQ:base + 2 * DEC_SEQ, :])
            cols = slice(m * LANES, (m + 1) * LANES)
            b_ref[:, cols] = (o_m * zb3[:, :, cols]).reshape(CHUNK, LANES).astype(BF16)

    keep = WINDOW - DEC_SEQ
    lane_w = lax.broadcasted_iota(jnp.int32, (KV_WIDTH, WINDOW), 1)
    for new_rows, cache_ref, out_ref in ((kv32_ref[:, :KV_WIDTH], ckt_ref, kwin_ref),
                                         (kv32_ref[:, KV_WIDTH:], cvt_ref, vwin_ref)):
        new_t = new_rows.T
        old = pltpu.roll(cache_ref[...].reshape(ns * KV_WIDTH, WINDOW), keep, axis=1)
        for s in range(ns):
            shifted_new = pltpu.roll(new_t, (keep - DEC_SEQ * s) % WINDOW, axis=1)
            out_ref[s] = jnp.where(lane_w < keep, old[s * KV_WIDTH:(s + 1) * KV_WIDTH], shifted_new)


def _sample_mixers(z, kvd, kv32, cache_kt, cache_vt, ln_g, ln_b, coef, b_s8, sinks):
    n = kvd.shape[0]
    n_seq = cache_kt.shape[0]
    wide = lambda cb: pl.BlockSpec((None, CHUNK, Z_BLOCK), lambda r, cb=cb: (cb, r, 0))
    const2 = lambda shape: pl.BlockSpec(shape, lambda r: (0, 0))
    win = pl.BlockSpec((SEQS_PER_STEP, KV_WIDTH, WINDOW), lambda r: (r, 0, 0))
    return pl.pallas_call(
        _sample_mixers_kernel,
        grid=(n // CHUNK,),
        in_specs=[
            wide(COL_U), wide(COL_V), wide(COL_ZA), wide(COL_Q), wide(COL_ZB),
            pl.BlockSpec((CHUNK, KVD_WIDTH), lambda r: (r, 0)),
            pl.BlockSpec((CHUNK, 2 * KV_WIDTH), lambda r: (r, 0)),
            win, win,
            const2((1, A_WIDTH)), const2((1, A_WIDTH)),
            pl.BlockSpec((DEC_SEQ, DEC_SEQ, A_WIDTH), lambda r: (0, 0, 0)),
            const2((DEC_SEQ, A_WIDTH)),
            pl.BlockSpec(memory_space=pltpu.SMEM),
        ],
        out_specs=[
            pl.BlockSpec((CHUNK, A_WIDTH), lambda r: (r, 0)),
            pl.BlockSpec((CHUNK, B_WIDTH), lambda r: (r, 0)),
            pl.BlockSpec((CHUNK, A_WIDTH), lambda r: (r, 0)),
            win, win,
        ],
        out_shape=[
            jax.ShapeDtypeStruct((n, A_WIDTH), BF16),
            jax.ShapeDtypeStruct((n, B_WIDTH), BF16),
            jax.ShapeDtypeStruct((n, A_WIDTH), F32),
            jax.ShapeDtypeStruct((n_seq, KV_WIDTH, WINDOW), F32),
            jax.ShapeDtypeStruct((n_seq, KV_WIDTH, WINDOW), F32),
        ],
        compiler_params=pltpu.CompilerParams(
            dimension_semantics=("parallel",), vmem_limit_bytes=VMEM_LIMIT),
        name="sample_mixers",
    )(z, z, z, z, z, kvd, kv32, cache_kt, cache_vt, ln_g, ln_b, coef, b_s8, sinks)


def _merge_out_kernel(x_ref, a_ref, b_ref, ga_ref, gb_ref, wa_ref, wb_ref, wo_ref, g_ref, y_ref):
    y_ref[...] = _merge(x_ref[...], a_ref[...], b_ref[...], ga_ref, gb_ref,
                        wa_ref, wb_ref, wo_ref, g_ref)


def _merge_out(x2d, a, b, z, w_a, w_b, w_o, g_post):
    n = x2d.shape[0]
    tm = MERGE_ROWS
    resident = lambda shape: pl.BlockSpec(shape, lambda i: (0, 0), pipeline_mode=pl.Buffered(1))
    return pl.pallas_call(
        _merge_out_kernel,
        grid=(n // tm,),
        in_specs=[
            pl.BlockSpec((tm, D_MODEL), lambda i: (i, 0)),
            pl.BlockSpec((tm, A_WIDTH), lambda i: (i, 0)),
            pl.BlockSpec((tm, B_WIDTH), lambda i: (i, 0)),
            pl.BlockSpec((D_MODEL // Z_BLOCK, tm, Z_BLOCK), lambda i: (COL_GA // (D_MODEL // Z_BLOCK), i, 0)),
            pl.BlockSpec((D_MODEL // Z_BLOCK, tm, Z_BLOCK), lambda i: (COL_GB // (D_MODEL // Z_BLOCK), i, 0)),
            resident((A_WIDTH, D_MODEL)),
            resident((B_WIDTH, D_MODEL)),
            resident((D_MODEL, D_MODEL)),
            resident((1, D_MODEL)),
        ],
        out_specs=pl.BlockSpec((tm, D_MODEL), lambda i: (i, 0)),
        out_shape=jax.ShapeDtypeStruct((n, D_MODEL), F32),
        compiler_params=pltpu.CompilerParams(
            dimension_semantics=("parallel",), vmem_limit_bytes=VMEM_LIMIT),
        name="merge_out",
    )(x2d, a, b, z, z, w_a, w_b, w_o, g_post)


def _cast_tile_kernel(w_ref, o_ref):
    o_ref[...] = w_ref[...].astype(BF16)


def _cast_column_tiles(w):
    rows, cols = w.shape
    return pl.pallas_call(
        _cast_tile_kernel,
        grid=(cols // COL_TILE,),
        in_specs=[pl.BlockSpec((rows, COL_TILE), lambda t: (0, t))],
        out_specs=pl.BlockSpec((None, rows, COL_TILE), lambda t: (t, 0, 0)),
        out_shape=jax.ShapeDtypeStruct((cols // COL_TILE, rows, COL_TILE), BF16),
        compiler_params=pltpu.CompilerParams(
            dimension_semantics=("parallel",), vmem_limit_bytes=VMEM_LIMIT),
        name="cast_column_tiles",
    )(w)


def _rope_tables(pos):
    lane = jnp.arange(LANES)
    inv = ROPE_THETA ** (-(2 * (lane % (HEAD_DIM // 2))).astype(F32) / HEAD_DIM)
    ang = pos.astype(F32)[:, None] * inv[None, :]
    sign = jnp.where((lane % HEAD_DIM) < HEAD_DIM // 2, -1.0, 1.0).astype(F32)
    return jnp.cos(ang), jnp.sin(ang) * sign[None, :]


def _window_first(win):
    n = win.shape[1]
    return jnp.transpose(win[0], (0, 2, 3, 1)).reshape(n, KV_WIDTH, WINDOW)


def _window_last(win_t):
    n = win_t.shape[0]
    return jnp.transpose(win_t.reshape(n, N_KV_HEADS, HEAD_DIM, WINDOW), (0, 3, 1, 2))[None]


def kernel(x_prompt, x_sample, cache_k_win, cache_v_win, g_pre, w_in, ln_v_g, ln_v_b, w_spatial,
           b_spatial, sinks, w_proj_a, w_proj_b, w_out, g_post):
    bsz, seq, _ = x_prompt.shape
    dbsz, dseq, _ = x_sample.shape
    assert seq == SEQ and dseq == DEC_SEQ and seq % IN_PROJ_ROWS == 0
    assert (dbsz * dseq) % IN_PROJ_ROWS == 0 and g_pre.shape[0] == 1

    w_in_b = _cast_column_tiles(w_in[0])
    b_s_rows = jnp.repeat(b_spatial[0].T, LANES, axis=1)
    coef = jnp.repeat(jnp.transpose(w_spatial[0][:, :DEC_SEQ, :DEC_SEQ], (2, 1, 0)), LANES, axis=2)
    cos_p, sin_p = _rope_tables(jnp.arange(SEQ))
    cos_s, sin_s = _rope_tables(PAST_LEN + jnp.arange(dseq))

    xp = x_prompt.reshape(bsz * seq, D_MODEL)
    (z_p, kvd_p, kv32_p), (w_a, w_b, w_o) = _in_proj(
        xp, g_pre, w_in_b, cos_p, sin_p, IN_PROJ_ROWS, 2, 256, 4,
        cast_along=(w_proj_a[0], w_proj_b[0], w_out[0]))
    y_p = _prompt_mix_merge(xp, z_p, kvd_p, ln_v_g, ln_v_b, w_spatial[0], b_s_rows, sinks[0],
                            w_a, w_b, w_o, g_post)
    kv_win_p = kv32_p.reshape(bsz, seq, 2 * KV_WIDTH)[:, seq - WINDOW:]
    k_win_p = kv_win_p[..., :KV_WIDTH].reshape(1, bsz, WINDOW, N_KV_HEADS, HEAD_DIM)
    v_win_p = kv_win_p[..., KV_WIDTH:].reshape(1, bsz, WINDOW, N_KV_HEADS, HEAD_DIM)

    xs = x_sample.reshape(dbsz * dseq, D_MODEL)
    (z_s, kvd_s, kv32_s), _ = _in_proj(xs, g_pre, w_in_b, cos_s, sin_s, dseq, 2, 1024, 1)
    a_s, b_s, vn_s, k_win_t, v_win_t = _sample_mixers(
        z_s, kvd_s, kv32_s, _window_first(cache_k_win), _window_first(cache_v_win),
        ln_v_g, ln_v_b, coef, b_s_rows[:dseq], sinks[0])
    y_s = _merge_out(xs, a_s, b_s, z_s, w_a, w_b, w_o, g_post)

    return (y_p.reshape(bsz, seq, D_MODEL),
            y_s.reshape(dbsz, dseq, D_MODEL),
            k_win_p, v_win_p,
            _window_last(k_win_t), _window_last(v_win_t),
            vn_s.reshape(1, dbsz, dseq, A_WIDTH))
```

```python
import functools

import numpy as np
import jax
import jax.numpy as jnp
from jax import lax
from jax.experimental import pallas as pl
from jax.experimental.pallas import tpu as pltpu

D_MODEL = 2048
SEQ = 2048
DEC_SEQ = 8
PAST_LEN = 8192
CHUNK = 128
A_WIDTH = 1024
A_GROUPS = 8
HEAD_DIM = 64
N_HEADS = 16
N_KV_HEADS = 4
Q_PER_KV = N_HEADS // N_KV_HEADS
B_WIDTH = N_HEADS * HEAD_DIM
KV_WIDTH = N_KV_HEADS * HEAD_DIM
WINDOW = 128
ROPE_THETA = 10000.0
EPS = 1e-6
IN_COLS = 3 * A_WIDTH + 2 * B_WIDTH + 2 * KV_WIDTH + 2 * D_MODEL

LANES = 128
SUBLANES = 8
BF16 = jnp.bfloat16
F32 = jnp.float32

COL_TILE = 512
N_COL_TILES = IN_COLS // COL_TILE
Z_BLOCK = 2 * COL_TILE
Z_COLS = (N_COL_TILES - 1) * COL_TILE
COL_GA, COL_GB, COL_U, COL_V, COL_ZA, COL_Q, COL_ZB = 0, 2, 4, 5, 6, 7, 8
KVD_WIDTH = 2 * N_KV_HEADS * LANES
SEC_GATE_NORM, SEC_GATE, SEC_GELU, SEC_SILU, SEC_Q, SEC_KV = range(6)


def _walk(tiles_per_step):
    sections = [((11, 12, 13, 14, 15, 16, 17, 18), SEC_GATE), ((0, 1, 2, 3), SEC_GELU),
                ((4, 5), SEC_SILU), ((6, 7), SEC_Q), ((9, 10), SEC_SILU)]
    steps = [(tiles[k:k + tiles_per_step], sec)
             for tiles, sec in sections for k in range(0, len(tiles), tiles_per_step)]
    steps = [(tiles, sec, s) for s, (tiles, sec) in enumerate(steps)]
    steps[0] = (steps[0][0], SEC_GATE_NORM, 0)
    mid = len(steps) // 2
    steps.insert(mid, ((8,) * tiles_per_step, SEC_KV, steps[mid - 1][2]))
    return steps


def _lookup(values, j):
    out = jnp.int32(values[-1])
    for k in range(len(values) - 2, -1, -1):
        out = jnp.where(j == k, jnp.int32(values[k]), out)
    return out


IN_PROJ_ROWS = 1024
MERGE_ROWS = 512
VMEM_LIMIT = 56 * 1024 * 1024


def _gelu(x):
    return 0.5 * x * (1.0 + lax.erf(x * np.float32(1.0 / np.sqrt(2.0))))


def _sigmoid(x):
    return 1.0 / (1.0 + jnp.exp(-x))


def _rope(x, cos, sin_signed):
    width = x.shape[1]
    lane = lax.broadcasted_iota(jnp.int32, x.shape, 1)
    first_half = (lane & (HEAD_DIM - 1)) < (HEAD_DIM // 2)
    partner = jnp.where(first_half,
                        pltpu.roll(x, width - HEAD_DIM // 2, axis=1),
                        pltpu.roll(x, HEAD_DIM // 2, axis=1))
    reps = width // LANES
    cos_w = jnp.concatenate([cos] * reps, axis=1) if reps > 1 else cos
    sin_w = jnp.concatenate([sin_signed] * reps, axis=1) if reps > 1 else sin_signed
    return x * cos_w + partner * sin_w


def _dup_heads(x):
    lane = lax.broadcasted_iota(jnp.int32, (x.shape[0], LANES), 1)
    low = lane < HEAD_DIM
    out = []
    for c in range(x.shape[1] // LANES):
        xc = x[:, c * LANES:(c + 1) * LANES]
        sw = pltpu.roll(xc, HEAD_DIM, axis=1)
        out += [jnp.where(low, xc, sw), jnp.where(low, sw, xc)]
    return jnp.concatenate(out, axis=1)


def _in_proj_kernel(*refs, table_rows, sub, walk, x_parts, n_cast, emit_bf16):
    n_w = len(walk[0][0])
    refs = list(refs)
    take = lambda k: [refs.pop(0) for _ in range(k)]
    x_refs = take(x_parts)
    (g_ref,) = take(1)
    w_refs = take(n_w)
    cos_ref, sin_ref = take(2)
    cast_in = take(n_cast)
    z_ref, kvd_ref, kv32_ref = take(3)
    cast_out = take(n_cast)
    wb_refs = take(n_w if emit_bf16 else 0)
    (h_scr,) = refs
    j = pl.program_id(1)
    part_rows = x_refs[0].shape[0]
    tm = part_rows * x_parts

    def tables(rs):
        if table_rows == tm:
            return cos_ref[rs, :], sin_ref[rs, :]
        tile = lambda t: jnp.broadcast_to(t[None], (sub // table_rows, table_rows, LANES)).reshape(sub, LANES)
        return tile(cos_ref[...]), tile(sin_ref[...])

    def run(epilogue, norm=False, n_tiles=n_w):
        if emit_bf16:
            for t in range(n_tiles):
                wb_refs[t][...] = w_refs[t][...].astype(BF16)
            weight = lambda t: wb_refs[t][...]
        else:
            weight = lambda t: w_refs[t][...]
        for r in range(tm // sub):
            rs = slice(r * sub, (r + 1) * sub)
            if norm:
                start = (r * sub) % part_rows
                x = x_refs[(r * sub) // part_rows][start:start + sub, :]
                ms = jnp.mean(x * x, axis=-1, keepdims=True)
                h = (x * lax.rsqrt(ms + EPS) * g_ref[...]).astype(BF16)
                h_scr[rs, :] = h
            else:
                h = h_scr[rs, :]
            for t in range(n_tiles):
                acc = jnp.dot(h, weight(t), preferred_element_type=F32)
                epilogue(acc, rs, slice(t * COL_TILE, (t + 1) * COL_TILE))

    def gate_epi(acc, rs, cs):
        z_ref[rs, cs] = _sigmoid(acc).astype(BF16)

    def gelu_epi(acc, rs, cs):
        z_ref[rs, cs] = _gelu(acc).astype(BF16)

    def silu_epi(acc, rs, cs):
        z_ref[rs, cs] = (acc * _sigmoid(acc)).astype(BF16)

    def q_epi(acc, rs, cs):
        cos, sin = tables(rs)
        z_ref[rs, cs] = (_rope(acc, cos, sin) * np.float32(HEAD_DIM ** -0.5)).astype(BF16)

    def kv_epi(acc, rs, cs):
        cos, sin = tables(rs)
        k = _rope(acc[:, :KV_WIDTH], cos, sin)
        v = acc[:, KV_WIDTH:]
        kv32_ref[rs, :KV_WIDTH] = k
        kv32_ref[rs, KV_WIDTH:] = v
        kvd_ref[rs, :KVD_WIDTH // 2] = _dup_heads(k).astype(BF16)
        kvd_ref[rs, KVD_WIDTH // 2:] = _dup_heads(v).astype(BF16)

    sec = _lookup(tuple(sec for _, sec, _ in walk), j)
    def first_step():
        run(gate_epi, norm=True)
        for src, dst in zip(cast_in, cast_out):
            dst[...] = src[...].astype(BF16)

    pl.when(sec == SEC_GATE_NORM)(first_step)
    pl.when(sec == SEC_GATE)(lambda: run(gate_epi))
    pl.when(sec == SEC_GELU)(lambda: run(gelu_epi))
    pl.when(sec == SEC_SILU)(lambda: run(silu_epi))
    pl.when(sec == SEC_Q)(lambda: run(q_epi))
    pl.when(sec == SEC_KV)(lambda: run(kv_epi, n_tiles=1))


def _in_proj(x2d, g_pre, w_in, cos_t, sin_t, table_rows, tiles_per_step, sub, x_parts, cast_along=(),
             emit_bf16=False):
    n = x2d.shape[0]
    tm = IN_PROJ_ROWS
    walk = _walk(tiles_per_step)
    n_steps = len(walk)
    if table_rows == tm:
        n_tab = cos_t.shape[0] // tm
        table_spec = pl.BlockSpec((tm, LANES), lambda i, j: (i % n_tab, 0))
    else:
        table_spec = pl.BlockSpec((table_rows, LANES), lambda i, j: (0, 0))
    w_tile = lambda j, t: _lookup(tuple(tiles[t] for tiles, _, _ in walk), j)
    tile_specs = [pl.BlockSpec((None, D_MODEL, COL_TILE), lambda i, j, t=t: (w_tile(j, t), 0, 0))
                  for t in range(tiles_per_step)]
    if emit_bf16:
        assert tiles_per_step == 1 and n == tm
        w_specs = [pl.BlockSpec((D_MODEL, COL_TILE), lambda i, j: (0, w_tile(j, 0)))]
        wb_specs, wb_shapes = tile_specs, [jax.ShapeDtypeStruct((N_COL_TILES, D_MODEL, COL_TILE), BF16)]
    else:
        w_specs, wb_specs, wb_shapes = tile_specs, [], []
    steps_per_block = Z_BLOCK // (tiles_per_step * COL_TILE)

    def z_index(i, j):
        s = _lookup(tuple(z_step for _, _, z_step in walk), j)
        return (s // steps_per_block, i, s % steps_per_block)

    n_row_tiles = n // tm
    assert sub <= tm // x_parts and x_parts < n_steps

    def x_index(i, j, p):
        tile = jnp.minimum(i + (j >= n_steps - p).astype(jnp.int32), n_row_tiles - 1)
        return (tile * x_parts + p, 0)

    x_specs = [pl.BlockSpec((tm // x_parts, D_MODEL), functools.partial(x_index, p=p))
               for p in range(x_parts)]
    cast_specs = [pl.BlockSpec((w.shape[0] // n_row_tiles, w.shape[1]), lambda i, j: (i, 0)) for w in cast_along]
    outs = pl.pallas_call(
        functools.partial(_in_proj_kernel, table_rows=table_rows, sub=sub, walk=walk, x_parts=x_parts,
                          n_cast=len(cast_along), emit_bf16=emit_bf16),
        grid=(n_row_tiles, n_steps),
        in_specs=[
            *x_specs,
            pl.BlockSpec((1, D_MODEL), lambda i, j: (0, 0)),
            *w_specs,
            table_spec, table_spec,
            *cast_specs,
        ],
        out_specs=[
            pl.BlockSpec((None, tm, tiles_per_step * COL_TILE), z_index),
            pl.BlockSpec((tm, KVD_WIDTH), lambda i, j: (i, 0)),
            pl.BlockSpec((tm, 2 * KV_WIDTH), lambda i, j: (i, 0)),
            *cast_specs,
            *wb_specs,
        ],
        out_shape=[
            jax.ShapeDtypeStruct((Z_COLS // Z_BLOCK, n, Z_BLOCK), BF16),
            jax.ShapeDtypeStruct((n, KVD_WIDTH), BF16),
            jax.ShapeDtypeStruct((n, 2 * KV_WIDTH), F32),
            *[jax.ShapeDtypeStruct(w.shape, BF16) for w in cast_along],
            *wb_shapes,
        ],
        scratch_shapes=[pltpu.VMEM((tm, D_MODEL), BF16)],
        compiler_params=pltpu.CompilerParams(
            dimension_semantics=("parallel", "arbitrary"), vmem_limit_bytes=VMEM_LIMIT),
        name="in_proj",
    )(*([x2d] * x_parts), g_pre, *([w_in] * tiles_per_step), cos_t, sin_t, *cast_along)
    return outs[:3], outs[3:]


def _layernorm_v(v_ref, lng_ref, lnb_ref):
    vg = v_ref[...].astype(F32)
    mu = jnp.mean(vg, axis=-1, keepdims=True)
    xc = vg - mu
    var = jnp.mean(xc * xc, axis=-1, keepdims=True)
    return xc * lax.rsqrt(var + EPS) * lng_ref[...] + lnb_ref[...]


def _spatial_block(u, v, za, lng, lnb, ws_ref, bs_ref, store_a):
    row = lax.broadcasted_iota(jnp.int32, (CHUNK, CHUNK), 0)
    col = lax.broadcasted_iota(jnp.int32, (CHUNK, CHUNK), 1)
    vg = v.astype(F32)
    mu = jnp.mean(vg, axis=-1, keepdims=True)
    xc = vg - mu
    var = jnp.mean(xc * xc, axis=-1, keepdims=True)
    vn_b = (xc * lax.rsqrt(var + EPS) * lng + lnb).astype(BF16)
    for g in range(A_GROUPS):
        cols = slice(g * LANES, (g + 1) * LANES)
        w = jnp.where(col <= row, ws_ref[g], 0.0).astype(BF16)
        s = jnp.dot(w, vn_b[:, cols], preferred_element_type=F32) + bs_ref[:, cols]
        store_a(cols, ((u[:, cols].astype(F32) * s) * za[:, cols].astype(F32)).astype(BF16))


def _attention_block(q, zb, kvd, kvd_prev, first_block, sinks_ref, store_b):
    t = lax.broadcasted_iota(jnp.int32, (WINDOW, 2 * WINDOW), 0)
    jj = lax.broadcasted_iota(jnp.int32, (WINDOW, 2 * WINDOW), 1)
    mask = (jj > t) & (jj <= t + WINDOW) & ((jj >= WINDOW) | jnp.logical_not(first_block))
    lane = lax.broadcasted_iota(jnp.int32, (WINDOW, LANES), 1)
    low_half = lane < HEAD_DIM
    for m in range(N_HEADS // 2):
        cols = slice(m * LANES, (m + 1) * LANES)
        kcols = slice((m // 2) * LANES, (m // 2 + 1) * LANES)
        vcols = slice(KVD_WIDTH // 2 + (m // 2) * LANES, KVD_WIDTH // 2 + (m // 2 + 1) * LANES)
        k_c = jnp.concatenate([kvd_prev[:, kcols], kvd[:, kcols]], axis=0)
        v_c = jnp.concatenate([kvd_prev[:, vcols], kvd[:, vcols]], axis=0)
        q_m = q[:, cols]
        outs = []
        for half in range(2):
            sink = sinks_ref[2 * m + half]
            q_h = jnp.where(low_half if half == 0 else ~low_half, q_m, jnp.zeros_like(q_m))
            s = lax.dot_general(q_h, k_c, (((1,), (1,)), ((), ())), preferred_element_type=F32)
            s = jnp.where(mask, s, -jnp.inf)
            mx = jnp.maximum(jnp.max(s, axis=-1, keepdims=True), sink)
            p = jnp.exp(s - mx)
            denom = jnp.sum(p, axis=-1, keepdims=True) + jnp.exp(sink - mx)
            o = jnp.dot(p.astype(BF16), v_c, preferred_element_type=F32)
            outs.append(o * (1.0 / denom))
        o_m = jnp.where(low_half, outs[0], outs[1])
        store_b(cols, (o_m * zb[:, cols].astype(F32)).astype(BF16))


def _merge(x, a, b, ga_ref, gb_ref, wa_ref, wb_ref, wo_ref, g_ref):
    gate = lambda ref: jnp.concatenate([ref[c] for c in range(ref.shape[0])], axis=1).astype(F32)
    pa = jnp.dot(a, wa_ref[...], preferred_element_type=F32)
    pb = jnp.dot(b, wb_ref[...], preferred_element_type=F32)
    merged = gate(ga_ref) * pa + gate(gb_ref) * pb
    out = jnp.dot(merged.astype(BF16), wo_ref[...], preferred_element_type=F32)
    ms = jnp.mean(out * out, axis=-1, keepdims=True)
    return x + out * lax.rsqrt(ms + EPS) * g_ref[...]


MIX_ROWS = 256
MIX_BLOCKS = MIX_ROWS // CHUNK


def _prompt_mix_merge_kernel(u_ref, v_ref, za_ref, q_ref, zb_ref, kvd_ref, kvdp_ref, x_ref, ga_ref,
                             gb_ref, lng_ref, lnb_ref, ws_ref, bs_ref, sinks_ref, wa_ref, wb_ref,
                             wo_ref, g_ref, y_ref, a_scr, b_scr):
    i = pl.program_id(0)
    last_tile = pl.num_programs(0) - 2
    tile = jnp.minimum(i, last_tile)
    slot = i % 2

    @pl.when(i == 0)
    def _():
        a_scr[1] = jnp.zeros(a_scr.shape[1:], BF16)
        b_scr[1] = jnp.zeros(b_scr.shape[1:], BF16)

    def attention(blk):
        rows = slice(blk * CHUNK, (blk + 1) * CHUNK)
        first_block = ((tile * MIX_BLOCKS + blk) % (SEQ // WINDOW)) == 0
        kvd_prev = kvdp_ref[...] if blk == 0 else kvd_ref[(blk - 1) * CHUNK:blk * CHUNK, :]

        def store_b(cols, val):
            b_scr[slot, rows, cols] = val

        _attention_block(q_ref[rows, :], zb_ref[rows, :], kvd_ref[rows, :], kvd_prev, first_block,
                         sinks_ref, store_b)

    def spatial(blk):
        rows = slice(blk * CHUNK, (blk + 1) * CHUNK)

        def store_a(cols, val):
            a_scr[slot, rows, cols] = val

        _spatial_block(u_ref[rows, :], v_ref[rows, :], za_ref[rows, :], lng_ref[...], lnb_ref[...],
                       ws_ref, bs_ref, store_a)

    for blk in range(MIX_BLOCKS):
        attention(blk)
    y_ref[...] = _merge(x_ref[...], a_scr[1 - slot], b_scr[1 - slot], ga_ref, gb_ref,
                        wa_ref, wb_ref, wo_ref, g_ref)
    for blk in range(MIX_BLOCKS):
        spatial(blk)


def _prompt_mix_merge(x2d, z, kvd, ln_g, ln_b, w_s, b_s_rows, sinks, w_a, w_b, w_o, g_post):
    n = x2d.shape[0]
    tm = MIX_ROWS
    n_tiles = n // tm
    mix_tile = lambda i: jnp.minimum(i, n_tiles - 1)
    merge_tile = lambda i: jnp.maximum(i - 1, 0)
    wide = lambda cb: pl.BlockSpec((None, tm, Z_BLOCK), lambda i, cb=cb: (cb, mix_tile(i), 0))
    gate = lambda cb: pl.BlockSpec((D_MODEL // Z_BLOCK, tm, Z_BLOCK),
                                   lambda i, cb=cb: (cb // (D_MODEL // Z_BLOCK), merge_tile(i), 0))
    const2 = lambda shape: pl.BlockSpec(shape, lambda i: (0, 0))
    resident = lambda shape: pl.BlockSpec(shape, lambda i: (0, 0), pipeline_mode=pl.Buffered(1))
    return pl.pallas_call(
        _prompt_mix_merge_kernel,
        grid=(n_tiles + 1,),
        in_specs=[
            wide(COL_U), wide(COL_V), wide(COL_ZA), wide(COL_Q), wide(COL_ZB),
            pl.BlockSpec((tm, KVD_WIDTH), lambda i: (mix_tile(i), 0)),
            pl.BlockSpec((CHUNK, KVD_WIDTH), lambda i: (jnp.maximum(mix_tile(i) * MIX_BLOCKS - 1, 0), 0)),
            pl.BlockSpec((tm, D_MODEL), lambda i: (merge_tile(i), 0)),
            gate(COL_GA), gate(COL_GB),
            const2((1, A_WIDTH)), const2((1, A_WIDTH)),
            pl.BlockSpec((A_GROUPS, CHUNK, CHUNK), lambda i: (0, 0, 0)),
            const2((CHUNK, A_WIDTH)),
            pl.BlockSpec(memory_space=pltpu.SMEM),
            resident((A_WIDTH, D_MODEL)),
            resident((B_WIDTH, D_MODEL)),
            resident((D_MODEL, D_MODEL)),
            resident((1, D_MODEL)),
        ],
        out_specs=pl.BlockSpec((tm, D_MODEL), lambda i: (merge_tile(i), 0)),
        out_shape=jax.ShapeDtypeStruct((n, D_MODEL), F32),
        scratch_shapes=[pltpu.VMEM((2, tm, A_WIDTH), BF16), pltpu.VMEM((2, tm, B_WIDTH), BF16)],
        compiler_params=pltpu.CompilerParams(
            dimension_semantics=("arbitrary",), vmem_limit_bytes=VMEM_LIMIT),
        name="prompt_mix_merge",
    )(z, z, z, z, z, kvd, kvd, x2d, z, z, ln_g, ln_b, w_s, b_s_rows, sinks, w_a, w_b, w_o, g_post)


SEQS_PER_STEP = CHUNK // DEC_SEQ


def _sample_mixers_kernel(u_ref, v_ref, za_ref, q_ref, zb_ref, kvd_ref, kv32_ref, ckt_ref, cvt_ref,
                          lng_ref, lnb_ref, coef_ref, bs_ref, sinks_ref,
                          a_ref, b_ref, vn_ref, kwin_ref, vwin_ref):
    ns = SEQS_PER_STEP
    split = lambda x: x.reshape(ns, DEC_SEQ, x.shape[-1])

    vn = _layernorm_v(v_ref, lng_ref, lnb_ref)
    vn_ref[...] = vn
    vn3 = split(vn)
    t_row = lax.broadcasted_iota(jnp.int32, (DEC_SEQ, A_WIDTH), 0)
    s_acc = jnp.broadcast_to(bs_ref[...][None], (ns, DEC_SEQ, A_WIDTH))
    for s in range(DEC_SEQ):
        coef = jnp.where(t_row >= s, coef_ref[s], 0.0)
        s_acc = s_acc + vn3[:, s:s + 1, :] * coef[None]
    a3 = split(u_ref[...].astype(F32)) * s_acc * split(za_ref[...].astype(F32))
    a_ref[...] = a3.reshape(CHUNK, A_WIDTH).astype(BF16)

    rows_q = Q_PER_KV * DEC_SEQ
    lane3 = lax.broadcasted_iota(jnp.int32, (ns, DEC_SEQ, LANES), 2)
    low3 = lane3 < HEAD_DIM
    r_idx = lax.broadcasted_iota(jnp.int32, (ns, rows_q, LANES), 1)
    l_idx = lax.broadcasted_iota(jnp.int32, (ns, rows_q, LANES), 2)
    s_idx = lax.broadcasted_iota(jnp.int32, (ns, rows_q, LANES), 0)
    t_q = r_idx & (DEC_SEQ - 1)
    mask_cache = l_idx > t_q
    mask_new = ((l_idx >> 3) == s_idx) & ((l_idx & (DEC_SEQ - 1)) <= t_q)
    head_of_row = lax.broadcasted_iota(jnp.int32, (1, rows_q, 1), 1) >> 3
    q3 = split(q_ref[...].astype(F32))
    zb3 = split(zb_ref[...].astype(F32))
    for kh in range(N_KV_HEADS):
        pieces = []
        for m in (2 * kh, 2 * kh + 1):
            q_m = q3[:, :, m * LANES:(m + 1) * LANES]
            pieces += [jnp.where(low3, q_m, 0.0), jnp.where(low3, 0.0, q_m)]
        lhs = jnp.concatenate(pieces, axis=1).astype(BF16)
        rows = slice(kh * HEAD_DIM, (kh + 1) * HEAD_DIM)
        kt = ckt_ref[:, rows, :].astype(BF16)
        vt = cvt_ref[:, rows, :].astype(BF16)
        kt2 = jnp.concatenate([kt, kt], axis=1)
        vt2 = jnp.concatenate([vt, vt], axis=1)
        k_new = kvd_ref[:, kh * LANES:(kh + 1) * LANES]
        v_new = kvd_ref[:, KVD_WIDTH // 2 + kh * LANES:KVD_WIDTH // 2 + (kh + 1) * LANES]
        s_c = lax.dot_general(lhs, kt2, (((2,), (1,)), ((0,), (0,))), preferred_element_type=F32)
        s_n = lax.dot_general(lhs.reshape(ns * rows_q, LANES), k_new, (((1,), (1,)), ((), ())),
                              preferred_element_type=F32).reshape(ns, rows_q, LANES)
        s_c = jnp.where(mask_cache, s_c, -jnp.inf)
        s_n = jnp.where(mask_new, s_n, -jnp.inf)
        sink = jnp.zeros((1, rows_q, 1), F32)
        for jq in range(Q_PER_KV):
            sink = jnp.where(head_of_row == jq, sinks_ref[Q_PER_KV * kh + jq], sink)
        mx = jnp.maximum(jnp.maximum(jnp.max(s_c, axis=-1, keepdims=True),
                                     jnp.max(s_n, axis=-1, keepdims=True)), sink)
        p_c = jnp.exp(s_c - mx)
        p_n = jnp.exp(s_n - mx)
        denom = (jnp.sum(p_c, axis=-1, keepdims=True) + jnp.sum(p_n, axis=-1, keepdims=True)
                 + jnp.exp(sink - mx))
        o = lax.dot_general(p_c.astype(BF16), vt2, (((2,), (2,)), ((0,), (0,))),
                            preferred_element_type=F32)
        o = o + jnp.dot(p_n.reshape(ns * rows_q, LANES).astype(BF16), v_new,
                        preferred_element_type=F32).reshape(ns, rows_q, LANES)
        o = o * (1.0 / denom)
        for i, m in enumerate((2 * kh, 2 * kh + 1)):
            base = 2 * DEC_SEQ * i
            o_m = jnp.where(low3, o[:, base:base + DEC_SEQ, :], o[:, base + DEC_SEQ:base + 2 * DEC_SEQ, :])
            cols = slice(m * LANES, (m + 1) * LANES)
            b_ref[:, cols] = (o_m * zb3[:, :, cols]).reshape(CHUNK, LANES).astype(BF16)

    keep = WINDOW - DEC_SEQ
    lane_w = lax.broadcasted_iota(jnp.int32, (KV_WIDTH, WINDOW), 1)
    for new_rows, cache_ref, out_ref in ((kv32_ref[:, :KV_WIDTH], ckt_ref, kwin_ref),
                                         (kv32_ref[:, KV_WIDTH:], cvt_ref, vwin_ref)):
        new_t = new_rows.T
        old = pltpu.roll(cache_ref[...].reshape(ns * KV_WIDTH, WINDOW), keep, axis=1)
        for s in range(ns):
            shifted_new = pltpu.roll(new_t, (keep - DEC_SEQ * s) % WINDOW, axis=1)
            out_ref[s] = jnp.where(lane_w < keep, old[s * KV_WIDTH:(s + 1) * KV_WIDTH], shifted_new)


def _sample_mixers(z, kvd, kv32, cache_kt, cache_vt, ln_g, ln_b, coef, b_s8, sinks):
    n = kvd.shape[0]
    n_seq = cache_kt.shape[0]
    wide = lambda cb: pl.BlockSpec((None, CHUNK, Z_BLOCK), lambda r, cb=cb: (cb, r, 0))
    const2 = lambda shape: pl.BlockSpec(shape, lambda r: (0, 0))
    win = pl.BlockSpec((SEQS_PER_STEP, KV_WIDTH, WINDOW), lambda r: (r, 0, 0))
    return pl.pallas_call(
        _sample_mixers_kernel,
        grid=(n // CHUNK,),
        in_specs=[
            wide(COL_U), wide(COL_V), wide(COL_ZA), wide(COL_Q), wide(COL_ZB),
            pl.BlockSpec((CHUNK, KVD_WIDTH), lambda r: (r, 0)),
            pl.BlockSpec((CHUNK, 2 * KV_WIDTH), lambda r: (r, 0)),
            win, win,
            const2((1, A_WIDTH)), const2((1, A_WIDTH)),
            pl.BlockSpec((DEC_SEQ, DEC_SEQ, A_WIDTH), lambda r: (0, 0, 0)),
            const2((DEC_SEQ, A_WIDTH)),
            pl.BlockSpec(memory_space=pltpu.SMEM),
        ],
        out_specs=[
            pl.BlockSpec((CHUNK, A_WIDTH), lambda r: (r, 0)),
            pl.BlockSpec((CHUNK, B_WIDTH), lambda r: (r, 0)),
            pl.BlockSpec((CHUNK, A_WIDTH), lambda r: (r, 0)),
            win, win,
        ],
        out_shape=[
            jax.ShapeDtypeStruct((n, A_WIDTH), BF16),
            jax.ShapeDtypeStruct((n, B_WIDTH), BF16),
            jax.ShapeDtypeStruct((n, A_WIDTH), F32),
            jax.ShapeDtypeStruct((n_seq, KV_WIDTH, WINDOW), F32),
            jax.ShapeDtypeStruct((n_seq, KV_WIDTH, WINDOW), F32),
        ],
        compiler_params=pltpu.CompilerParams(
            dimension_semantics=("parallel",), vmem_limit_bytes=VMEM_LIMIT),
        name="sample_mixers",
    )(z, z, z, z, z, kvd, kv32, cache_kt, cache_vt, ln_g, ln_b, coef, b_s8, sinks)


def _merge_out_kernel(x_ref, a_ref, b_ref, ga_ref, gb_ref, wa_ref, wb_ref, wo_ref, g_ref, y_ref):
    y_ref[...] = _merge(x_ref[...], a_ref[...], b_ref[...], ga_ref, gb_ref,
                        wa_ref, wb_ref, wo_ref, g_ref)


def _merge_out(x2d, a, b, z, w_a, w_b, w_o, g_post):
    n = x2d.shape[0]
    tm = MERGE_ROWS
    resident = lambda shape: pl.BlockSpec(shape, lambda i: (0, 0), pipeline_mode=pl.Buffered(1))
    return pl.pallas_call(
        _merge_out_kernel,
        grid=(n // tm,),
        in_specs=[
            pl.BlockSpec((tm, D_MODEL), lambda i: (i, 0)),
            pl.BlockSpec((tm, A_WIDTH), lambda i: (i, 0)),
            pl.BlockSpec((tm, B_WIDTH), lambda i: (i, 0)),
            pl.BlockSpec((D_MODEL // Z_BLOCK, tm, Z_BLOCK), lambda i: (COL_GA // (D_MODEL // Z_BLOCK), i, 0)),
            pl.BlockSpec((D_MODEL // Z_BLOCK, tm, Z_BLOCK), lambda i: (COL_GB // (D_MODEL // Z_BLOCK), i, 0)),
            resident((A_WIDTH, D_MODEL)),
            resident((B_WIDTH, D_MODEL)),
            resident((D_MODEL, D_MODEL)),
            resident((1, D_MODEL)),
        ],
        out_specs=pl.BlockSpec((tm, D_MODEL), lambda i: (i, 0)),
        out_shape=jax.ShapeDtypeStruct((n, D_MODEL), F32),
        compiler_params=pltpu.CompilerParams(
            dimension_semantics=("parallel",), vmem_limit_bytes=VMEM_LIMIT),
        name="merge_out",
    )(x2d, a, b, z, z, w_a, w_b, w_o, g_post)


def _rope_tables(pos):
    lane = jnp.arange(LANES)
    inv = ROPE_THETA ** (-(2 * (lane % (HEAD_DIM // 2))).astype(F32) / HEAD_DIM)
    ang = pos.astype(F32)[:, None] * inv[None, :]
    sign = jnp.where((lane % HEAD_DIM) < HEAD_DIM // 2, -1.0, 1.0).astype(F32)
    return jnp.cos(ang), jnp.sin(ang) * sign[None, :]


def _window_first(win):
    n = win.shape[1]
    return jnp.transpose(win[0], (0, 2, 3, 1)).reshape(n, KV_WIDTH, WINDOW)


def _window_last(win_t):
    n = win_t.shape[0]
    return jnp.transpose(win_t.reshape(n, N_KV_HEADS, HEAD_DIM, WINDOW), (0, 3, 1, 2))[None]


def kernel(x_prompt, x_sample, cache_k_win, cache_v_win, g_pre, w_in, ln_v_g, ln_v_b, w_spatial,
           b_spatial, sinks, w_proj_a, w_proj_b, w_out, g_post):
    bsz, seq, _ = x_prompt.shape
    dbsz, dseq, _ = x_sample.shape
    assert seq == SEQ and dseq == DEC_SEQ and seq % IN_PROJ_ROWS == 0
    assert (dbsz * dseq) % IN_PROJ_ROWS == 0 and g_pre.shape[0] == 1

    b_s_rows = jnp.repeat(b_spatial[0].T, LANES, axis=1)
    coef = jnp.repeat(jnp.transpose(w_spatial[0][:, :DEC_SEQ, :DEC_SEQ], (2, 1, 0)), LANES, axis=2)
    cos_p, sin_p = _rope_tables(jnp.arange(SEQ))
    cos_s, sin_s = _rope_tables(PAST_LEN + jnp.arange(dseq))

    xs = x_sample.reshape(dbsz * dseq, D_MODEL)
    (z_s, kvd_s, kv32_s), (w_in_b,) = _in_proj(xs, g_pre, w_in[0], cos_s, sin_s, dseq, 1, 256, 1,
                                               emit_bf16=True)

    xp = x_prompt.reshape(bsz * seq, D_MODEL)
    (z_p, kvd_p, kv32_p), (w_a, w_b, w_o) = _in_proj(
        xp, g_pre, w_in_b, cos_p, sin_p, IN_PROJ_ROWS, 2, 256, 4,
        cast_along=(w_proj_a[0], w_proj_b[0], w_out[0]))
    y_p = _prompt_mix_merge(xp, z_p, kvd_p, ln_v_g, ln_v_b, w_spatial[0], b_s_rows, sinks[0],
                            w_a, w_b, w_o, g_post)
    kv_win_p = kv32_p.reshape(bsz, seq, 2 * KV_WIDTH)[:, seq - WINDOW:]
    k_win_p = kv_win_p[..., :KV_WIDTH].reshape(1, bsz, WINDOW, N_KV_HEADS, HEAD_DIM)
    v_win_p = kv_win_p[..., KV_WIDTH:].reshape(1, bsz, WINDOW, N_KV_HEADS, HEAD_DIM)

    a_s, b_s, vn_s, k_win_t, v_win_t = _sample_mixers(
        z_s, kvd_s, kv32_s, _window_first(cache_k_win), _window_first(cache_v_win),
        ln_v_g, ln_v_b, coef, b_s_rows[:dseq], sinks[0])
    y_s = _merge_out(xs, a_s, b_s, z_s, w_a, w_b, w_o, g_post)

    return (y_p.reshape(bsz, seq, D_MODEL),
            y_s.reshape(dbsz, dseq, D_MODEL),
            k_win_p, v_win_p,
            _window_last(k_win_t), _window_last(v_win_t),
            vn_s.reshape(1, dbsz, dseq, A_WIDTH))
```

```python
import functools

import numpy as np
import jax
import jax.numpy as jnp
from jax import lax
from jax.experimental import pallas as pl
from jax.experimental.pallas import tpu as pltpu

D_MODEL = 2048
SEQ = 2048
DEC_SEQ = 8
PAST_LEN = 8192
CHUNK = 128
A_WIDTH = 1024
A_GROUPS = 8
HEAD_DIM = 64
N_HEADS = 16
N_KV_HEADS = 4
Q_PER_KV = N_HEADS // N_KV_HEADS
B_WIDTH = N_HEADS * HEAD_DIM
KV_WIDTH = N_KV_HEADS * HEAD_DIM
WINDOW = 128
ROPE_THETA = 10000.0
EPS = 1e-6
IN_COLS = 3 * A_WIDTH + 2 * B_WIDTH + 2 * KV_WIDTH + 2 * D_MODEL

LANES = 128
SUBLANES = 8
BF16 = jnp.bfloat16
F32 = jnp.float32

COL_TILE = 512
N_COL_TILES = IN_COLS // COL_TILE
Z_BLOCK = 2 * COL_TILE
Z_COLS = (N_COL_TILES - 1) * COL_TILE
COL_GA, COL_GB, COL_U, COL_V, COL_ZA, COL_Q, COL_ZB = 0, 2, 4, 5, 6, 7, 8
KVD_WIDTH = 2 * N_KV_HEADS * LANES
SEC_GATE_NORM, SEC_GATE, SEC_GELU, SEC_SILU, SEC_Q, SEC_KV = range(6)


def _walk(tiles_per_step):
    sections = [((11, 12, 13, 14, 15, 16, 17, 18), SEC_GATE), ((0, 1, 2, 3), SEC_GELU),
                ((4, 5), SEC_SILU), ((6, 7), SEC_Q), ((9, 10), SEC_SILU)]
    steps = [(tiles[k:k + tiles_per_step], sec)
             for tiles, sec in sections for k in range(0, len(tiles), tiles_per_step)]
    steps = [(tiles, sec, s) for s, (tiles, sec) in enumerate(steps)]
    steps[0] = (steps[0][0], SEC_GATE_NORM, 0)
    mid = len(steps) // 2
    steps.insert(mid, ((8,) * tiles_per_step, SEC_KV, steps[mid - 1][2]))
    return steps


def _lookup(values, j):
    out = jnp.int32(values[-1])
    for k in range(len(values) - 2, -1, -1):
        out = jnp.where(j == k, jnp.int32(values[k]), out)
    return out


IN_PROJ_ROWS = 1024
VMEM_LIMIT = 56 * 1024 * 1024


def _gelu(x):
    return 0.5 * x * (1.0 + lax.erf(x * np.float32(1.0 / np.sqrt(2.0))))


def _sigmoid(x):
    return 1.0 / (1.0 + jnp.exp(-x))


def _rope(x, cos, sin_signed):
    width = x.shape[1]
    lane = lax.broadcasted_iota(jnp.int32, x.shape, 1)
    first_half = (lane & (HEAD_DIM - 1)) < (HEAD_DIM // 2)
    partner = jnp.where(first_half,
                        pltpu.roll(x, width - HEAD_DIM // 2, axis=1),
                        pltpu.roll(x, HEAD_DIM // 2, axis=1))
    reps = width // LANES
    cos_w = jnp.concatenate([cos] * reps, axis=1) if reps > 1 else cos
    sin_w = jnp.concatenate([sin_signed] * reps, axis=1) if reps > 1 else sin_signed
    return x * cos_w + partner * sin_w


def _dup_heads(x):
    lane = lax.broadcasted_iota(jnp.int32, (x.shape[0], LANES), 1)
    low = lane < HEAD_DIM
    out = []
    for c in range(x.shape[1] // LANES):
        xc = x[:, c * LANES:(c + 1) * LANES]
        sw = pltpu.roll(xc, HEAD_DIM, axis=1)
        out += [jnp.where(low, xc, sw), jnp.where(low, sw, xc)]
    return jnp.concatenate(out, axis=1)


def _in_proj_kernel(*refs, table_rows, sub, walk, x_parts, n_cast, emit_bf16):
    n_w = len(walk[0][0])
    refs = list(refs)
    take = lambda k: [refs.pop(0) for _ in range(k)]
    x_refs = take(x_parts)
    (g_ref,) = take(1)
    w_refs = take(n_w)
    cos_ref, sin_ref = take(2)
    cast_in = take(n_cast)
    z_ref, kvd_ref, kv32_ref = take(3)
    cast_out = take(n_cast)
    wb_refs = take(n_w if emit_bf16 else 0)
    (h_scr,) = refs
    j = pl.program_id(1)
    part_rows = x_refs[0].shape[0]
    tm = part_rows * x_parts

    def tables(rs):
        if table_rows == tm:
            return cos_ref[rs, :], sin_ref[rs, :]
        tile = lambda t: jnp.broadcast_to(t[None], (sub // table_rows, table_rows, LANES)).reshape(sub, LANES)
        return tile(cos_ref[...]), tile(sin_ref[...])

    def run(epilogue, norm=False, n_tiles=n_w):
        if emit_bf16:
            for t in range(n_tiles):
                wb_refs[t][...] = w_refs[t][...].astype(BF16)
            weight = lambda t: wb_refs[t][...]
        else:
            weight = lambda t: w_refs[t][...]
        for r in range(tm // sub):
            rs = slice(r * sub, (r + 1) * sub)
            if norm:
                start = (r * sub) % part_rows
                x = x_refs[(r * sub) // part_rows][start:start + sub, :]
                ms = jnp.mean(x * x, axis=-1, keepdims=True)
                h = (x * lax.rsqrt(ms + EPS) * g_ref[...]).astype(BF16)
                h_scr[rs, :] = h
            else:
                h = h_scr[rs, :]
            for t in range(n_tiles):
                acc = jnp.dot(h, weight(t), preferred_element_type=F32)
                epilogue(acc, rs, slice(t * COL_TILE, (t + 1) * COL_TILE))

    def gate_epi(acc, rs, cs):
        z_ref[rs, cs] = _sigmoid(acc).astype(BF16)

    def gelu_epi(acc, rs, cs):
        z_ref[rs, cs] = _gelu(acc).astype(BF16)

    def silu_epi(acc, rs, cs):
        z_ref[rs, cs] = (acc * _sigmoid(acc)).astype(BF16)

    def q_epi(acc, rs, cs):
        cos, sin = tables(rs)
        z_ref[rs, cs] = (_rope(acc, cos, sin) * np.float32(HEAD_DIM ** -0.5)).astype(BF16)

    def kv_epi(acc, rs, cs):
        cos, sin = tables(rs)
        k = _rope(acc[:, :KV_WIDTH], cos, sin)
        v = acc[:, KV_WIDTH:]
        kv32_ref[rs, :KV_WIDTH] = k
        kv32_ref[rs, KV_WIDTH:] = v
        kvd_ref[rs, :KVD_WIDTH // 2] = _dup_heads(k).astype(BF16)
        kvd_ref[rs, KVD_WIDTH // 2:] = _dup_heads(v).astype(BF16)

    sec = _lookup(tuple(sec for _, sec, _ in walk), j)
    def first_step():
        run(gate_epi, norm=True)
        for src, dst in zip(cast_in, cast_out):
            dst[...] = src[...].astype(BF16)

    @pl.when(sec <= SEC_GELU)
    def _():
        pl.when(sec == SEC_GATE_NORM)(first_step)
        pl.when(sec == SEC_GATE)(lambda: run(gate_epi))
        pl.when(sec == SEC_GELU)(lambda: run(gelu_epi))

    @pl.when(sec > SEC_GELU)
    def _():
        pl.when(sec == SEC_SILU)(lambda: run(silu_epi))
        pl.when(sec == SEC_Q)(lambda: run(q_epi))
        pl.when(sec == SEC_KV)(lambda: run(kv_epi, n_tiles=1))


def _in_proj(x2d, g_pre, w_in, cos_t, sin_t, table_rows, tiles_per_step, sub, x_parts, cast_along=(),
             emit_bf16=False):
    n = x2d.shape[0]
    tm = IN_PROJ_ROWS
    walk = _walk(tiles_per_step)
    n_steps = len(walk)
    if table_rows == tm:
        n_tab = cos_t.shape[0] // tm
        table_spec = pl.BlockSpec((tm, LANES), lambda i, j: (i % n_tab, 0))
    else:
        table_spec = pl.BlockSpec((table_rows, LANES), lambda i, j: (0, 0))
    w_tile = lambda j, t: _lookup(tuple(tiles[t] for tiles, _, _ in walk), j)
    tile_specs = [pl.BlockSpec((None, D_MODEL, COL_TILE), lambda i, j, t=t: (w_tile(j, t), 0, 0))
                  for t in range(tiles_per_step)]
    if emit_bf16:
        assert tiles_per_step == 1 and n == tm
        w_specs = [pl.BlockSpec((D_MODEL, COL_TILE), lambda i, j: (0, w_tile(j, 0)))]
        wb_specs, wb_shapes = tile_specs, [jax.ShapeDtypeStruct((N_COL_TILES, D_MODEL, COL_TILE), BF16)]
    else:
        w_specs, wb_specs, wb_shapes = tile_specs, [], []
    steps_per_block = Z_BLOCK // (tiles_per_step * COL_TILE)

    def z_index(i, j):
        s = _lookup(tuple(z_step for _, _, z_step in walk), j)
        return (s // steps_per_block, i, s % steps_per_block)

    n_row_tiles = n // tm
    assert sub <= tm // x_parts and x_parts < n_steps

    def x_index(i, j, p):
        tile = jnp.minimum(i + (j >= n_steps - p).astype(jnp.int32), n_row_tiles - 1)
        return (tile * x_parts + p, 0)

    x_specs = [pl.BlockSpec((tm // x_parts, D_MODEL), functools.partial(x_index, p=p))
               for p in range(x_parts)]
    cast_specs = [pl.BlockSpec((w.shape[0] // n_row_tiles, w.shape[1]), lambda i, j: (i, 0)) for w in cast_along]
    outs = pl.pallas_call(
        functools.partial(_in_proj_kernel, table_rows=table_rows, sub=sub, walk=walk, x_parts=x_parts,
                          n_cast=len(cast_along), emit_bf16=emit_bf16),
        grid=(n_row_tiles, n_steps),
        in_specs=[
            *x_specs,
            pl.BlockSpec((1, D_MODEL), lambda i, j: (0, 0)),
            *w_specs,
            table_spec, table_spec,
            *cast_specs,
        ],
        out_specs=[
            pl.BlockSpec((None, tm, tiles_per_step * COL_TILE), z_index),
            pl.BlockSpec((tm, KVD_WIDTH), lambda i, j: (i, 0)),
            pl.BlockSpec((tm, 2 * KV_WIDTH), lambda i, j: (i, 0)),
            *cast_specs,
            *wb_specs,
        ],
        out_shape=[
            jax.ShapeDtypeStruct((Z_COLS // Z_BLOCK, n, Z_BLOCK), BF16),
            jax.ShapeDtypeStruct((n, KVD_WIDTH), BF16),
            jax.ShapeDtypeStruct((n, 2 * KV_WIDTH), F32),
            *[jax.ShapeDtypeStruct(w.shape, BF16) for w in cast_along],
            *wb_shapes,
        ],
        scratch_shapes=[pltpu.VMEM((tm, D_MODEL), BF16)],
        compiler_params=pltpu.CompilerParams(
            dimension_semantics=("parallel", "arbitrary"), vmem_limit_bytes=VMEM_LIMIT),
        name="in_proj",
    )(*([x2d] * x_parts), g_pre, *([w_in] * tiles_per_step), cos_t, sin_t, *cast_along)
    return outs[:3], outs[3:]


def _layernorm_v(v_ref, lng_ref, lnb_ref):
    vg = v_ref[...].astype(F32)
    mu = jnp.mean(vg, axis=-1, keepdims=True)
    xc = vg - mu
    var = jnp.mean(xc * xc, axis=-1, keepdims=True)
    return xc * lax.rsqrt(var + EPS) * lng_ref[...] + lnb_ref[...]


def _spatial_block(u, v, za, lng, lnb, ws_ref, bs_ref, store_a):
    row = lax.broadcasted_iota(jnp.int32, (CHUNK, CHUNK), 0)
    col = lax.broadcasted_iota(jnp.int32, (CHUNK, CHUNK), 1)
    vg = v.astype(F32)
    mu = jnp.mean(vg, axis=-1, keepdims=True)
    xc = vg - mu
    var = jnp.mean(xc * xc, axis=-1, keepdims=True)
    vn_b = (xc * lax.rsqrt(var + EPS) * lng + lnb).astype(BF16)
    for g in range(A_GROUPS):
        cols = slice(g * LANES, (g + 1) * LANES)
        w = jnp.where(col <= row, ws_ref[g], 0.0).astype(BF16)
        s = jnp.dot(w, vn_b[:, cols], preferred_element_type=F32) + bs_ref[:, cols]
        store_a(cols, ((u[:, cols].astype(F32) * s) * za[:, cols].astype(F32)).astype(BF16))


def _attention_block(q, zb, kvd, kvd_prev, first_block, sinks_ref, store_b):
    t = lax.broadcasted_iota(jnp.int32, (WINDOW, 2 * WINDOW), 0)
    jj = lax.broadcasted_iota(jnp.int32, (WINDOW, 2 * WINDOW), 1)
    mask = (jj > t) & (jj <= t + WINDOW) & ((jj >= WINDOW) | jnp.logical_not(first_block))
    lane = lax.broadcasted_iota(jnp.int32, (WINDOW, LANES), 1)
    low_half = lane < HEAD_DIM
    for m in range(N_HEADS // 2):
        cols = slice(m * LANES, (m + 1) * LANES)
        kcols = slice((m // 2) * LANES, (m // 2 + 1) * LANES)
        vcols = slice(KVD_WIDTH // 2 + (m // 2) * LANES, KVD_WIDTH // 2 + (m // 2 + 1) * LANES)
        k_c = jnp.concatenate([kvd_prev[:, kcols], kvd[:, kcols]], axis=0)
        v_c = jnp.concatenate([kvd_prev[:, vcols], kvd[:, vcols]], axis=0)
        q_m = q[:, cols]
        outs = []
        for half in range(2):
            sink = sinks_ref[2 * m + half]
            q_h = jnp.where(low_half if half == 0 else ~low_half, q_m, jnp.zeros_like(q_m))
            s = lax.dot_general(q_h, k_c, (((1,), (1,)), ((), ())), preferred_element_type=F32)
            s = jnp.where(mask, s, -jnp.inf)
            mx = jnp.maximum(jnp.max(s, axis=-1, keepdims=True), sink)
            p = jnp.exp(s - mx)
            denom = jnp.sum(p, axis=-1, keepdims=True) + jnp.exp(sink - mx)
            o = jnp.dot(p.astype(BF16), v_c, preferred_element_type=F32)
            outs.append(o * (1.0 / denom))
        o_m = jnp.where(low_half, outs[0], outs[1])
        store_b(cols, (o_m * zb[:, cols].astype(F32)).astype(BF16))


def _merge(x, a, b, ga_ref, gb_ref, wa_ref, wb_ref, wo_ref, g_ref):
    gate = lambda ref: jnp.concatenate([ref[c] for c in range(ref.shape[0])], axis=1).astype(F32)
    pa = jnp.dot(a, wa_ref[...], preferred_element_type=F32)
    pb = jnp.dot(b, wb_ref[...], preferred_element_type=F32)
    merged = gate(ga_ref) * pa + gate(gb_ref) * pb
    out = jnp.dot(merged.astype(BF16), wo_ref[...], preferred_element_type=F32)
    ms = jnp.mean(out * out, axis=-1, keepdims=True)
    return x + out * lax.rsqrt(ms + EPS) * g_ref[...]


MIX_ROWS = 256
MIX_BLOCKS = MIX_ROWS // CHUNK


def _prompt_mix_merge_kernel(u_ref, v_ref, za_ref, q_ref, zb_ref, kvd_ref, kvdp_ref, x_ref, ga_ref,
                             gb_ref, lng_ref, lnb_ref, ws_ref, bs_ref, sinks_ref, wa_ref, wb_ref,
                             wo_ref, g_ref, y_ref, a_scr, b_scr):
    i = pl.program_id(0)
    last_tile = pl.num_programs(0) - 2
    tile = jnp.minimum(i, last_tile)
    slot = i % 2

    @pl.when(i == 0)
    def _():
        a_scr[1] = jnp.zeros(a_scr.shape[1:], BF16)
        b_scr[1] = jnp.zeros(b_scr.shape[1:], BF16)

    def attention(blk):
        rows = slice(blk * CHUNK, (blk + 1) * CHUNK)
        first_block = ((tile * MIX_BLOCKS + blk) % (SEQ // WINDOW)) == 0
        kvd_prev = kvdp_ref[...] if blk == 0 else kvd_ref[(blk - 1) * CHUNK:blk * CHUNK, :]

        def store_b(cols, val):
            b_scr[slot, rows, cols] = val

        _attention_block(q_ref[rows, :], zb_ref[rows, :], kvd_ref[rows, :], kvd_prev, first_block,
                         sinks_ref, store_b)

    def spatial(blk):
        rows = slice(blk * CHUNK, (blk + 1) * CHUNK)

        def store_a(cols, val):
            a_scr[slot, rows, cols] = val

        _spatial_block(u_ref[rows, :], v_ref[rows, :], za_ref[rows, :], lng_ref[...], lnb_ref[...],
                       ws_ref, bs_ref, store_a)

    for blk in range(MIX_BLOCKS):
        attention(blk)
    y_ref[...] = _merge(x_ref[...], a_scr[1 - slot], b_scr[1 - slot], ga_ref, gb_ref,
                        wa_ref, wb_ref, wo_ref, g_ref)
    for blk in range(MIX_BLOCKS):
        spatial(blk)


def _prompt_mix_merge(x2d, z, kvd, ln_g, ln_b, w_s, b_s_rows, sinks, w_a, w_b, w_o, g_post):
    n = x2d.shape[0]
    tm = MIX_ROWS
    n_tiles = n // tm
    mix_tile = lambda i: jnp.minimum(i, n_tiles - 1)
    merge_tile = lambda i: jnp.maximum(i - 1, 0)
    wide = lambda cb: pl.BlockSpec((None, tm, Z_BLOCK), lambda i, cb=cb: (cb, mix_tile(i), 0))
    gate = lambda cb: pl.BlockSpec((D_MODEL // Z_BLOCK, tm, Z_BLOCK),
                                   lambda i, cb=cb: (cb // (D_MODEL // Z_BLOCK), merge_tile(i), 0))
    const2 = lambda shape: pl.BlockSpec(shape, lambda i: (0, 0))
    resident = lambda shape: pl.BlockSpec(shape, lambda i: (0, 0), pipeline_mode=pl.Buffered(1))
    return pl.pallas_call(
        _prompt_mix_merge_kernel,
        grid=(n_tiles + 1,),
        in_specs=[
            wide(COL_U), wide(COL_V), wide(COL_ZA), wide(COL_Q), wide(COL_ZB),
            pl.BlockSpec((tm, KVD_WIDTH), lambda i: (mix_tile(i), 0)),
            pl.BlockSpec((CHUNK, KVD_WIDTH), lambda i: (jnp.maximum(mix_tile(i) * MIX_BLOCKS - 1, 0), 0)),
            pl.BlockSpec((tm, D_MODEL), lambda i: (merge_tile(i), 0)),
            gate(COL_GA), gate(COL_GB),
            const2((1, A_WIDTH)), const2((1, A_WIDTH)),
            pl.BlockSpec((A_GROUPS, CHUNK, CHUNK), lambda i: (0, 0, 0)),
            const2((CHUNK, A_WIDTH)),
            pl.BlockSpec(memory_space=pltpu.SMEM),
            resident((A_WIDTH, D_MODEL)),
            resident((B_WIDTH, D_MODEL)),
            resident((D_MODEL, D_MODEL)),
            resident((1, D_MODEL)),
        ],
        out_specs=pl.BlockSpec((tm, D_MODEL), lambda i: (merge_tile(i), 0)),
        out_shape=jax.ShapeDtypeStruct((n, D_MODEL), F32),
        scratch_shapes=[pltpu.VMEM((2, tm, A_WIDTH), BF16), pltpu.VMEM((2, tm, B_WIDTH), BF16)],
        compiler_params=pltpu.CompilerParams(
            dimension_semantics=("arbitrary",), vmem_limit_bytes=VMEM_LIMIT),
        name="prompt_mix_merge",
    )(z, z, z, z, z, kvd, kvd, x2d, z, z, ln_g, ln_b, w_s, b_s_rows, sinks, w_a, w_b, w_o, g_post)


SEQS_PER_STEP = CHUNK // DEC_SEQ


def _sample_mix_merge_kernel(u_ref, v_ref, za_ref, q_ref, zb_ref, kvd_ref, kv32_ref, ckt_ref, cvt_ref,
                             lng_ref, lnb_ref, coef_ref, bs_ref, sinks_ref, x_ref, ga_ref, gb_ref,
                             wa_ref, wb_ref, wo_ref, g_ref,
                             vn_ref, kwin_ref, vwin_ref, y_ref, a_scr, b_scr):
    i = pl.program_id(0)
    slot = i % 2

    @pl.when(i == 0)
    def _():
        a_scr[1] = jnp.zeros(a_scr.shape[1:], BF16)
        b_scr[1] = jnp.zeros(b_scr.shape[1:], BF16)

    ns = SEQS_PER_STEP
    split = lambda x: x.reshape(ns, DEC_SEQ, x.shape[-1])

    vn = _layernorm_v(v_ref, lng_ref, lnb_ref)
    vn_ref[...] = vn
    vn3 = split(vn)
    t_row = lax.broadcasted_iota(jnp.int32, (DEC_SEQ, A_WIDTH), 0)
    s_acc = jnp.broadcast_to(bs_ref[...][None], (ns, DEC_SEQ, A_WIDTH))
    for s in range(DEC_SEQ):
        coef = jnp.where(t_row >= s, coef_ref[s], 0.0)
        s_acc = s_acc + vn3[:, s:s + 1, :] * coef[None]
    a3 = split(u_ref[...].astype(F32)) * s_acc * split(za_ref[...].astype(F32))
    a_scr[slot] = a3.reshape(CHUNK, A_WIDTH).astype(BF16)

    rows_q = Q_PER_KV * DEC_SEQ
    lane3 = lax.broadcasted_iota(jnp.int32, (ns, DEC_SEQ, LANES), 2)
    low3 = lane3 < HEAD_DIM
    r_idx = lax.broadcasted_iota(jnp.int32, (ns, rows_q, LANES), 1)
    l_idx = lax.broadcasted_iota(jnp.int32, (ns, rows_q, LANES), 2)
    s_idx = lax.broadcasted_iota(jnp.int32, (ns, rows_q, LANES), 0)
    t_q = r_idx & (DEC_SEQ - 1)
    mask_cache = l_idx > t_q
    mask_new = ((l_idx >> 3) == s_idx) & ((l_idx & (DEC_SEQ - 1)) <= t_q)
    head_of_row = lax.broadcasted_iota(jnp.int32, (1, rows_q, 1), 1) >> 3
    q3 = split(q_ref[...].astype(F32))
    zb3 = split(zb_ref[...].astype(F32))
    for kh in range(N_KV_HEADS):
        pieces = []
        for m in (2 * kh, 2 * kh + 1):
            q_m = q3[:, :, m * LANES:(m + 1) * LANES]
            pieces += [jnp.where(low3, q_m, 0.0), jnp.where(low3, 0.0, q_m)]
        lhs = jnp.concatenate(pieces, axis=1).astype(BF16)
        rows = slice(kh * HEAD_DIM, (kh + 1) * HEAD_DIM)
        kt = ckt_ref[:, rows, :].astype(BF16)
        vt = cvt_ref[:, rows, :].astype(BF16)
        kt2 = jnp.concatenate([kt, kt], axis=1)
        vt2 = jnp.concatenate([vt, vt], axis=1)
        k_new = kvd_ref[:, kh * LANES:(kh + 1) * LANES]
        v_new = kvd_ref[:, KVD_WIDTH // 2 + kh * LANES:KVD_WIDTH // 2 + (kh + 1) * LANES]
        s_c = lax.dot_general(lhs, kt2, (((2,), (1,)), ((0,), (0,))), preferred_element_type=F32)
        s_n = lax.dot_general(lhs.reshape(ns * rows_q, LANES), k_new, (((1,), (1,)), ((), ())),
                              preferred_element_type=F32).reshape(ns, rows_q, LANES)
        s_c = jnp.where(mask_cache, s_c, -jnp.inf)
        s_n = jnp.where(mask_new, s_n, -jnp.inf)
        sink = jnp.zeros((1, rows_q, 1), F32)
        for jq in range(Q_PER_KV):
            sink = jnp.where(head_of_row == jq, sinks_ref[Q_PER_KV * kh + jq], sink)
        mx = jnp.maximum(jnp.maximum(jnp.max(s_c, axis=-1, keepdims=True),
                                     jnp.max(s_n, axis=-1, keepdims=True)), sink)
        p_c = jnp.exp(s_c - mx)
        p_n = jnp.exp(s_n - mx)
        denom = (jnp.sum(p_c, axis=-1, keepdims=True) + jnp.sum(p_n, axis=-1, keepdims=True)
                 + jnp.exp(sink - mx))
        o = lax.dot_general(p_c.astype(BF16), vt2, (((2,), (2,)), ((0,), (0,))),
                            preferred_element_type=F32)
        o = o + jnp.dot(p_n.reshape(ns * rows_q, LANES).astype(BF16), v_new,
                        preferred_element_type=F32).reshape(ns, rows_q, LANES)
        o = o * (1.0 / denom)
        for i, m in enumerate((2 * kh, 2 * kh + 1)):
            base = 2 * DEC_SEQ * i
            o_m = jnp.where(low3, o[:, base:base + DEC_SEQ, :], o[:, base + DEC_SEQ:base + 2 * DEC_SEQ, :])
            cols = slice(m * LANES, (m + 1) * LANES)
            b_scr[slot, :, cols] = (o_m * zb3[:, :, cols]).reshape(CHUNK, LANES).astype(BF16)

    y_ref[...] = _merge(x_ref[...], a_scr[1 - slot], b_scr[1 - slot], ga_ref, gb_ref,
                        wa_ref, wb_ref, wo_ref, g_ref)

    keep = WINDOW - DEC_SEQ
    lane_w = lax.broadcasted_iota(jnp.int32, (KV_WIDTH, WINDOW), 1)
    for new_rows, cache_ref, out_ref in ((kv32_ref[:, :KV_WIDTH], ckt_ref, kwin_ref),
                                         (kv32_ref[:, KV_WIDTH:], cvt_ref, vwin_ref)):
        new_t = new_rows.T
        old = pltpu.roll(cache_ref[...].reshape(ns * KV_WIDTH, WINDOW), keep, axis=1)
        for s in range(ns):
            shifted_new = pltpu.roll(new_t, (keep - DEC_SEQ * s) % WINDOW, axis=1)
            out_ref[s] = jnp.where(lane_w < keep, old[s * KV_WIDTH:(s + 1) * KV_WIDTH], shifted_new)


def _sample_mix_merge(x2d, z, kvd, kv32, cache_kt, cache_vt, ln_g, ln_b, coef, b_s8, sinks,
                      w_a, w_b, w_o, g_post):
    n = x2d.shape[0]
    n_blocks = n // CHUNK
    n_seq = cache_kt.shape[0]
    mix_blk = lambda i: jnp.minimum(i, n_blocks - 1)
    merge_blk = lambda i: jnp.maximum(i - 1, 0)
    wide = lambda cb: pl.BlockSpec((None, CHUNK, Z_BLOCK), lambda i, cb=cb: (cb, mix_blk(i), 0))
    gate = lambda cb: pl.BlockSpec((D_MODEL // Z_BLOCK, CHUNK, Z_BLOCK),
                                   lambda i, cb=cb: (cb // (D_MODEL // Z_BLOCK), merge_blk(i), 0))
    const2 = lambda shape: pl.BlockSpec(shape, lambda i: (0, 0))
    resident = lambda shape: pl.BlockSpec(shape, lambda i: (0, 0), pipeline_mode=pl.Buffered(1))
    win = pl.BlockSpec((SEQS_PER_STEP, KV_WIDTH, WINDOW), lambda i: (mix_blk(i), 0, 0))
    return pl.pallas_call(
        _sample_mix_merge_kernel,
        grid=(n_blocks + 1,),
        in_specs=[
            wide(COL_U), wide(COL_V), wide(COL_ZA), wide(COL_Q), wide(COL_ZB),
            pl.BlockSpec((CHUNK, KVD_WIDTH), lambda i: (mix_blk(i), 0)),
            pl.BlockSpec((CHUNK, 2 * KV_WIDTH), lambda i: (mix_blk(i), 0)),
            win, win,
            const2((1, A_WIDTH)), const2((1, A_WIDTH)),
            pl.BlockSpec((DEC_SEQ, DEC_SEQ, A_WIDTH), lambda i: (0, 0, 0)),
            const2((DEC_SEQ, A_WIDTH)),
            pl.BlockSpec(memory_space=pltpu.SMEM),
            pl.BlockSpec((CHUNK, D_MODEL), lambda i: (merge_blk(i), 0)),
            gate(COL_GA), gate(COL_GB),
            resident((A_WIDTH, D_MODEL)),
            resident((B_WIDTH, D_MODEL)),
            resident((D_MODEL, D_MODEL)),
            resident((1, D_MODEL)),
        ],
        out_specs=[
            pl.BlockSpec((CHUNK, A_WIDTH), lambda i: (mix_blk(i), 0)),
            win, win,
            pl.BlockSpec((CHUNK, D_MODEL), lambda i: (merge_blk(i), 0)),
        ],
        out_shape=[
            jax.ShapeDtypeStruct((n, A_WIDTH), F32),
            jax.ShapeDtypeStruct((n_seq, KV_WIDTH, WINDOW), F32),
            jax.ShapeDtypeStruct((n_seq, KV_WIDTH, WINDOW), F32),
            jax.ShapeDtypeStruct((n, D_MODEL), F32),
        ],
        scratch_shapes=[pltpu.VMEM((2, CHUNK, A_WIDTH), BF16), pltpu.VMEM((2, CHUNK, B_WIDTH), BF16)],
        compiler_params=pltpu.CompilerParams(
            dimension_semantics=("arbitrary",), vmem_limit_bytes=VMEM_LIMIT),
        name="sample_mix_merge",
    )(z, z, z, z, z, kvd, kv32, cache_kt, cache_vt, ln_g, ln_b, coef, b_s8, sinks,
      x2d, z, z, w_a, w_b, w_o, g_post)


def _rope_tables(pos):
    lane = jnp.arange(LANES)
    inv = ROPE_THETA ** (-(2 * (lane % (HEAD_DIM // 2))).astype(F32) / HEAD_DIM)
    ang = pos.astype(F32)[:, None] * inv[None, :]
    sign = jnp.where((lane % HEAD_DIM) < HEAD_DIM // 2, -1.0, 1.0).astype(F32)
    return jnp.cos(ang), jnp.sin(ang) * sign[None, :]


def _window_first(win):
    n = win.shape[1]
    return jnp.transpose(win[0], (0, 2, 3, 1)).reshape(n, KV_WIDTH, WINDOW)


def _window_last(win_t):
    n = win_t.shape[0]
    return jnp.transpose(win_t.reshape(n, N_KV_HEADS, HEAD_DIM, WINDOW), (0, 3, 1, 2))[None]


def kernel(x_prompt, x_sample, cache_k_win, cache_v_win, g_pre, w_in, ln_v_g, ln_v_b, w_spatial,
           b_spatial, sinks, w_proj_a, w_proj_b, w_out, g_post):
    bsz, seq, _ = x_prompt.shape
    dbsz, dseq, _ = x_sample.shape
    assert seq == SEQ and dseq == DEC_SEQ and seq % IN_PROJ_ROWS == 0
    assert (dbsz * dseq) % IN_PROJ_ROWS == 0 and g_pre.shape[0] == 1

    b_s_rows = jnp.repeat(b_spatial[0].T, LANES, axis=1)
    coef = jnp.repeat(jnp.transpose(w_spatial[0][:, :DEC_SEQ, :DEC_SEQ], (2, 1, 0)), LANES, axis=2)
    cos_p, sin_p = _rope_tables(jnp.arange(SEQ))
    cos_s, sin_s = _rope_tables(PAST_LEN + jnp.arange(dseq))

    xs = x_sample.reshape(dbsz * dseq, D_MODEL)
    (z_s, kvd_s, kv32_s), (w_in_b,) = _in_proj(xs, g_pre, w_in[0], cos_s, sin_s, dseq, 1, 256, 1,
                                               emit_bf16=True)

    xp = x_prompt.reshape(bsz * seq, D_MODEL)
    (z_p, kvd_p, kv32_p), (w_a, w_b, w_o) = _in_proj(
        xp, g_pre, w_in_b, cos_p, sin_p, IN_PROJ_ROWS, 2, 256, 4,
        cast_along=(w_proj_a[0], w_proj_b[0], w_out[0]))
    y_p = _prompt_mix_merge(xp, z_p, kvd_p, ln_v_g, ln_v_b, w_spatial[0], b_s_rows, sinks[0],
                            w_a, w_b, w_o, g_post)
    kv_win_p = kv32_p.reshape(bsz, seq, 2 * KV_WIDTH)[:, seq - WINDOW:]
    k_win_p = kv_win_p[..., :KV_WIDTH].reshape(1, bsz, WINDOW, N_KV_HEADS, HEAD_DIM)
    v_win_p = kv_win_p[..., KV_WIDTH:].reshape(1, bsz, WINDOW, N_KV_HEADS, HEAD_DIM)

    vn_s, k_win_t, v_win_t, y_s = _sample_mix_merge(
        xs, z_s, kvd_s, kv32_s, _window_first(cache_k_win), _window_first(cache_v_win),
        ln_v_g, ln_v_b, coef, b_s_rows[:dseq], sinks[0], w_a, w_b, w_o, g_post)

    return (y_p.reshape(bsz, seq, D_MODEL),
            y_s.reshape(dbsz, dseq, D_MODEL),
            k_win_p, v_win_p,
            _window_last(k_win_t), _window_last(v_win_t),
            vn_s.reshape(1, dbsz, dseq, A_WIDTH))
```

```python
import functools

import numpy as np
import jax
import jax.numpy as jnp
from jax import lax
from jax.experimental import pallas as pl
from jax.experimental.pallas import tpu as pltpu

D_MODEL = 2048
SEQ = 2048
DEC_SEQ = 8
PAST_LEN = 8192
CHUNK = 128
A_WIDTH = 1024
A_GROUPS = 8
HEAD_DIM = 64
N_HEADS = 16
N_KV_HEADS = 4
Q_PER_KV = N_HEADS // N_KV_HEADS
B_WIDTH = N_HEADS * HEAD_DIM
KV_WIDTH = N_KV_HEADS * HEAD_DIM
WINDOW = 128
ROPE_THETA = 10000.0
EPS = 1e-6
IN_COLS = 3 * A_WIDTH + 2 * B_WIDTH + 2 * KV_WIDTH + 2 * D_MODEL

LANES = 128
SUBLANES = 8
BF16 = jnp.bfloat16
F32 = jnp.float32

COL_TILE = 512
N_COL_TILES = IN_COLS // COL_TILE
Z_BLOCK = 2 * COL_TILE
Z_COLS = (N_COL_TILES - 1) * COL_TILE
COL_GA, COL_GB, COL_U, COL_V, COL_ZA, COL_Q, COL_ZB = 0, 2, 4, 5, 6, 7, 8
KVD_WIDTH = 2 * N_KV_HEADS * LANES
EPI_GATE, EPI_GELU, EPI_SILU, EPI_Q, EPI_KV = "gate", "gelu", "silu", "q", "kv"
KV_TILE = 8


def _walk(tiles_per_step):
    sections = [((11, 12, 13, 14, 15, 16, 17, 18), EPI_GATE), ((0, 1, 2, 3), EPI_GELU),
                ((4, 5), EPI_SILU), ((6, 7), EPI_Q), ((9, 10), EPI_SILU)]
    slots = [(t, e) for tiles, e in sections for t in tiles]
    steps = []
    for s, k in enumerate(range(0, len(slots), tiles_per_step)):
        chunk = slots[k:k + tiles_per_step]
        steps.append(([t for t, _ in chunk], [e for _, e in chunk], s))
    tiles, epis, _ = steps[-1]
    if len(tiles) < tiles_per_step:
        pad = tiles_per_step - len(tiles)
        tiles += [KV_TILE] * pad
        epis += [EPI_KV] + [None] * (pad - 1)
    else:
        mid = len(steps) // 2
        steps.insert(mid, ([KV_TILE] * tiles_per_step, [EPI_KV] + [None] * (tiles_per_step - 1),
                           steps[mid - 1][2]))
    return [(tuple(t), tuple(e), z) for t, e, z in steps]


def _lookup(values, j):
    out = jnp.int32(values[-1])
    for k in range(len(values) - 2, -1, -1):
        out = jnp.where(j == k, jnp.int32(values[k]), out)
    return out


IN_PROJ_ROWS = 1024
PROMPT_TILES_PER_STEP = 4
VMEM_LIMIT = 56 * 1024 * 1024
VMEM_LIMIT_IN_PROJ = 62 * 1024 * 1024


def _gelu(x):
    return 0.5 * x * (1.0 + lax.erf(x * np.float32(1.0 / np.sqrt(2.0))))


def _sigmoid(x):
    return 1.0 / (1.0 + jnp.exp(-x))


def _rope(x, cos, sin_signed):
    width = x.shape[1]
    lane = lax.broadcasted_iota(jnp.int32, x.shape, 1)
    first_half = (lane & (HEAD_DIM - 1)) < (HEAD_DIM // 2)
    partner = jnp.where(first_half,
                        pltpu.roll(x, width - HEAD_DIM // 2, axis=1),
                        pltpu.roll(x, HEAD_DIM // 2, axis=1))
    reps = width // LANES
    cos_w = jnp.concatenate([cos] * reps, axis=1) if reps > 1 else cos
    sin_w = jnp.concatenate([sin_signed] * reps, axis=1) if reps > 1 else sin_signed
    return x * cos_w + partner * sin_w


def _dup_heads(x):
    lane = lax.broadcasted_iota(jnp.int32, (x.shape[0], LANES), 1)
    low = lane < HEAD_DIM
    out = []
    for c in range(x.shape[1] // LANES):
        xc = x[:, c * LANES:(c + 1) * LANES]
        sw = pltpu.roll(xc, HEAD_DIM, axis=1)
        out += [jnp.where(low, xc, sw), jnp.where(low, sw, xc)]
    return jnp.concatenate(out, axis=1)


def _in_proj_kernel(*refs, table_rows, sub, walk, x_parts, n_cast, emit_bf16, wide):
    n_w = len(walk[0][0])
    refs = list(refs)
    take = lambda k: [refs.pop(0) for _ in range(k)]
    x_refs = take(x_parts)
    (g_ref,) = take(1)
    w_refs = take(1 if wide else n_w)
    cos_ref, sin_ref = take(2)
    cast_in = take(n_cast)
    z_ref, kvd_ref, kv32_ref = take(3)
    cast_out = take(n_cast)
    wb_refs = take(n_w if emit_bf16 else 0)
    (h_scr,) = refs
    j = pl.program_id(1)
    part_rows = x_refs[0].shape[0]
    tm = part_rows * x_parts

    def tables(rs):
        if table_rows == tm:
            return cos_ref[rs, :], sin_ref[rs, :]
        tile = lambda t: jnp.broadcast_to(t[None], (sub // table_rows, table_rows, LANES)).reshape(sub, LANES)
        return tile(cos_ref[...]), tile(sin_ref[...])

    def z_store(t, rs, val):
        col = t * COL_TILE
        if len(z_ref.shape) == 3:
            z_ref[col // Z_BLOCK, rs, col % Z_BLOCK:col % Z_BLOCK + COL_TILE] = val
        else:
            z_ref[rs, col:col + COL_TILE] = val

    def kv_store(acc, rs):
        cos, sin = tables(rs)
        k = _rope(acc[:, :KV_WIDTH], cos, sin)
        v = acc[:, KV_WIDTH:]
        kv32_ref[rs, :KV_WIDTH] = k
        kv32_ref[rs, KV_WIDTH:] = v
        kvd_ref[rs, :KVD_WIDTH // 2] = _dup_heads(k).astype(BF16)
        kvd_ref[rs, KVD_WIDTH // 2:] = _dup_heads(v).astype(BF16)

    z_value = {
        EPI_GATE: lambda acc, rs: _sigmoid(acc),
        EPI_GELU: lambda acc, rs: _gelu(acc),
        EPI_SILU: lambda acc, rs: acc * _sigmoid(acc),
        EPI_Q: lambda acc, rs: _rope(acc, *tables(rs)) * np.float32(HEAD_DIM ** -0.5),
    }

    def run(epis, norm):
        if emit_bf16:
            for t, epi in enumerate(epis):
                if epi is not None:
                    wb_refs[t][...] = w_refs[t][...].astype(BF16)
            product = lambda h, t: jnp.dot(h, wb_refs[t][...], preferred_element_type=F32)
        elif not wide:
            product = lambda h, t: jnp.dot(h, w_refs[t][...], preferred_element_type=F32)
        n_used = sum(epi is not None for epi in epis)
        stores_z = any(epi in z_value for epi in epis)
        for r in range(tm // sub):
            rs = slice(r * sub, (r + 1) * sub)
            if norm:
                start = (r * sub) % part_rows
                x = x_refs[(r * sub) // part_rows][start:start + sub, :]
                ms = jnp.mean(x * x, axis=-1, keepdims=True)
                h = (x * lax.rsqrt(ms + EPS) * g_ref[...]).astype(BF16)
                h_scr[rs, :] = h
            else:
                h = h_scr[rs, :]
            if wide:
                wide_acc = jnp.dot(h, w_refs[0][:, :n_used * COL_TILE], preferred_element_type=F32)
                product = lambda h, t: wide_acc[:, t * COL_TILE:(t + 1) * COL_TILE]
            for t, epi in enumerate(epis):
                if epi in z_value:
                    z_store(t, rs, z_value[epi](product(h, t), rs).astype(BF16))
                else:
                    if epi == EPI_KV:
                        kv_store(product(h, t), rs)
                    if stores_z:
                        z_store(t, rs, jnp.zeros((sub, COL_TILE), BF16))
        if norm:
            for src, dst in zip(cast_in, cast_out):
                dst[...] = src[...].astype(BF16)

    kinds = []
    for s, (_, epis, _) in enumerate(walk):
        if (epis, s == 0) not in kinds:
            kinds.append((epis, s == 0))
    kind = _lookup(tuple(kinds.index((epis, s == 0)) for s, (_, epis, _) in enumerate(walk)), j)
    for k, (epis, norm) in enumerate(kinds):
        pl.when(kind == k)(functools.partial(run, epis, norm))


def _in_proj(x2d, g_pre, w_in, cos_t, sin_t, table_rows, tiles_per_step, sub, x_parts, cast_along=(),
             emit_for=None):
    n = x2d.shape[0]
    tm = IN_PROJ_ROWS
    walk = _walk(tiles_per_step)
    n_steps = len(walk)
    if table_rows == tm:
        n_tab = cos_t.shape[0] // tm
        table_spec = pl.BlockSpec((tm, LANES), lambda i, j: (i % n_tab, 0))
    else:
        table_spec = pl.BlockSpec((table_rows, LANES), lambda i, j: (0, 0))
    if emit_for is not None:
        assert tiles_per_step == 1 and n == tm
        target = _walk(emit_for)
        place = {t: (s, k) for s, (tiles, epis, _) in enumerate(target)
                 for k, (t, e) in enumerate(zip(tiles, epis)) if e is not None}
        tile_of = tuple(tiles[0] for tiles, _, _ in walk)
        w_specs = [pl.BlockSpec((D_MODEL, COL_TILE), lambda i, j: (0, _lookup(tile_of, j)))]
        wb_specs = [pl.BlockSpec(
            (None, D_MODEL, COL_TILE),
            lambda i, j: (_lookup(tuple(place[t][0] for t in tile_of), j), 0,
                          _lookup(tuple(place[t][1] for t in tile_of), j)))]
        wb_shapes = [jax.ShapeDtypeStruct((len(target), D_MODEL, emit_for * COL_TILE), BF16)]
    else:
        assert w_in.shape == (n_steps, D_MODEL, tiles_per_step * COL_TILE)
        w_specs = [pl.BlockSpec((None, D_MODEL, tiles_per_step * COL_TILE), lambda i, j: (j, 0, 0))]
        wb_specs, wb_shapes = [], []
    z_step = lambda j: _lookup(tuple(z for _, _, z in walk), j)
    n_z_steps = walk[-1][2] + 1
    step_cols = tiles_per_step * COL_TILE
    if step_cols <= Z_BLOCK:
        steps_per_block = Z_BLOCK // step_cols
        z_blocks = n_z_steps // steps_per_block
        z_spec = pl.BlockSpec((None, tm, step_cols),
                              lambda i, j: (z_step(j) // steps_per_block, i, z_step(j) % steps_per_block))
    else:
        z_blocks = n_z_steps * (step_cols // Z_BLOCK)
        z_spec = pl.BlockSpec((step_cols // Z_BLOCK, tm, Z_BLOCK), lambda i, j: (z_step(j), i, 0))
    assert z_blocks * Z_BLOCK >= Z_COLS

    n_row_tiles = n // tm
    assert sub <= tm // x_parts and x_parts < n_steps

    def x_index(i, j, p):
        tile = jnp.minimum(i + (j >= n_steps - p).astype(jnp.int32), n_row_tiles - 1)
        return (tile * x_parts + p, 0)

    x_specs = [pl.BlockSpec((tm // x_parts, D_MODEL), functools.partial(x_index, p=p))
               for p in range(x_parts)]
    cast_specs = [pl.BlockSpec((w.shape[0] // n_row_tiles, w.shape[1]), lambda i, j: (i, 0)) for w in cast_along]
    outs = pl.pallas_call(
        functools.partial(_in_proj_kernel, table_rows=table_rows, sub=sub, walk=walk, x_parts=x_parts,
                          n_cast=len(cast_along), emit_bf16=emit_for is not None, wide=emit_for is None),
        grid=(n_row_tiles, n_steps),
        in_specs=[
            *x_specs,
            pl.BlockSpec((1, D_MODEL), lambda i, j: (0, 0)),
            *w_specs,
            table_spec, table_spec,
            *cast_specs,
        ],
        out_specs=[
            z_spec,
            pl.BlockSpec((tm, KVD_WIDTH), lambda i, j: (i, 0)),
            pl.BlockSpec((tm, 2 * KV_WIDTH), lambda i, j: (i, 0)),
            *cast_specs,
            *wb_specs,
        ],
        out_shape=[
            jax.ShapeDtypeStruct((z_blocks, n, Z_BLOCK), BF16),
            jax.ShapeDtypeStruct((n, KVD_WIDTH), BF16),
            jax.ShapeDtypeStruct((n, 2 * KV_WIDTH), F32),
            *[jax.ShapeDtypeStruct(w.shape, BF16) for w in cast_along],
            *wb_shapes,
        ],
        scratch_shapes=[pltpu.VMEM((tm, D_MODEL), BF16)],
        compiler_params=pltpu.CompilerParams(
            dimension_semantics=("parallel", "arbitrary"), vmem_limit_bytes=VMEM_LIMIT_IN_PROJ),
        name="in_proj",
    )(*([x2d] * x_parts), g_pre, w_in, cos_t, sin_t, *cast_along)
    return outs[:3], outs[3:]


def _layernorm_v(v_ref, lng_ref, lnb_ref):
    vg = v_ref[...].astype(F32)
    mu = jnp.mean(vg, axis=-1, keepdims=True)
    xc = vg - mu
    var = jnp.mean(xc * xc, axis=-1, keepdims=True)
    return xc * lax.rsqrt(var + EPS) * lng_ref[...] + lnb_ref[...]


def _spatial_block(u, v, za, lng, lnb, ws_ref, bs_ref, store_a):
    row = lax.broadcasted_iota(jnp.int32, (CHUNK, CHUNK), 0)
    col = lax.broadcasted_iota(jnp.int32, (CHUNK, CHUNK), 1)
    vg = v.astype(F32)
    mu = jnp.mean(vg, axis=-1, keepdims=True)
    xc = vg - mu
    var = jnp.mean(xc * xc, axis=-1, keepdims=True)
    vn_b = (xc * lax.rsqrt(var + EPS) * lng + lnb).astype(BF16)
    for g in range(A_GROUPS):
        cols = slice(g * LANES, (g + 1) * LANES)
        w = jnp.where(col <= row, ws_ref[g], 0.0).astype(BF16)
        s = jnp.dot(w, vn_b[:, cols], preferred_element_type=F32) + bs_ref[:, cols]
        store_a(cols, ((u[:, cols].astype(F32) * s) * za[:, cols].astype(F32)).astype(BF16))


def _attention_block(q, zb, kvd, kvd_prev, first_block, sinks_ref, store_b):
    t = lax.broadcasted_iota(jnp.int32, (WINDOW, 2 * WINDOW), 0)
    jj = lax.broadcasted_iota(jnp.int32, (WINDOW, 2 * WINDOW), 1)
    mask = (jj > t) & (jj <= t + WINDOW) & ((jj >= WINDOW) | jnp.logical_not(first_block))
    lane = lax.broadcasted_iota(jnp.int32, (WINDOW, LANES), 1)
    low_half = lane < HEAD_DIM
    for m in range(N_HEADS // 2):
        cols = slice(m * LANES, (m + 1) * LANES)
        kcols = slice((m // 2) * LANES, (m // 2 + 1) * LANES)
        vcols = slice(KVD_WIDTH // 2 + (m // 2) * LANES, KVD_WIDTH // 2 + (m // 2 + 1) * LANES)
        k_c = jnp.concatenate([kvd_prev[:, kcols], kvd[:, kcols]], axis=0)
        v_c = jnp.concatenate([kvd_prev[:, vcols], kvd[:, vcols]], axis=0)
        q_m = q[:, cols]
        outs = []
        for half in range(2):
            sink = sinks_ref[2 * m + half]
            q_h = jnp.where(low_half if half == 0 else ~low_half, q_m, jnp.zeros_like(q_m))
            s = lax.dot_general(q_h, k_c, (((1,), (1,)), ((), ())), preferred_element_type=F32)
            s = jnp.where(mask, s, -jnp.inf)
            mx = jnp.maximum(jnp.max(s, axis=-1, keepdims=True), sink)
            p = jnp.exp(s - mx)
            denom = jnp.sum(p, axis=-1, keepdims=True) + jnp.exp(sink - mx)
            o = jnp.dot(p.astype(BF16), v_c, preferred_element_type=F32)
            outs.append(o * (1.0 / denom))
        o_m = jnp.where(low_half, outs[0], outs[1])
        store_b(cols, (o_m * zb[:, cols].astype(F32)).astype(BF16))


def _merge(x, a, b, ga_ref, gb_ref, wa_ref, wb_ref, wo_ref, g_ref):
    gate = lambda ref: jnp.concatenate([ref[c] for c in range(ref.shape[0])], axis=1).astype(F32)
    pa = jnp.dot(a, wa_ref[...], preferred_element_type=F32)
    pb = jnp.dot(b, wb_ref[...], preferred_element_type=F32)
    merged = gate(ga_ref) * pa + gate(gb_ref) * pb
    out = jnp.dot(merged.astype(BF16), wo_ref[...], preferred_element_type=F32)
    ms = jnp.mean(out * out, axis=-1, keepdims=True)
    return x + out * lax.rsqrt(ms + EPS) * g_ref[...]


MIX_ROWS = 256
MIX_BLOCKS = MIX_ROWS // CHUNK


def _prompt_mix_merge_kernel(u_ref, v_ref, za_ref, q_ref, zb_ref, kvd_ref, kvdp_ref, x_ref, ga_ref,
                             gb_ref, lng_ref, lnb_ref, ws_ref, bs_ref, sinks_ref, wa_ref, wb_ref,
                             wo_ref, g_ref, y_ref, a_scr, b_scr):
    i = pl.program_id(0)
    last_tile = pl.num_programs(0) - 2
    tile = jnp.minimum(i, last_tile)
    slot = i % 2

    @pl.when(i == 0)
    def _():
        a_scr[1] = jnp.zeros(a_scr.shape[1:], BF16)
        b_scr[1] = jnp.zeros(b_scr.shape[1:], BF16)

    def attention(blk):
        rows = slice(blk * CHUNK, (blk + 1) * CHUNK)
        first_block = ((tile * MIX_BLOCKS + blk) % (SEQ // WINDOW)) == 0
        kvd_prev = kvdp_ref[...] if blk == 0 else kvd_ref[(blk - 1) * CHUNK:blk * CHUNK, :]

        def store_b(cols, val):
            b_scr[slot, rows, cols] = val

        _attention_block(q_ref[rows, :], zb_ref[rows, :], kvd_ref[rows, :], kvd_prev, first_block,
                         sinks_ref, store_b)

    def spatial(blk):
        rows = slice(blk * CHUNK, (blk + 1) * CHUNK)

        def store_a(cols, val):
            a_scr[slot, rows, cols] = val

        _spatial_block(u_ref[rows, :], v_ref[rows, :], za_ref[rows, :], lng_ref[...], lnb_ref[...],
                       ws_ref, bs_ref, store_a)

    for blk in range(MIX_BLOCKS):
        attention(blk)
    y_ref[...] = _merge(x_ref[...], a_scr[1 - slot], b_scr[1 - slot], ga_ref, gb_ref,
                        wa_ref, wb_ref, wo_ref, g_ref)
    for blk in range(MIX_BLOCKS):
        spatial(blk)


def _prompt_mix_merge(x2d, z, kvd, ln_g, ln_b, w_s, b_s_rows, sinks, w_a, w_b, w_o, g_post):
    n = x2d.shape[0]
    tm = MIX_ROWS
    n_tiles = n // tm
    mix_tile = lambda i: jnp.minimum(i, n_tiles - 1)
    merge_tile = lambda i: jnp.maximum(i - 1, 0)
    wide = lambda cb: pl.BlockSpec((None, tm, Z_BLOCK), lambda i, cb=cb: (cb, mix_tile(i), 0))
    gate = lambda cb: pl.BlockSpec((D_MODEL // Z_BLOCK, tm, Z_BLOCK),
                                   lambda i, cb=cb: (cb // (D_MODEL // Z_BLOCK), merge_tile(i), 0))
    const2 = lambda shape: pl.BlockSpec(shape, lambda i: (0, 0))
    resident = lambda shape: pl.BlockSpec(shape, lambda i: (0, 0), pipeline_mode=pl.Buffered(1))
    return pl.pallas_call(
        _prompt_mix_merge_kernel,
        grid=(n_tiles + 1,),
        in_specs=[
            wide(COL_U), wide(COL_V), wide(COL_ZA), wide(COL_Q), wide(COL_ZB),
            pl.BlockSpec((tm, KVD_WIDTH), lambda i: (mix_tile(i), 0)),
            pl.BlockSpec((CHUNK, KVD_WIDTH), lambda i: (jnp.maximum(mix_tile(i) * MIX_BLOCKS - 1, 0), 0)),
            pl.BlockSpec((tm, D_MODEL), lambda i: (merge_tile(i), 0)),
            gate(COL_GA), gate(COL_GB),
            const2((1, A_WIDTH)), const2((1, A_WIDTH)),
            pl.BlockSpec((A_GROUPS, CHUNK, CHUNK), lambda i: (0, 0, 0)),
            const2((CHUNK, A_WIDTH)),
            pl.BlockSpec(memory_space=pltpu.SMEM),
            resident((A_WIDTH, D_MODEL)),
            resident((B_WIDTH, D_MODEL)),
            resident((D_MODEL, D_MODEL)),
            resident((1, D_MODEL)),
        ],
        out_specs=pl.BlockSpec((tm, D_MODEL), lambda i: (merge_tile(i), 0)),
        out_shape=jax.ShapeDtypeStruct((n, D_MODEL), F32),
        scratch_shapes=[pltpu.VMEM((2, tm, A_WIDTH), BF16), pltpu.VMEM((2, tm, B_WIDTH), BF16)],
        compiler_params=pltpu.CompilerParams(
            dimension_semantics=("arbitrary",), vmem_limit_bytes=VMEM_LIMIT),
        name="prompt_mix_merge",
    )(z, z, z, z, z, kvd, kvd, x2d, z, z, ln_g, ln_b, w_s, b_s_rows, sinks, w_a, w_b, w_o, g_post)


SEQS_PER_STEP = CHUNK // DEC_SEQ


def _sample_mix_merge_kernel(u_ref, v_ref, za_ref, q_ref, zb_ref, kvd_ref, kv32_ref, ckt_ref, cvt_ref,
                             lng_ref, lnb_ref, coef_ref, bs_ref, sinks_ref, x_ref, ga_ref, gb_ref,
                             wa_ref, wb_ref, wo_ref, g_ref,
                             vn_ref, kwin_ref, vwin_ref, y_ref, a_scr, b_scr):
    i = pl.program_id(0)
    slot = i % 2

    @pl.when(i == 0)
    def _():
        a_scr[1] = jnp.zeros(a_scr.shape[1:], BF16)
        b_scr[1] = jnp.zeros(b_scr.shape[1:], BF16)

    ns = SEQS_PER_STEP
    split = lambda x: x.reshape(ns, DEC_SEQ, x.shape[-1])

    vn = _layernorm_v(v_ref, lng_ref, lnb_ref)
    vn_ref[...] = vn
    vn3 = split(vn)
    t_row = lax.broadcasted_iota(jnp.int32, (DEC_SEQ, A_WIDTH), 0)
    s_acc = jnp.broadcast_to(bs_ref[...][None], (ns, DEC_SEQ, A_WIDTH))
    for s in range(DEC_SEQ):
        coef = jnp.where(t_row >= s, coef_ref[s], 0.0)
        s_acc = s_acc + vn3[:, s:s + 1, :] * coef[None]
    a3 = split(u_ref[...].astype(F32)) * s_acc * split(za_ref[...].astype(F32))
    a_scr[slot] = a3.reshape(CHUNK, A_WIDTH).astype(BF16)

    rows_q = Q_PER_KV * DEC_SEQ
    lane3 = lax.broadcasted_iota(jnp.int32, (ns, DEC_SEQ, LANES), 2)
    low3 = lane3 < HEAD_DIM
    r_idx = lax.broadcasted_iota(jnp.int32, (ns, rows_q, LANES), 1)
    l_idx = lax.broadcasted_iota(jnp.int32, (ns, rows_q, LANES), 2)
    s_idx = lax.broadcasted_iota(jnp.int32, (ns, rows_q, LANES), 0)
    t_q = r_idx & (DEC_SEQ - 1)
    mask_cache = l_idx > t_q
    mask_new = ((l_idx >> 3) == s_idx) & ((l_idx & (DEC_SEQ - 1)) <= t_q)
    head_of_row = lax.broadcasted_iota(jnp.int32, (1, rows_q, 1), 1) >> 3
    q3 = split(q_ref[...].astype(F32))
    zb3 = split(zb_ref[...].astype(F32))
    for kh in range(N_KV_HEADS):
        pieces = []
        for m in (2 * kh, 2 * kh + 1):
            q_m = q3[:, :, m * LANES:(m + 1) * LANES]
            pieces += [jnp.where(low3, q_m, 0.0), jnp.where(low3, 0.0, q_m)]
        lhs = jnp.concatenate(pieces, axis=1).astype(BF16)
        rows = slice(kh * HEAD_DIM, (kh + 1) * HEAD_DIM)
        kt = ckt_ref[:, rows, :].astype(BF16)
        vt = cvt_ref[:, rows, :].astype(BF16)
        kt2 = jnp.concatenate([kt, kt], axis=1)
        vt2 = jnp.concatenate([vt, vt], axis=1)
        k_new = kvd_ref[:, kh * LANES:(kh + 1) * LANES]
        v_new = kvd_ref[:, KVD_WIDTH // 2 + kh * LANES:KVD_WIDTH // 2 + (kh + 1) * LANES]
        s_c = lax.dot_general(lhs, kt2, (((2,), (1,)), ((0,), (0,))), preferred_element_type=F32)
        s_n = lax.dot_general(lhs.reshape(ns * rows_q, LANES), k_new, (((1,), (1,)), ((), ())),
                              preferred_element_type=F32).reshape(ns, rows_q, LANES)
        s_c = jnp.where(mask_cache, s_c, -jnp.inf)
        s_n = jnp.where(mask_new, s_n, -jnp.inf)
        sink = jnp.zeros((1, rows_q, 1), F32)
        for jq in range(Q_PER_KV):
            sink = jnp.where(head_of_row == jq, sinks_ref[Q_PER_KV * kh + jq], sink)
        mx = jnp.maximum(jnp.maximum(jnp.max(s_c, axis=-1, keepdims=True),
                                     jnp.max(s_n, axis=-1, keepdims=True)), sink)
        p_c = jnp.exp(s_c - mx)
        p_n = jnp.exp(s_n - mx)
        denom = (jnp.sum(p_c, axis=-1, keepdims=True) + jnp.sum(p_n, axis=-1, keepdims=True)
                 + jnp.exp(sink - mx))
        o = lax.dot_general(p_c.astype(BF16), vt2, (((2,), (2,)), ((0,), (0,))),
                            preferred_element_type=F32)
        o = o + jnp.dot(p_n.reshape(ns * rows_q, LANES).astype(BF16), v_new,
                        preferred_element_type=F32).reshape(ns, rows_q, LANES)
        o = o * (1.0 / denom)
        for i, m in enumerate((2 * kh, 2 * kh + 1)):
            base = 2 * DEC_SEQ * i
            o_m = jnp.where(low3, o[:, base:base + DEC_SEQ, :], o[:, base + DEC_SEQ:base + 2 * DEC_SEQ, :])
            cols = slice(m * LANES, (m + 1) * LANES)
            b_scr[slot, :, cols] = (o_m * zb3[:, :, cols]).reshape(CHUNK, LANES).astype(BF16)

    y_ref[...] = _merge(x_ref[...], a_scr[1 - slot], b_scr[1 - slot], ga_ref, gb_ref,
                        wa_ref, wb_ref, wo_ref, g_ref)

    keep = WINDOW - DEC_SEQ
    lane_w = lax.broadcasted_iota(jnp.int32, (KV_WIDTH, WINDOW), 1)
    for new_rows, cache_ref, out_ref in ((kv32_ref[:, :KV_WIDTH], ckt_ref, kwin_ref),
                                         (kv32_ref[:, KV_WIDTH:], cvt_ref, vwin_ref)):
        new_t = new_rows.T
        old = pltpu.roll(cache_ref[...].reshape(ns * KV_WIDTH, WINDOW), keep, axis=1)
        for s in range(ns):
            shifted_new = pltpu.roll(new_t, (keep - DEC_SEQ * s) % WINDOW, axis=1)
            out_ref[s] = jnp.where(lane_w < keep, old[s * KV_WIDTH:(s + 1) * KV_WIDTH], shifted_new)


def _sample_mix_merge(x2d, z, kvd, kv32, cache_kt, cache_vt, ln_g, ln_b, coef, b_s8, sinks,
                      w_a, w_b, w_o, g_post):
    n = x2d.shape[0]
    n_blocks = n // CHUNK
    n_seq = cache_kt.shape[0]
    mix_blk = lambda i: jnp.minimum(i, n_blocks - 1)
    merge_blk = lambda i: jnp.maximum(i - 1, 0)
    wide = lambda cb: pl.BlockSpec((None, CHUNK, Z_BLOCK), lambda i, cb=cb: (cb, mix_blk(i), 0))
    gate = lambda cb: pl.BlockSpec((D_MODEL // Z_BLOCK, CHUNK, Z_BLOCK),
                                   lambda i, cb=cb: (cb // (D_MODEL // Z_BLOCK), merge_blk(i), 0))
    const2 = lambda shape: pl.BlockSpec(shape, lambda i: (0, 0))
    resident = lambda shape: pl.BlockSpec(shape, lambda i: (0, 0), pipeline_mode=pl.Buffered(1))
    win = pl.BlockSpec((SEQS_PER_STEP, KV_WIDTH, WINDOW), lambda i: (mix_blk(i), 0, 0))
    return pl.pallas_call(
        _sample_mix_merge_kernel,
        grid=(n_blocks + 1,),
        in_specs=[
            wide(COL_U), wide(COL_V), wide(COL_ZA), wide(COL_Q), wide(COL_ZB),
            pl.BlockSpec((CHUNK, KVD_WIDTH), lambda i: (mix_blk(i), 0)),
            pl.BlockSpec((CHUNK, 2 * KV_WIDTH), lambda i: (mix_blk(i), 0)),
            win, win,
            const2((1, A_WIDTH)), const2((1, A_WIDTH)),
            pl.BlockSpec((DEC_SEQ, DEC_SEQ, A_WIDTH), lambda i: (0, 0, 0)),
            const2((DEC_SEQ, A_WIDTH)),
            pl.BlockSpec(memory_space=pltpu.SMEM),
            pl.BlockSpec((CHUNK, D_MODEL), lambda i: (merge_blk(i), 0)),
            gate(COL_GA), gate(COL_GB),
            resident((A_WIDTH, D_MODEL)),
            resident((B_WIDTH, D_MODEL)),
            resident((D_MODEL, D_MODEL)),
            resident((1, D_MODEL)),
        ],
        out_specs=[
            pl.BlockSpec((CHUNK, A_WIDTH), lambda i: (mix_blk(i), 0)),
            win, win,
            pl.BlockSpec((CHUNK, D_MODEL), lambda i: (merge_blk(i), 0)),
        ],
        out_shape=[
            jax.ShapeDtypeStruct((n, A_WIDTH), F32),
            jax.ShapeDtypeStruct((n_seq, KV_WIDTH, WINDOW), F32),
            jax.ShapeDtypeStruct((n_seq, KV_WIDTH, WINDOW), F32),
            jax.ShapeDtypeStruct((n, D_MODEL), F32),
        ],
        scratch_shapes=[pltpu.VMEM((2, CHUNK, A_WIDTH), BF16), pltpu.VMEM((2, CHUNK, B_WIDTH), BF16)],
        compiler_params=pltpu.CompilerParams(
            dimension_semantics=("arbitrary",), vmem_limit_bytes=VMEM_LIMIT),
        name="sample_mix_merge",
    )(z, z, z, z, z, kvd, kv32, cache_kt, cache_vt, ln_g, ln_b, coef, b_s8, sinks,
      x2d, z, z, w_a, w_b, w_o, g_post)


def _rope_tables(pos):
    lane = jnp.arange(LANES)
    inv = ROPE_THETA ** (-(2 * (lane % (HEAD_DIM // 2))).astype(F32) / HEAD_DIM)
    ang = pos.astype(F32)[:, None] * inv[None, :]
    sign = jnp.where((lane % HEAD_DIM) < HEAD_DIM // 2, -1.0, 1.0).astype(F32)
    return jnp.cos(ang), jnp.sin(ang) * sign[None, :]


def _window_first(win):
    n = win.shape[1]
    return jnp.transpose(win[0], (0, 2, 3, 1)).reshape(n, KV_WIDTH, WINDOW)


def _window_last(win_t):
    n = win_t.shape[0]
    return jnp.transpose(win_t.reshape(n, N_KV_HEADS, HEAD_DIM, WINDOW), (0, 3, 1, 2))[None]


def kernel(x_prompt, x_sample, cache_k_win, cache_v_win, g_pre, w_in, ln_v_g, ln_v_b, w_spatial,
           b_spatial, sinks, w_proj_a, w_proj_b, w_out, g_post):
    bsz, seq, _ = x_prompt.shape
    dbsz, dseq, _ = x_sample.shape
    assert seq == SEQ and dseq == DEC_SEQ and seq % IN_PROJ_ROWS == 0
    assert (dbsz * dseq) % IN_PROJ_ROWS == 0 and g_pre.shape[0] == 1

    b_s_rows = jnp.repeat(b_spatial[0].T, LANES, axis=1)
    coef = jnp.repeat(jnp.transpose(w_spatial[0][:, :DEC_SEQ, :DEC_SEQ], (2, 1, 0)), LANES, axis=2)
    cos_p, sin_p = _rope_tables(jnp.arange(SEQ))
    cos_s, sin_s = _rope_tables(PAST_LEN + jnp.arange(dseq))

    xs = x_sample.reshape(dbsz * dseq, D_MODEL)
    (z_s, kvd_s, kv32_s), (w_in_b,) = _in_proj(xs, g_pre, w_in[0], cos_s, sin_s, dseq, 1, 256, 1,
                                               emit_for=PROMPT_TILES_PER_STEP)

    xp = x_prompt.reshape(bsz * seq, D_MODEL)
    (z_p, kvd_p, kv32_p), (w_a, w_b, w_o) = _in_proj(
        xp, g_pre, w_in_b, cos_p, sin_p, IN_PROJ_ROWS, PROMPT_TILES_PER_STEP, 256, 4,
        cast_along=(w_proj_a[0], w_proj_b[0], w_out[0]))
    y_p = _prompt_mix_merge(xp, z_p, kvd_p, ln_v_g, ln_v_b, w_spatial[0], b_s_rows, sinks[0],
                            w_a, w_b, w_o, g_post)
    kv_win_p = kv32_p.reshape(bsz, seq, 2 * KV_WIDTH)[:, seq - WINDOW:]
    k_win_p = kv_win_p[..., :KV_WIDTH].reshape(1, bsz, WINDOW, N_KV_HEADS, HEAD_DIM)
    v_win_p = kv_win_p[..., KV_WIDTH:].reshape(1, bsz, WINDOW, N_KV_HEADS, HEAD_DIM)

    vn_s, k_win_t, v_win_t, y_s = _sample_mix_merge(
        xs, z_s, kvd_s, kv32_s, _window_first(cache_k_win), _window_first(cache_v_win),
        ln_v_g, ln_v_b, coef, b_s_rows[:dseq], sinks[0], w_a, w_b, w_o, g_post)

    return (y_p.reshape(bsz, seq, D_MODEL),
            y_s.reshape(dbsz, dseq, D_MODEL),
            k_win_p, v_win_p,
            _window_last(k_win_t), _window_last(v_win_t),
            vn_s.reshape(1, dbsz, dseq, A_WIDTH))
```

```python
import functools

import numpy as np
import jax
import jax.numpy as jnp
from jax import lax
from jax.experimental import pallas as pl
from jax.experimental.pallas import tpu as pltpu

D_MODEL = 2048
SEQ = 2048
DEC_SEQ = 8
PAST_LEN = 8192
CHUNK = 128
A_WIDTH = 1024
A_GROUPS = 8
HEAD_DIM = 64
N_HEADS = 16
N_KV_HEADS = 4
Q_PER_KV = N_HEADS // N_KV_HEADS
B_WIDTH = N_HEADS * HEAD_DIM
KV_WIDTH = N_KV_HEADS * HEAD_DIM
WINDOW = 128
ROPE_THETA = 10000.0
EPS = 1e-6
IN_COLS = 3 * A_WIDTH + 2 * B_WIDTH + 2 * KV_WIDTH + 2 * D_MODEL

LANES = 128
SUBLANES = 8
BF16 = jnp.bfloat16
F32 = jnp.float32

COL_TILE = 512
N_COL_TILES = IN_COLS // COL_TILE
Z_BLOCK = 2 * COL_TILE
Z_COLS = (N_COL_TILES - 1) * COL_TILE
COL_GA, COL_GB, COL_U, COL_V, COL_ZA, COL_Q, COL_ZB = 0, 2, 4, 5, 6, 7, 8
KVD_WIDTH = 2 * N_KV_HEADS * LANES
EPI_GATE, EPI_GELU, EPI_SILU, EPI_Q, EPI_KV = "gate", "gelu", "silu", "q", "kv"
KV_TILE = 8


def _walk(tiles_per_step):
    sections = [((11, 12, 13, 14, 15, 16, 17, 18), EPI_GATE), ((0, 1, 2, 3), EPI_GELU),
                ((4, 5), EPI_SILU), ((6, 7), EPI_Q), ((9, 10), EPI_SILU)]
    slots = [(t, e) for tiles, e in sections for t in tiles]
    steps = []
    for s, k in enumerate(range(0, len(slots), tiles_per_step)):
        chunk = slots[k:k + tiles_per_step]
        steps.append(([t for t, _ in chunk], [e for _, e in chunk], s))
    tiles, epis, _ = steps[-1]
    if len(tiles) < tiles_per_step:
        pad = tiles_per_step - len(tiles)
        tiles += [KV_TILE] * pad
        epis += [EPI_KV] + [None] * (pad - 1)
    else:
        mid = len(steps) // 2
        steps.insert(mid, ([KV_TILE] * tiles_per_step, [EPI_KV] + [None] * (tiles_per_step - 1),
                           steps[mid - 1][2]))
    return [(tuple(t), tuple(e), z) for t, e, z in steps]


def _lookup(values, j):
    out = jnp.int32(values[-1])
    for k in range(len(values) - 2, -1, -1):
        out = jnp.where(j == k, jnp.int32(values[k]), out)
    return out


IN_PROJ_ROWS = 1024
VMEM_LIMIT = 56 * 1024 * 1024
VMEM_LIMIT_IN_PROJ = 62 * 1024 * 1024


def _gelu(x):
    return 0.5 * x * (1.0 + lax.erf(x * np.float32(1.0 / np.sqrt(2.0))))


def _sigmoid(x):
    return 1.0 / (1.0 + jnp.exp(-x))


def _rope(x, cos, sin_signed):
    width = x.shape[1]
    lane = lax.broadcasted_iota(jnp.int32, x.shape, 1)
    first_half = (lane & (HEAD_DIM - 1)) < (HEAD_DIM // 2)
    partner = jnp.where(first_half,
                        pltpu.roll(x, width - HEAD_DIM // 2, axis=1),
                        pltpu.roll(x, HEAD_DIM // 2, axis=1))
    reps = width // LANES
    cos_w = jnp.concatenate([cos] * reps, axis=1) if reps > 1 else cos
    sin_w = jnp.concatenate([sin_signed] * reps, axis=1) if reps > 1 else sin_signed
    return x * cos_w + partner * sin_w


def _dup_heads(x):
    lane = lax.broadcasted_iota(jnp.int32, (x.shape[0], LANES), 1)
    low = lane < HEAD_DIM
    out = []
    for c in range(x.shape[1] // LANES):
        xc = x[:, c * LANES:(c + 1) * LANES]
        sw = pltpu.roll(xc, HEAD_DIM, axis=1)
        out += [jnp.where(low, xc, sw), jnp.where(low, sw, xc)]
    return jnp.concatenate(out, axis=1)


def _in_proj_kernel(*refs, table_rows, sub, walk, x_parts, n_cast, emit_bf16, windows_only):
    n_w = len(walk[0][0])
    refs = list(refs)
    take = lambda k: [refs.pop(0) for _ in range(k)]
    x_refs = take(x_parts)
    (g_ref,) = take(1)
    w_refs = take(n_w)
    cos_ref, sin_ref = take(2)
    cast_in = take(n_cast)
    z_ref, kvd_ref = take(2)
    kv32_refs = take(2 if windows_only else 1)
    cast_out = take(n_cast)
    wb_refs = take(n_w if emit_bf16 else 0)
    (h_scr,) = refs
    j = pl.program_id(1)
    part_rows = x_refs[0].shape[0]
    tm = part_rows * x_parts

    def tables(rs):
        if table_rows == tm:
            return cos_ref[rs, :], sin_ref[rs, :]
        tile = lambda t: jnp.broadcast_to(t[None], (sub // table_rows, table_rows, LANES)).reshape(sub, LANES)
        return tile(cos_ref[...]), tile(sin_ref[...])

    def z_store(t, rs, val):
        col = t * COL_TILE
        if len(z_ref.shape) == 3:
            z_ref[col // Z_BLOCK, rs, col % Z_BLOCK:col % Z_BLOCK + COL_TILE] = val
        else:
            z_ref[rs, col:col + COL_TILE] = val

    def kv_store(acc, rs):
        cos, sin = tables(rs)
        k = _rope(acc[:, :KV_WIDTH], cos, sin)
        v = acc[:, KV_WIDTH:]
        if windows_only:
            lo = tm - WINDOW - rs.start
            if 0 <= lo and lo + WINDOW <= sub:
                kv32_refs[0][...] = k[lo:lo + WINDOW, :].T
                kv32_refs[1][...] = v[lo:lo + WINDOW, :].T
        else:
            kv32_refs[0][rs, :KV_WIDTH] = k
            kv32_refs[0][rs, KV_WIDTH:] = v
        kvd_ref[rs, :KVD_WIDTH // 2] = _dup_heads(k).astype(BF16)
        kvd_ref[rs, KVD_WIDTH // 2:] = _dup_heads(v).astype(BF16)

    z_value = {
        EPI_GATE: lambda acc, rs: _sigmoid(acc),
        EPI_GELU: lambda acc, rs: _gelu(acc),
        EPI_SILU: lambda acc, rs: acc * _sigmoid(acc),
        EPI_Q: lambda acc, rs: _rope(acc, *tables(rs)) * np.float32(HEAD_DIM ** -0.5),
    }

    def run(epis, norm):
        if emit_bf16:
            for t, epi in enumerate(epis):
                if epi is not None:
                    wb_refs[t][...] = w_refs[t][...].astype(BF16)
            weight = lambda t: wb_refs[t][...]
        else:
            weight = lambda t: w_refs[t][...]
        stores_z = any(epi in z_value for epi in epis)
        for r in range(tm // sub):
            rs = slice(r * sub, (r + 1) * sub)
            if norm:
                start = (r * sub) % part_rows
                x = x_refs[(r * sub) // part_rows][start:start + sub, :]
                ms = jnp.mean(x * x, axis=-1, keepdims=True)
                h = (x * lax.rsqrt(ms + EPS) * g_ref[...]).astype(BF16)
                h_scr[rs, :] = h
            else:
                h = h_scr[rs, :]
            for t, epi in enumerate(epis):
                if epi in z_value:
                    acc = jnp.dot(h, weight(t), preferred_element_type=F32)
                    z_store(t, rs, z_value[epi](acc, rs).astype(BF16))
                else:
                    if epi == EPI_KV:
                        kv_store(jnp.dot(h, weight(t), preferred_element_type=F32), rs)
                    if stores_z:
                        z_store(t, rs, jnp.zeros((sub, COL_TILE), BF16))
        if EPI_KV in epis:
            for src, dst in zip(cast_in, cast_out):
                dst[...] = src[...].astype(BF16)

    kinds = []
    for s, (_, epis, _) in enumerate(walk):
        if (epis, s == 0) not in kinds:
            kinds.append((epis, s == 0))
    kind = _lookup(tuple(kinds.index((epis, s == 0)) for s, (_, epis, _) in enumerate(walk)), j)
    for k, (epis, norm) in enumerate(kinds):
        pl.when(kind == k)(functools.partial(run, epis, norm))


def _in_proj(x2d, g_pre, w_in, cos_t, sin_t, table_rows, tiles_per_step, sub, x_parts, cast_along=(),
             emit_bf16=False, seq_rows=None):
    n = x2d.shape[0]
    tm = IN_PROJ_ROWS
    walk = _walk(tiles_per_step)
    n_steps = len(walk)
    if table_rows == tm:
        n_tab = cos_t.shape[0] // tm
        table_spec = pl.BlockSpec((tm, LANES), lambda i, j: (i % n_tab, 0))
    else:
        table_spec = pl.BlockSpec((table_rows, LANES), lambda i, j: (0, 0))
    w_tile = lambda j, t: _lookup(tuple(tiles[t] for tiles, _, _ in walk), j)
    tile_specs = [pl.BlockSpec((None, D_MODEL, COL_TILE), lambda i, j, t=t: (w_tile(j, t), 0, 0))
                  for t in range(tiles_per_step)]
    if emit_bf16:
        assert tiles_per_step == 1 and n == tm
        w_specs = [pl.BlockSpec((D_MODEL, COL_TILE), lambda i, j: (0, w_tile(j, 0)))]
        wb_specs, wb_shapes = tile_specs, [jax.ShapeDtypeStruct((N_COL_TILES, D_MODEL, COL_TILE), BF16)]
    else:
        w_specs, wb_specs, wb_shapes = tile_specs, [], []
    z_step = lambda j: _lookup(tuple(z for _, _, z in walk), j)
    n_z_steps = walk[-1][2] + 1
    step_cols = tiles_per_step * COL_TILE
    if step_cols <= Z_BLOCK:
        steps_per_block = Z_BLOCK // step_cols
        z_blocks = n_z_steps // steps_per_block
        z_spec = pl.BlockSpec((None, tm, step_cols),
                              lambda i, j: (z_step(j) // steps_per_block, i, z_step(j) % steps_per_block))
    else:
        z_blocks = n_z_steps * (step_cols // Z_BLOCK)
        z_spec = pl.BlockSpec((step_cols // Z_BLOCK, tm, Z_BLOCK), lambda i, j: (z_step(j), i, 0))
    assert z_blocks * Z_BLOCK >= Z_COLS

    n_row_tiles = n // tm
    assert sub <= tm // x_parts and x_parts < n_steps

    def x_index(i, j, p):
        tile = jnp.minimum(i + (j >= n_steps - p).astype(jnp.int32), n_row_tiles - 1)
        return (tile * x_parts + p, 0)

    x_specs = [pl.BlockSpec((tm // x_parts, D_MODEL), functools.partial(x_index, p=p))
               for p in range(x_parts)]
    cast_specs = [pl.BlockSpec((w.shape[0] // n_row_tiles, w.shape[1]), lambda i, j: (i, 0)) for w in cast_along]
    if seq_rows is None:
        kv32_specs = [pl.BlockSpec((tm, 2 * KV_WIDTH), lambda i, j: (i, 0))]
        kv32_shapes = [jax.ShapeDtypeStruct((n, 2 * KV_WIDTH), F32)]
    else:
        assert seq_rows % tm == 0 and (tm - WINDOW) // sub == (tm - 1) // sub
        kv32_specs = [pl.BlockSpec((None, KV_WIDTH, WINDOW), lambda i, j: (i // (seq_rows // tm), 0, 0))] * 2
        kv32_shapes = [jax.ShapeDtypeStruct((n // seq_rows, KV_WIDTH, WINDOW), F32)] * 2
    n_main = 2 + len(kv32_specs)
    outs = pl.pallas_call(
        functools.partial(_in_proj_kernel, table_rows=table_rows, sub=sub, walk=walk, x_parts=x_parts,
                          n_cast=len(cast_along), emit_bf16=emit_bf16, windows_only=seq_rows is not None),
        grid=(n_row_tiles, n_steps),
        in_specs=[
            *x_specs,
            pl.BlockSpec((1, D_MODEL), lambda i, j: (0, 0)),
            *w_specs,
            table_spec, table_spec,
            *cast_specs,
        ],
        out_specs=[
            z_spec,
            pl.BlockSpec((tm, KVD_WIDTH), lambda i, j: (i, 0)),
            *kv32_specs,
            *cast_specs,
            *wb_specs,
        ],
        out_shape=[
            jax.ShapeDtypeStruct((z_blocks, n, Z_BLOCK), BF16),
            jax.ShapeDtypeStruct((n, KVD_WIDTH), BF16),
            *kv32_shapes,
            *[jax.ShapeDtypeStruct(w.shape, BF16) for w in cast_along],
            *wb_shapes,
        ],
        scratch_shapes=[pltpu.VMEM((tm, D_MODEL), BF16)],
        compiler_params=pltpu.CompilerParams(
            dimension_semantics=("arbitrary", "arbitrary"), vmem_limit_bytes=VMEM_LIMIT_IN_PROJ),
        name="in_proj",
    )(*([x2d] * x_parts), g_pre, *([w_in] * tiles_per_step), cos_t, sin_t, *cast_along)
    return outs[:n_main], outs[n_main:]


def _layernorm_v(v_ref, lng_ref, lnb_ref):
    vg = v_ref[...].astype(F32)
    mu = jnp.mean(vg, axis=-1, keepdims=True)
    xc = vg - mu
    var = jnp.mean(xc * xc, axis=-1, keepdims=True)
    return xc * lax.rsqrt(var + EPS) * lng_ref[...] + lnb_ref[...]


def _spatial_block(u, v, za, lng, lnb, ws_ref, bs_ref, store_a):
    row = lax.broadcasted_iota(jnp.int32, (CHUNK, CHUNK), 0)
    col = lax.broadcasted_iota(jnp.int32, (CHUNK, CHUNK), 1)
    vg = v.astype(F32)
    mu = jnp.mean(vg, axis=-1, keepdims=True)
    xc = vg - mu
    var = jnp.mean(xc * xc, axis=-1, keepdims=True)
    vn_b = (xc * lax.rsqrt(var + EPS) * lng + lnb).astype(BF16)
    for g in range(A_GROUPS):
        cols = slice(g * LANES, (g + 1) * LANES)
        w = jnp.where(col <= row, ws_ref[g], 0.0).astype(BF16)
        s = jnp.dot(w, vn_b[:, cols], preferred_element_type=F32) + bs_ref[:, cols]
        store_a(cols, ((u[:, cols].astype(F32) * s) * za[:, cols].astype(F32)).astype(BF16))


def _attention_block(q, zb, kvd, kvd_prev, first_block, sinks_ref, store_b):
    t = lax.broadcasted_iota(jnp.int32, (WINDOW, 2 * WINDOW), 0)
    jj = lax.broadcasted_iota(jnp.int32, (WINDOW, 2 * WINDOW), 1)
    mask = (jj > t) & (jj <= t + WINDOW) & ((jj >= WINDOW) | jnp.logical_not(first_block))
    lane = lax.broadcasted_iota(jnp.int32, (WINDOW, LANES), 1)
    low_half = lane < HEAD_DIM
    for m in range(N_HEADS // 2):
        cols = slice(m * LANES, (m + 1) * LANES)
        kcols = slice((m // 2) * LANES, (m // 2 + 1) * LANES)
        vcols = slice(KVD_WIDTH // 2 + (m // 2) * LANES, KVD_WIDTH // 2 + (m // 2 + 1) * LANES)
        k_c = jnp.concatenate([kvd_prev[:, kcols], kvd[:, kcols]], axis=0)
        v_c = jnp.concatenate([kvd_prev[:, vcols], kvd[:, vcols]], axis=0)
        q_m = q[:, cols]
        outs = []
        for half in range(2):
            sink = sinks_ref[2 * m + half]
            q_h = jnp.where(low_half if half == 0 else ~low_half, q_m, jnp.zeros_like(q_m))
            s = lax.dot_general(q_h, k_c, (((1,), (1,)), ((), ())), preferred_element_type=F32)
            s = jnp.where(mask, s, -jnp.inf)
            mx = jnp.maximum(jnp.max(s, axis=-1, keepdims=True), sink)
            p = jnp.exp(s - mx)
            denom = jnp.sum(p, axis=-1, keepdims=True) + jnp.exp(sink - mx)
            o = jnp.dot(p.astype(BF16), v_c, preferred_element_type=F32)
            outs.append(o * (1.0 / denom))
        o_m = jnp.where(low_half, outs[0], outs[1])
        store_b(cols, (o_m * zb[:, cols].astype(F32)).astype(BF16))


def _merge(x, a, b, ga_ref, gb_ref, wa_ref, wb_ref, wo_ref, g_ref):
    gate = lambda ref: jnp.concatenate([ref[c] for c in range(ref.shape[0])], axis=1).astype(F32)
    pa = jnp.dot(a, wa_ref[...], preferred_element_type=F32)
    pb = jnp.dot(b, wb_ref[...], preferred_element_type=F32)
    merged = gate(ga_ref) * pa + gate(gb_ref) * pb
    out = jnp.dot(merged.astype(BF16), wo_ref[...], preferred_element_type=F32)
    ms = jnp.mean(out * out, axis=-1, keepdims=True)
    return x + out * lax.rsqrt(ms + EPS) * g_ref[...]


MIX_ROWS = 256
MIX_BLOCKS = MIX_ROWS // CHUNK


def _prompt_mix_merge_kernel(u_ref, v_ref, za_ref, q_ref, zb_ref, kvd_ref, kvdp_ref, x_ref, ga_ref,
                             gb_ref, lng_ref, lnb_ref, ws_ref, bs_ref, sinks_ref, wa_ref, wb_ref,
                             wo_ref, g_ref, y_ref, a_scr, b_scr):
    i = pl.program_id(0)
    last_tile = pl.num_programs(0) - 2
    tile = jnp.minimum(i, last_tile)
    slot = i % 2

    @pl.when(i == 0)
    def _():
        a_scr[1] = jnp.zeros(a_scr.shape[1:], BF16)
        b_scr[1] = jnp.zeros(b_scr.shape[1:], BF16)

    def attention(blk):
        rows = slice(blk * CHUNK, (blk + 1) * CHUNK)
        first_block = ((tile * MIX_BLOCKS + blk) % (SEQ // WINDOW)) == 0
        kvd_prev = kvdp_ref[...] if blk == 0 else kvd_ref[(blk - 1) * CHUNK:blk * CHUNK, :]

        def store_b(cols, val):
            b_scr[slot, rows, cols] = val

        _attention_block(q_ref[rows, :], zb_ref[rows, :], kvd_ref[rows, :], kvd_prev, first_block,
                         sinks_ref, store_b)

    def spatial(blk):
        rows = slice(blk * CHUNK, (blk + 1) * CHUNK)

        def store_a(cols, val):
            a_scr[slot, rows, cols] = val

        _spatial_block(u_ref[rows, :], v_ref[rows, :], za_ref[rows, :], lng_ref[...], lnb_ref[...],
                       ws_ref, bs_ref, store_a)

    for blk in range(MIX_BLOCKS):
        attention(blk)
    y_ref[...] = _merge(x_ref[...], a_scr[1 - slot], b_scr[1 - slot], ga_ref, gb_ref,
                        wa_ref, wb_ref, wo_ref, g_ref)
    for blk in range(MIX_BLOCKS):
        spatial(blk)


def _prompt_mix_merge(x2d, z, kvd, ln_g, ln_b, w_s, b_s_rows, sinks, w_a, w_b, w_o, g_post):
    n = x2d.shape[0]
    tm = MIX_ROWS
    n_tiles = n // tm
    mix_tile = lambda i: jnp.minimum(i, n_tiles - 1)
    merge_tile = lambda i: jnp.maximum(i - 1, 0)
    wide = lambda cb: pl.BlockSpec((None, tm, Z_BLOCK), lambda i, cb=cb: (cb, mix_tile(i), 0))
    gate = lambda cb: pl.BlockSpec((D_MODEL // Z_BLOCK, tm, Z_BLOCK),
                                   lambda i, cb=cb: (cb // (D_MODEL // Z_BLOCK), merge_tile(i), 0))
    const2 = lambda shape: pl.BlockSpec(shape, lambda i: (0, 0))
    resident = lambda shape: pl.BlockSpec(shape, lambda i: (0, 0), pipeline_mode=pl.Buffered(1))
    return pl.pallas_call(
        _prompt_mix_merge_kernel,
        grid=(n_tiles + 1,),
        in_specs=[
            wide(COL_U), wide(COL_V), wide(COL_ZA), wide(COL_Q), wide(COL_ZB),
            pl.BlockSpec((tm, KVD_WIDTH), lambda i: (mix_tile(i), 0)),
            pl.BlockSpec((CHUNK, KVD_WIDTH), lambda i: (jnp.maximum(mix_tile(i) * MIX_BLOCKS - 1, 0), 0)),
            pl.BlockSpec((tm, D_MODEL), lambda i: (merge_tile(i), 0)),
            gate(COL_GA), gate(COL_GB),
            const2((1, A_WIDTH)), const2((1, A_WIDTH)),
            pl.BlockSpec((A_GROUPS, CHUNK, CHUNK), lambda i: (0, 0, 0)),
            const2((CHUNK, A_WIDTH)),
            pl.BlockSpec(memory_space=pltpu.SMEM),
            resident((A_WIDTH, D_MODEL)),
            resident((B_WIDTH, D_MODEL)),
            resident((D_MODEL, D_MODEL)),
            resident((1, D_MODEL)),
        ],
        out_specs=pl.BlockSpec((tm, D_MODEL), lambda i: (merge_tile(i), 0)),
        out_shape=jax.ShapeDtypeStruct((n, D_MODEL), F32),
        scratch_shapes=[pltpu.VMEM((2, tm, A_WIDTH), BF16), pltpu.VMEM((2, tm, B_WIDTH), BF16)],
        compiler_params=pltpu.CompilerParams(
            dimension_semantics=("arbitrary",), vmem_limit_bytes=VMEM_LIMIT),
        name="prompt_mix_merge",
    )(z, z, z, z, z, kvd, kvd, x2d, z, z, ln_g, ln_b, w_s, b_s_rows, sinks, w_a, w_b, w_o, g_post)


SEQS_PER_STEP = CHUNK // DEC_SEQ


def _sample_mix_merge_kernel(u_ref, v_ref, za_ref, q_ref, zb_ref, kvd_ref, kv32_ref, ckt_ref, cvt_ref,
                             lng_ref, lnb_ref, coef_ref, bs_ref, sinks_ref, x_ref, ga_ref, gb_ref,
                             wa_ref, wb_ref, wo_ref, g_ref,
                             vn_ref, kwin_ref, vwin_ref, y_ref, a_scr, b_scr):
    i = pl.program_id(0)
    slot = i % 2

    @pl.when(i == 0)
    def _():
        a_scr[1] = jnp.zeros(a_scr.shape[1:], BF16)
        b_scr[1] = jnp.zeros(b_scr.shape[1:], BF16)

    ns = SEQS_PER_STEP
    split = lambda x: x.reshape(ns, DEC_SEQ, x.shape[-1])

    vn = _layernorm_v(v_ref, lng_ref, lnb_ref)
    vn_ref[...] = vn
    vn3 = split(vn)
    t_row = lax.broadcasted_iota(jnp.int32, (DEC_SEQ, A_WIDTH), 0)
    s_acc = jnp.broadcast_to(bs_ref[...][None], (ns, DEC_SEQ, A_WIDTH))
    for s in range(DEC_SEQ):
        coef = jnp.where(t_row >= s, coef_ref[s], 0.0)
        s_acc = s_acc + vn3[:, s:s + 1, :] * coef[None]
    a3 = split(u_ref[...].astype(F32)) * s_acc * split(za_ref[...].astype(F32))
    a_scr[slot] = a3.reshape(CHUNK, A_WIDTH).astype(BF16)

    rows_q = Q_PER_KV * DEC_SEQ
    lane3 = lax.broadcasted_iota(jnp.int32, (ns, DEC_SEQ, LANES), 2)
    low3 = lane3 < HEAD_DIM
    r_idx = lax.broadcasted_iota(jnp.int32, (ns, rows_q, LANES), 1)
    l_idx = lax.broadcasted_iota(jnp.int32, (ns, rows_q, LANES), 2)
    s_idx = lax.broadcasted_iota(jnp.int32, (ns, rows_q, LANES), 0)
    t_q = r_idx & (DEC_SEQ - 1)
    mask_cache = l_idx > t_q
    mask_new = ((l_idx >> 3) == s_idx) & ((l_idx & (DEC_SEQ - 1)) <= t_q)
    head_of_row = lax.broadcasted_iota(jnp.int32, (1, rows_q, 1), 1) >> 3
    q3 = split(q_ref[...].astype(F32))
    zb3 = split(zb_ref[...].astype(F32))
    for kh in range(N_KV_HEADS):
        pieces = []
        for m in (2 * kh, 2 * kh + 1):
            q_m = q3[:, :, m * LANES:(m + 1) * LANES]
            pieces += [jnp.where(low3, q_m, 0.0), jnp.where(low3, 0.0, q_m)]
        lhs = jnp.concatenate(pieces, axis=1).astype(BF16)
        rows = slice(kh * HEAD_DIM, (kh + 1) * HEAD_DIM)
        kt = ckt_ref[:, rows, :].astype(BF16)
        vt = cvt_ref[:, rows, :].astype(BF16)
        kt2 = jnp.concatenate([kt, kt], axis=1)
        vt2 = jnp.concatenate([vt, vt], axis=1)
        k_new = kvd_ref[:, kh * LANES:(kh + 1) * LANES]
        v_new = kvd_ref[:, KVD_WIDTH // 2 + kh * LANES:KVD_WIDTH // 2 + (kh + 1) * LANES]
        s_c = lax.dot_general(lhs, kt2, (((2,), (1,)), ((0,), (0,))), preferred_element_type=F32)
        s_n = lax.dot_general(lhs.reshape(ns * rows_q, LANES), k_new, (((1,), (1,)), ((), ())),
                              preferred_element_type=F32).reshape(ns, rows_q, LANES)
        s_c = jnp.where(mask_cache, s_c, -jnp.inf)
        s_n = jnp.where(mask_new, s_n, -jnp.inf)
        sink = jnp.zeros((1, rows_q, 1), F32)
        for jq in range(Q_PER_KV):
            sink = jnp.where(head_of_row == jq, sinks_ref[Q_PER_KV * kh + jq], sink)
        mx = jnp.maximum(jnp.maximum(jnp.max(s_c, axis=-1, keepdims=True),
                                     jnp.max(s_n, axis=-1, keepdims=True)), sink)
        p_c = jnp.exp(s_c - mx)
        p_n = jnp.exp(s_n - mx)
        denom = (jnp.sum(p_c, axis=-1, keepdims=True) + jnp.sum(p_n, axis=-1, keepdims=True)
                 + jnp.exp(sink - mx))
        o = lax.dot_general(p_c.astype(BF16), vt2, (((2,), (2,)), ((0,), (0,))),
                            preferred_element_type=F32)
        o = o + jnp.dot(p_n.reshape(ns * rows_q, LANES).astype(BF16), v_new,
                        preferred_element_type=F32).reshape(ns, rows_q, LANES)
        o = o * (1.0 / denom)
        for i, m in enumerate((2 * kh, 2 * kh + 1)):
            base = 2 * DEC_SEQ * i
            o_m = jnp.where(low3, o[:, base:base + DEC_SEQ, :], o[:, base + DEC_SEQ:base + 2 * DEC_SEQ, :])
            cols = slice(m * LANES, (m + 1) * LANES)
            b_scr[slot, :, cols] = (o_m * zb3[:, :, cols]).reshape(CHUNK, LANES).astype(BF16)

    y_ref[...] = _merge(x_ref[...], a_scr[1 - slot], b_scr[1 - slot], ga_ref, gb_ref,
                        wa_ref, wb_ref, wo_ref, g_ref)

    keep = WINDOW - DEC_SEQ
    lane_w = lax.broadcasted_iota(jnp.int32, (KV_WIDTH, WINDOW), 1)
    for new_rows, cache_ref, out_ref in ((kv32_ref[:, :KV_WIDTH], ckt_ref, kwin_ref),
                                         (kv32_ref[:, KV_WIDTH:], cvt_ref, vwin_ref)):
        new_t = new_rows.T
        old = pltpu.roll(cache_ref[...].reshape(ns * KV_WIDTH, WINDOW), keep, axis=1)
        for s in range(ns):
            shifted_new = pltpu.roll(new_t, (keep - DEC_SEQ * s) % WINDOW, axis=1)
            out_ref[s] = jnp.where(lane_w < keep, old[s * KV_WIDTH:(s + 1) * KV_WIDTH], shifted_new)


def _sample_mix_merge(x2d, z, kvd, kv32, cache_kt, cache_vt, ln_g, ln_b, coef, b_s8, sinks,
                      w_a, w_b, w_o, g_post):
    n = x2d.shape[0]
    n_blocks = n // CHUNK
    n_seq = cache_kt.shape[0]
    mix_blk = lambda i: jnp.minimum(i, n_blocks - 1)
    merge_blk = lambda i: jnp.maximum(i - 1, 0)
    wide = lambda cb: pl.BlockSpec((None, CHUNK, Z_BLOCK), lambda i, cb=cb: (cb, mix_blk(i), 0))
    gate = lambda cb: pl.BlockSpec((D_MODEL // Z_BLOCK, CHUNK, Z_BLOCK),
                                   lambda i, cb=cb: (cb // (D_MODEL // Z_BLOCK), merge_blk(i), 0))
    const2 = lambda shape: pl.BlockSpec(shape, lambda i: (0, 0))
    resident = lambda shape: pl.BlockSpec(shape, lambda i: (0, 0), pipeline_mode=pl.Buffered(1))
    win = pl.BlockSpec((SEQS_PER_STEP, KV_WIDTH, WINDOW), lambda i: (mix_blk(i), 0, 0))
    return pl.pallas_call(
        _sample_mix_merge_kernel,
        grid=(n_blocks + 1,),
        in_specs=[
            wide(COL_U), wide(COL_V), wide(COL_ZA), wide(COL_Q), wide(COL_ZB),
            pl.BlockSpec((CHUNK, KVD_WIDTH), lambda i: (mix_blk(i), 0)),
            pl.BlockSpec((CHUNK, 2 * KV_WIDTH), lambda i: (mix_blk(i), 0)),
            win, win,
            const2((1, A_WIDTH)), const2((1, A_WIDTH)),
            pl.BlockSpec((DEC_SEQ, DEC_SEQ, A_WIDTH), lambda i: (0, 0, 0)),
            const2((DEC_SEQ, A_WIDTH)),
            pl.BlockSpec(memory_space=pltpu.SMEM),
            pl.BlockSpec((CHUNK, D_MODEL), lambda i: (merge_blk(i), 0)),
            gate(COL_GA), gate(COL_GB),
            resident((A_WIDTH, D_MODEL)),
            resident((B_WIDTH, D_MODEL)),
            resident((D_MODEL, D_MODEL)),
            resident((1, D_MODEL)),
        ],
        out_specs=[
            pl.BlockSpec((CHUNK, A_WIDTH), lambda i: (mix_blk(i), 0)),
            win, win,
            pl.BlockSpec((CHUNK, D_MODEL), lambda i: (merge_blk(i), 0)),
        ],
        out_shape=[
            jax.ShapeDtypeStruct((n, A_WIDTH), F32),
            jax.ShapeDtypeStruct((n_seq, KV_WIDTH, WINDOW), F32),
            jax.ShapeDtypeStruct((n_seq, KV_WIDTH, WINDOW), F32),
            jax.ShapeDtypeStruct((n, D_MODEL), F32),
        ],
        scratch_shapes=[pltpu.VMEM((2, CHUNK, A_WIDTH), BF16), pltpu.VMEM((2, CHUNK, B_WIDTH), BF16)],
        compiler_params=pltpu.CompilerParams(
            dimension_semantics=("arbitrary",), vmem_limit_bytes=VMEM_LIMIT),
        name="sample_mix_merge",
    )(z, z, z, z, z, kvd, kv32, cache_kt, cache_vt, ln_g, ln_b, coef, b_s8, sinks,
      x2d, z, z, w_a, w_b, w_o, g_post)


def _rope_tables(pos):
    lane = jnp.arange(LANES)
    inv = ROPE_THETA ** (-(2 * (lane % (HEAD_DIM // 2))).astype(F32) / HEAD_DIM)
    ang = pos.astype(F32)[:, None] * inv[None, :]
    sign = jnp.where((lane % HEAD_DIM) < HEAD_DIM // 2, -1.0, 1.0).astype(F32)
    return jnp.cos(ang), jnp.sin(ang) * sign[None, :]


def _window_first(win):
    n = win.shape[1]
    return jnp.transpose(win[0], (0, 2, 3, 1)).reshape(n, KV_WIDTH, WINDOW)


def _window_last(win_t):
    n = win_t.shape[0]
    return jnp.transpose(win_t.reshape(n, N_KV_HEADS, HEAD_DIM, WINDOW), (0, 3, 1, 2))[None]


def kernel(x_prompt, x_sample, cache_k_win, cache_v_win, g_pre, w_in, ln_v_g, ln_v_b, w_spatial,
           b_spatial, sinks, w_proj_a, w_proj_b, w_out, g_post):
    bsz, seq, _ = x_prompt.shape
    dbsz, dseq, _ = x_sample.shape
    assert seq == SEQ and dseq == DEC_SEQ and seq % IN_PROJ_ROWS == 0
    assert (dbsz * dseq) % IN_PROJ_ROWS == 0 and g_pre.shape[0] == 1

    b_s_rows = jnp.repeat(b_spatial[0].T, LANES, axis=1)
    coef = jnp.repeat(jnp.transpose(w_spatial[0][:, :DEC_SEQ, :DEC_SEQ], (2, 1, 0)), LANES, axis=2)
    cos_p, sin_p = _rope_tables(jnp.arange(SEQ))
    cos_s, sin_s = _rope_tables(PAST_LEN + jnp.arange(dseq))

    xs = x_sample.reshape(dbsz * dseq, D_MODEL)
    (z_s, kvd_s, kv32_s), (w_in_b,) = _in_proj(xs, g_pre, w_in[0], cos_s, sin_s, dseq, 1, 256, 1,
                                               emit_bf16=True)

    xp = x_prompt.reshape(bsz * seq, D_MODEL)
    (z_p, kvd_p, k_win_pt, v_win_pt), (w_a, w_b, w_o) = _in_proj(
        xp, g_pre, w_in_b, cos_p, sin_p, IN_PROJ_ROWS, 4, 256, 4, seq_rows=seq,
        cast_along=(w_proj_a[0], w_proj_b[0], w_out[0]))
    y_p = _prompt_mix_merge(xp, z_p, kvd_p, ln_v_g, ln_v_b, w_spatial[0], b_s_rows, sinks[0],
                            w_a, w_b, w_o, g_post)

    vn_s, k_win_t, v_win_t, y_s = _sample_mix_merge(
        xs, z_s, kvd_s, kv32_s, _window_first(cache_k_win), _window_first(cache_v_win),
        ln_v_g, ln_v_b, coef, b_s_rows[:dseq], sinks[0], w_a, w_b, w_o, g_post)

    return (y_p.reshape(bsz, seq, D_MODEL),
            y_s.reshape(dbsz, dseq, D_MODEL),
            _window_last(k_win_pt), _window_last(v_win_pt),
            _window_last(k_win_t), _window_last(v_win_t),
            vn_s.reshape(1, dbsz, dseq, A_WIDTH))
```

```python
import functools

import numpy as np
import jax
import jax.numpy as jnp
from jax import lax
from jax.experimental import pallas as pl
from jax.experimental.pallas import tpu as pltpu

D_MODEL = 2048
SEQ = 2048
DEC_SEQ = 8
PAST_LEN = 8192
CHUNK = 128
A_WIDTH = 1024
A_GROUPS = 8
HEAD_DIM = 64
N_HEADS = 16
N_KV_HEADS = 4
Q_PER_KV = N_HEADS // N_KV_HEADS
B_WIDTH = N_HEADS * HEAD_DIM
KV_WIDTH = N_KV_HEADS * HEAD_DIM
WINDOW = 128
ROPE_THETA = 10000.0
EPS = 1e-6
IN_COLS = 3 * A_WIDTH + 2 * B_WIDTH + 2 * KV_WIDTH + 2 * D_MODEL

LANES = 128
SUBLANES = 8
BF16 = jnp.bfloat16
F32 = jnp.float32

COL_TILE = 512
N_COL_TILES = IN_COLS // COL_TILE
Z_BLOCK = 2 * COL_TILE
Z_COLS = (N_COL_TILES - 1) * COL_TILE
COL_GA, COL_GB, COL_U, COL_V, COL_ZA, COL_Q, COL_ZB = 0, 2, 4, 5, 6, 7, 8
KVD_WIDTH = 2 * N_KV_HEADS * LANES
EPI_GATE, EPI_GELU, EPI_SILU, EPI_Q, EPI_KV = "gate", "gelu", "silu", "q", "kv"
KV_TILE = 8


def _walk(tiles_per_step):
    sections = [((11, 12, 13, 14, 15, 16, 17, 18), EPI_GATE), ((0, 1, 2, 3), EPI_GELU),
                ((4, 5), EPI_SILU), ((6, 7), EPI_Q), ((9, 10), EPI_SILU)]
    slots = [(t, e) for tiles, e in sections for t in tiles]
    steps = []
    for s, k in enumerate(range(0, len(slots), tiles_per_step)):
        chunk = slots[k:k + tiles_per_step]
        steps.append(([t for t, _ in chunk], [e for _, e in chunk], s))
    tiles, epis, _ = steps[-1]
    if len(tiles) < tiles_per_step:
        pad = tiles_per_step - len(tiles)
        tiles += [KV_TILE] * pad
        epis += [EPI_KV] + [None] * (pad - 1)
    else:
        mid = len(steps) // 2
        steps.insert(mid, ([KV_TILE] * tiles_per_step, [EPI_KV] + [None] * (tiles_per_step - 1),
                           steps[mid - 1][2]))
    return [(tuple(t), tuple(e), z) for t, e, z in steps]


def _lookup(values, j):
    out = jnp.int32(values[-1])
    for k in range(len(values) - 2, -1, -1):
        out = jnp.where(j == k, jnp.int32(values[k]), out)
    return out


IN_PROJ_ROWS = 1024
VMEM_LIMIT = 56 * 1024 * 1024
VMEM_LIMIT_IN_PROJ = 62 * 1024 * 1024


def _gelu(x):
    return 0.5 * x * (1.0 + lax.erf(x * np.float32(1.0 / np.sqrt(2.0))))


def _sigmoid(x):
    return 1.0 / (1.0 + jnp.exp(-x))


def _rope(x, cos, sin_signed):
    width = x.shape[1]
    lane = lax.broadcasted_iota(jnp.int32, x.shape, 1)
    first_half = (lane & (HEAD_DIM - 1)) < (HEAD_DIM // 2)
    partner = jnp.where(first_half,
                        pltpu.roll(x, width - HEAD_DIM // 2, axis=1),
                        pltpu.roll(x, HEAD_DIM // 2, axis=1))
    reps = width // LANES
    cos_w = jnp.concatenate([cos] * reps, axis=1) if reps > 1 else cos
    sin_w = jnp.concatenate([sin_signed] * reps, axis=1) if reps > 1 else sin_signed
    return x * cos_w + partner * sin_w


def _dup_heads(x):
    lane = lax.broadcasted_iota(jnp.int32, (x.shape[0], LANES), 1)
    low = lane < HEAD_DIM
    out = []
    for c in range(x.shape[1] // LANES):
        xc = x[:, c * LANES:(c + 1) * LANES]
        sw = pltpu.roll(xc, HEAD_DIM, axis=1)
        out += [jnp.where(low, xc, sw), jnp.where(low, sw, xc)]
    return jnp.concatenate(out, axis=1)


def _in_proj_kernel(*refs, table_rows, sub, walk, x_parts, n_cast, emit_bf16, windows_only):
    n_w = len(walk[0][0])
    refs = list(refs)
    take = lambda k: [refs.pop(0) for _ in range(k)]
    x_refs = take(x_parts)
    (g_ref,) = take(1)
    w_refs = take(n_w)
    cos_ref, sin_ref = take(2)
    cast_in = take(n_cast)
    z_ref, kvd_ref = take(2)
    kv32_refs = take(2 if windows_only else 1)
    cast_out = take(n_cast)
    wb_refs = take(n_w if emit_bf16 else 0)
    (h_scr,) = refs
    j = pl.program_id(1)
    part_rows = x_refs[0].shape[0]
    tm = part_rows * x_parts

    def tables(rs):
        if table_rows == tm:
            return cos_ref[rs, :], sin_ref[rs, :]
        tile = lambda t: jnp.broadcast_to(t[None], (sub // table_rows, table_rows, LANES)).reshape(sub, LANES)
        return tile(cos_ref[...]), tile(sin_ref[...])

    def z_store(t, rs, val):
        col = t * COL_TILE
        if len(z_ref.shape) == 3:
            z_ref[col // Z_BLOCK, rs, col % Z_BLOCK:col % Z_BLOCK + COL_TILE] = val
        else:
            z_ref[rs, col:col + COL_TILE] = val

    def kv_store(acc, rs):
        cos, sin = tables(rs)
        k = _rope(acc[:, :KV_WIDTH], cos, sin)
        v = acc[:, KV_WIDTH:]
        if windows_only:
            lo = tm - WINDOW - rs.start
            if 0 <= lo and lo + WINDOW <= sub:
                kv32_refs[0][...] = k[lo:lo + WINDOW, :].T
                kv32_refs[1][...] = v[lo:lo + WINDOW, :].T
        else:
            kv32_refs[0][rs, :KV_WIDTH] = k
            kv32_refs[0][rs, KV_WIDTH:] = v
        kvd_ref[rs, :KVD_WIDTH // 2] = _dup_heads(k).astype(BF16)
        kvd_ref[rs, KVD_WIDTH // 2:] = _dup_heads(v).astype(BF16)

    z_value = {
        EPI_GATE: lambda acc, rs: _sigmoid(acc),
        EPI_GELU: lambda acc, rs: _gelu(acc),
        EPI_SILU: lambda acc, rs: acc * _sigmoid(acc),
        EPI_Q: lambda acc, rs: _rope(acc, *tables(rs)) * np.float32(HEAD_DIM ** -0.5),
    }

    def run(epis, norm):
        if emit_bf16:
            for t, epi in enumerate(epis):
                if epi is not None:
                    wb_refs[t][...] = w_refs[t][...].astype(BF16)
            weight = lambda t: wb_refs[t][...]
        else:
            weight = lambda t: w_refs[t][...]
        stores_z = any(epi in z_value for epi in epis)
        for r in range(tm // sub):
            rs = slice(r * sub, (r + 1) * sub)
            if norm:
                start = (r * sub) % part_rows
                x = x_refs[(r * sub) // part_rows][start:start + sub, :]
                ms = jnp.mean(x * x, axis=-1, keepdims=True)
                h = (x * lax.rsqrt(ms + EPS) * g_ref[...]).astype(BF16)
                h_scr[rs, :] = h
            else:
                h = h_scr[rs, :]
            for t, epi in enumerate(epis):
                if epi in z_value:
                    acc = jnp.dot(h, weight(t), preferred_element_type=F32)
                    z_store(t, rs, z_value[epi](acc, rs).astype(BF16))
                else:
                    if epi == EPI_KV:
                        kv_store(jnp.dot(h, weight(t), preferred_element_type=F32), rs)
                    if stores_z:
                        z_store(t, rs, jnp.zeros((sub, COL_TILE), BF16))
        if EPI_KV in epis:
            for src, dst in zip(cast_in, cast_out):
                dst[...] = src[...].astype(BF16)

    kinds = []
    for s, (_, epis, _) in enumerate(walk):
        if (epis, s == 0) not in kinds:
            kinds.append((epis, s == 0))
    kind = _lookup(tuple(kinds.index((epis, s == 0)) for s, (_, epis, _) in enumerate(walk)), j)
    for k, (epis, norm) in enumerate(kinds):
        pl.when(kind == k)(functools.partial(run, epis, norm))


def _in_proj(x2d, g_pre, w_in, cos_t, sin_t, table_rows, tiles_per_step, sub, x_parts, cast_along=(),
             emit_bf16=False, seq_rows=None):
    n = x2d.shape[0]
    tm = IN_PROJ_ROWS
    walk = _walk(tiles_per_step)
    n_steps = len(walk)
    if table_rows == tm:
        n_tab = cos_t.shape[0] // tm
        table_spec = pl.BlockSpec((tm, LANES), lambda i, j: (i % n_tab, 0))
    else:
        table_spec = pl.BlockSpec((table_rows, LANES), lambda i, j: (0, 0))
    w_tile = lambda j, t: _lookup(tuple(tiles[t] for tiles, _, _ in walk), j)
    tile_specs = [pl.BlockSpec((None, D_MODEL, COL_TILE), lambda i, j, t=t: (w_tile(j, t), 0, 0))
                  for t in range(tiles_per_step)]
    if emit_bf16:
        assert tiles_per_step == 1 and n == tm
        w_specs = [pl.BlockSpec((D_MODEL, COL_TILE), lambda i, j: (0, w_tile(j, 0)))]
        wb_specs, wb_shapes = tile_specs, [jax.ShapeDtypeStruct((N_COL_TILES, D_MODEL, COL_TILE), BF16)]
    else:
        w_specs, wb_specs, wb_shapes = tile_specs, [], []
    z_step = lambda j: _lookup(tuple(z for _, _, z in walk), j)
    n_z_steps = walk[-1][2] + 1
    step_cols = tiles_per_step * COL_TILE
    if step_cols <= Z_BLOCK:
        steps_per_block = Z_BLOCK // step_cols
        z_blocks = n_z_steps // steps_per_block
        z_spec = pl.BlockSpec((None, tm, step_cols),
                              lambda i, j: (z_step(j) // steps_per_block, i, z_step(j) % steps_per_block))
    else:
        z_blocks = n_z_steps * (step_cols // Z_BLOCK)
        z_spec = pl.BlockSpec((step_cols // Z_BLOCK, tm, Z_BLOCK), lambda i, j: (z_step(j), i, 0))
    assert z_blocks * Z_BLOCK >= Z_COLS

    n_row_tiles = n // tm
    assert sub <= tm // x_parts and x_parts < n_steps

    def x_index(i, j, p):
        tile = jnp.minimum(i + (j >= n_steps - p).astype(jnp.int32), n_row_tiles - 1)
        return (tile * x_parts + p, 0)

    x_specs = [pl.BlockSpec((tm // x_parts, D_MODEL), functools.partial(x_index, p=p))
               for p in range(x_parts)]
    cast_specs = [pl.BlockSpec((w.shape[0] // n_row_tiles, w.shape[1]), lambda i, j: (i, 0)) for w in cast_along]
    if seq_rows is None:
        kv32_specs = [pl.BlockSpec((tm, 2 * KV_WIDTH), lambda i, j: (i, 0))]
        kv32_shapes = [jax.ShapeDtypeStruct((n, 2 * KV_WIDTH), F32)]
    else:
        assert seq_rows % tm == 0 and (tm - WINDOW) // sub == (tm - 1) // sub
        kv32_specs = [pl.BlockSpec((None, KV_WIDTH, WINDOW), lambda i, j: (i // (seq_rows // tm), 0, 0))] * 2
        kv32_shapes = [jax.ShapeDtypeStruct((n // seq_rows, KV_WIDTH, WINDOW), F32)] * 2
    n_main = 2 + len(kv32_specs)
    outs = pl.pallas_call(
        functools.partial(_in_proj_kernel, table_rows=table_rows, sub=sub, walk=walk, x_parts=x_parts,
                          n_cast=len(cast_along), emit_bf16=emit_bf16, windows_only=seq_rows is not None),
        grid=(n_row_tiles, n_steps),
        in_specs=[
            *x_specs,
            pl.BlockSpec((1, D_MODEL), lambda i, j: (0, 0)),
            *w_specs,
            table_spec, table_spec,
            *cast_specs,
        ],
        out_specs=[
            z_spec,
            pl.BlockSpec((tm, KVD_WIDTH), lambda i, j: (i, 0)),
            *kv32_specs,
            *cast_specs,
            *wb_specs,
        ],
        out_shape=[
            jax.ShapeDtypeStruct((z_blocks, n, Z_BLOCK), BF16),
            jax.ShapeDtypeStruct((n, KVD_WIDTH), BF16),
            *kv32_shapes,
            *[jax.ShapeDtypeStruct(w.shape, BF16) for w in cast_along],
            *wb_shapes,
        ],
        scratch_shapes=[pltpu.VMEM((tm, D_MODEL), BF16)],
        compiler_params=pltpu.CompilerParams(
            dimension_semantics=("arbitrary", "arbitrary"), vmem_limit_bytes=VMEM_LIMIT_IN_PROJ),
        name="in_proj",
    )(*([x2d] * x_parts), g_pre, *([w_in] * tiles_per_step), cos_t, sin_t, *cast_along)
    return outs[:n_main], outs[n_main:]


def _layernorm_v(v_ref, lng_ref, lnb_ref):
    vg = v_ref[...].astype(F32)
    mu = jnp.mean(vg, axis=-1, keepdims=True)
    xc = vg - mu
    var = jnp.mean(xc * xc, axis=-1, keepdims=True)
    return xc * lax.rsqrt(var + EPS) * lng_ref[...] + lnb_ref[...]


def _spatial_block(u, v, za, lng, lnb, ws_ref, bs_ref, store_a):
    row = lax.broadcasted_iota(jnp.int32, (CHUNK, CHUNK), 0)
    col = lax.broadcasted_iota(jnp.int32, (CHUNK, CHUNK), 1)
    vg = v.astype(F32)
    mu = jnp.mean(vg, axis=-1, keepdims=True)
    xc = vg - mu
    var = jnp.mean(xc * xc, axis=-1, keepdims=True)
    vn_b = (xc * lax.rsqrt(var + EPS) * lng + lnb).astype(BF16)
    for g in range(A_GROUPS):
        cols = slice(g * LANES, (g + 1) * LANES)
        w = jnp.where(col <= row, ws_ref[g], 0.0).astype(BF16)
        s = jnp.dot(w, vn_b[:, cols], preferred_element_type=F32) + bs_ref[:, cols]
        store_a(cols, ((u[:, cols].astype(F32) * s) * za[:, cols].astype(F32)).astype(BF16))


def _attention_block(q, zb, kvd, kvd_prev, first_block, sinks_ref, store_b):
    t = lax.broadcasted_iota(jnp.int32, (WINDOW, 2 * WINDOW), 0)
    jj = lax.broadcasted_iota(jnp.int32, (WINDOW, 2 * WINDOW), 1)
    mask = (jj > t) & (jj <= t + WINDOW) & ((jj >= WINDOW) | jnp.logical_not(first_block))
    lane = lax.broadcasted_iota(jnp.int32, (WINDOW, LANES), 1)
    low_half = lane < HEAD_DIM
    operands = {}
    for kh in range(N_KV_HEADS):
        kcols = slice(kh * LANES, (kh + 1) * LANES)
        vcols = slice(KVD_WIDTH // 2 + kh * LANES, KVD_WIDTH // 2 + (kh + 1) * LANES)
        operands[kh] = (jnp.concatenate([kvd_prev[:, kcols], kvd[:, kcols]], axis=0),
                        jnp.concatenate([kvd_prev[:, vcols], kvd[:, vcols]], axis=0))
    for m in range(N_HEADS // 2):
        cols = slice(m * LANES, (m + 1) * LANES)
        k_c, v_c = operands[m // 2]
        q_m = q[:, cols]
        outs = []
        for half in range(2):
            sink = sinks_ref[2 * m + half]
            q_h = jnp.where(low_half if half == 0 else ~low_half, q_m, jnp.zeros_like(q_m))
            s = lax.dot_general(q_h, k_c, (((1,), (1,)), ((), ())), preferred_element_type=F32)
            s = jnp.where(mask, s, -jnp.inf)
            mx = jnp.maximum(jnp.max(s, axis=-1, keepdims=True), sink)
            p = jnp.exp(s - mx)
            denom = jnp.sum(p, axis=-1, keepdims=True) + jnp.exp(sink - mx)
            o = jnp.dot(p.astype(BF16), v_c, preferred_element_type=F32)
            outs.append(o * (1.0 / denom))
        o_m = jnp.where(low_half, outs[0], outs[1])
        store_b(cols, (o_m * zb[:, cols].astype(F32)).astype(BF16))


def _merge(x, a, b, ga_ref, gb_ref, wa_ref, wb_ref, wo_ref, g_ref):
    gate = lambda ref: jnp.concatenate([ref[c] for c in range(ref.shape[0])], axis=1).astype(F32)
    pa = jnp.dot(a, wa_ref[...], preferred_element_type=F32)
    pb = jnp.dot(b, wb_ref[...], preferred_element_type=F32)
    merged = gate(ga_ref) * pa + gate(gb_ref) * pb
    out = jnp.dot(merged.astype(BF16), wo_ref[...], preferred_element_type=F32)
    ms = jnp.mean(out * out, axis=-1, keepdims=True)
    return x + out * lax.rsqrt(ms + EPS) * g_ref[...]


MIX_ROWS = 256
MIX_BLOCKS = MIX_ROWS // CHUNK


def _prompt_mix_merge_kernel(u_ref, v_ref, za_ref, q_ref, zb_ref, kvd_ref, kvdp_ref, x_ref, ga_ref,
                             gb_ref, lng_ref, lnb_ref, ws_ref, bs_ref, sinks_ref, wa_ref, wb_ref,
                             wo_ref, g_ref, y_ref, a_scr, b_scr):
    i = pl.program_id(0)
    last_tile = pl.num_programs(0) - 2
    tile = jnp.minimum(i, last_tile)
    slot = i % 2

    @pl.when(i == 0)
    def _():
        a_scr[1] = jnp.zeros(a_scr.shape[1:], BF16)
        b_scr[1] = jnp.zeros(b_scr.shape[1:], BF16)

    def attention(blk):
        rows = slice(blk * CHUNK, (blk + 1) * CHUNK)
        first_block = ((tile * MIX_BLOCKS + blk) % (SEQ // WINDOW)) == 0
        kvd_prev = kvdp_ref[...] if blk == 0 else kvd_ref[(blk - 1) * CHUNK:blk * CHUNK, :]

        def store_b(cols, val):
            b_scr[slot, rows, cols] = val

        _attention_block(q_ref[rows, :], zb_ref[rows, :], kvd_ref[rows, :], kvd_prev, first_block,
                         sinks_ref, store_b)

    def spatial(blk):
        rows = slice(blk * CHUNK, (blk + 1) * CHUNK)

        def store_a(cols, val):
            a_scr[slot, rows, cols] = val

        _spatial_block(u_ref[rows, :], v_ref[rows, :], za_ref[rows, :], lng_ref[...], lnb_ref[...],
                       ws_ref, bs_ref, store_a)

    for blk in range(MIX_BLOCKS):
        attention(blk)
    y_ref[...] = _merge(x_ref[...], a_scr[1 - slot], b_scr[1 - slot], ga_ref, gb_ref,
                        wa_ref, wb_ref, wo_ref, g_ref)
    for blk in range(MIX_BLOCKS):
        spatial(blk)


def _prompt_mix_merge(x2d, z, kvd, ln_g, ln_b, w_s, b_s_rows, sinks, w_a, w_b, w_o, g_post):
    n = x2d.shape[0]
    tm = MIX_ROWS
    n_tiles = n // tm
    mix_tile = lambda i: jnp.minimum(i, n_tiles - 1)
    merge_tile = lambda i: jnp.maximum(i - 1, 0)
    wide = lambda cb: pl.BlockSpec((None, tm, Z_BLOCK), lambda i, cb=cb: (cb, mix_tile(i), 0))
    gate = lambda cb: pl.BlockSpec((D_MODEL // Z_BLOCK, tm, Z_BLOCK),
                                   lambda i, cb=cb: (cb // (D_MODEL // Z_BLOCK), merge_tile(i), 0))
    const2 = lambda shape: pl.BlockSpec(shape, lambda i: (0, 0))
    resident = lambda shape: pl.BlockSpec(shape, lambda i: (0, 0), pipeline_mode=pl.Buffered(1))
    return pl.pallas_call(
        _prompt_mix_merge_kernel,
        grid=(n_tiles + 1,),
        in_specs=[
            wide(COL_U), wide(COL_V), wide(COL_ZA), wide(COL_Q), wide(COL_ZB),
            pl.BlockSpec((tm, KVD_WIDTH), lambda i: (mix_tile(i), 0)),
            pl.BlockSpec((CHUNK, KVD_WIDTH), lambda i: (jnp.maximum(mix_tile(i) * MIX_BLOCKS - 1, 0), 0)),
            pl.BlockSpec((tm, D_MODEL), lambda i: (merge_tile(i), 0)),
            gate(COL_GA), gate(COL_GB),
            const2((1, A_WIDTH)), const2((1, A_WIDTH)),
            pl.BlockSpec((A_GROUPS, CHUNK, CHUNK), lambda i: (0, 0, 0)),
            const2((CHUNK, A_WIDTH)),
            pl.BlockSpec(memory_space=pltpu.SMEM),
            resident((A_WIDTH, D_MODEL)),
            resident((B_WIDTH, D_MODEL)),
            resident((D_MODEL, D_MODEL)),
            resident((1, D_MODEL)),
        ],
        out_specs=pl.BlockSpec((tm, D_MODEL), lambda i: (merge_tile(i), 0)),
        out_shape=jax.ShapeDtypeStruct((n, D_MODEL), F32),
        scratch_shapes=[pltpu.VMEM((2, tm, A_WIDTH), BF16), pltpu.VMEM((2, tm, B_WIDTH), BF16)],
        compiler_params=pltpu.CompilerParams(
            dimension_semantics=("arbitrary",), vmem_limit_bytes=VMEM_LIMIT),
        name="prompt_mix_merge",
    )(z, z, z, z, z, kvd, kvd, x2d, z, z, ln_g, ln_b, w_s, b_s_rows, sinks, w_a, w_b, w_o, g_post)


SEQS_PER_STEP = CHUNK // DEC_SEQ


def _sample_mix_merge_kernel(u_ref, v_ref, za_ref, q_ref, zb_ref, kvd_ref, kv32_ref, ckt_ref, cvt_ref,
                             lng_ref, lnb_ref, coef_ref, bs_ref, sinks_ref, x_ref, ga_ref, gb_ref,
                             wa_ref, wb_ref, wo_ref, g_ref,
                             vn_ref, kwin_ref, vwin_ref, y_ref, a_scr, b_scr):
    i = pl.program_id(0)
    slot = i % 2

    @pl.when(i == 0)
    def _():
        a_scr[1] = jnp.zeros(a_scr.shape[1:], BF16)
        b_scr[1] = jnp.zeros(b_scr.shape[1:], BF16)

    ns = SEQS_PER_STEP
    split = lambda x: x.reshape(ns, DEC_SEQ, x.shape[-1])

    vn = _layernorm_v(v_ref, lng_ref, lnb_ref)
    vn_ref[...] = vn
    vn3 = split(vn)
    t_row = lax.broadcasted_iota(jnp.int32, (DEC_SEQ, A_WIDTH), 0)
    s_acc = jnp.broadcast_to(bs_ref[...][None], (ns, DEC_SEQ, A_WIDTH))
    for s in range(DEC_SEQ):
        coef = jnp.where(t_row >= s, coef_ref[s], 0.0)
        s_acc = s_acc + vn3[:, s:s + 1, :] * coef[None]
    a3 = split(u_ref[...].astype(F32)) * s_acc * split(za_ref[...].astype(F32))
    a_scr[slot] = a3.reshape(CHUNK, A_WIDTH).astype(BF16)

    rows_q = Q_PER_KV * DEC_SEQ
    lane3 = lax.broadcasted_iota(jnp.int32, (ns, DEC_SEQ, LANES), 2)
    low3 = lane3 < HEAD_DIM
    r_idx = lax.broadcasted_iota(jnp.int32, (ns, rows_q, LANES), 1)
    l_idx = lax.broadcasted_iota(jnp.int32, (ns, rows_q, LANES), 2)
    s_idx = lax.broadcasted_iota(jnp.int32, (ns, rows_q, LANES), 0)
    t_q = r_idx & (DEC_SEQ - 1)
    mask_cache = l_idx > t_q
    mask_new = ((l_idx >> 3) == s_idx) & ((l_idx & (DEC_SEQ - 1)) <= t_q)
    head_of_row = lax.broadcasted_iota(jnp.int32, (1, rows_q, 1), 1) >> 3
    q3 = split(q_ref[...].astype(F32))
    zb3 = split(zb_ref[...].astype(F32))
    for kh in range(N_KV_HEADS):
        pieces = []
        for m in (2 * kh, 2 * kh + 1):
            q_m = q3[:, :, m * LANES:(m + 1) * LANES]
            pieces += [jnp.where(low3, q_m, 0.0), jnp.where(low3, 0.0, q_m)]
        lhs = jnp.concatenate(pieces, axis=1).astype(BF16)
        rows = slice(kh * HEAD_DIM, (kh + 1) * HEAD_DIM)
        kt = ckt_ref[:, rows, :].astype(BF16)
        vt = cvt_ref[:, rows, :].astype(BF16)
        kt2 = jnp.concatenate([kt, kt], axis=1)
        vt2 = jnp.concatenate([vt, vt], axis=1)
        k_new = kvd_ref[:, kh * LANES:(kh + 1) * LANES]
        v_new = kvd_ref[:, KVD_WIDTH // 2 + kh * LANES:KVD_WIDTH // 2 + (kh + 1) * LANES]
        s_c = lax.dot_general(lhs, kt2, (((2,), (1,)), ((0,), (0,))), preferred_element_type=F32)
        s_n = lax.dot_general(lhs.reshape(ns * rows_q, LANES), k_new, (((1,), (1,)), ((), ())),
                              preferred_element_type=F32).reshape(ns, rows_q, LANES)
        s_c = jnp.where(mask_cache, s_c, -jnp.inf)
        s_n = jnp.where(mask_new, s_n, -jnp.inf)
        sink = jnp.zeros((1, rows_q, 1), F32)
        for jq in range(Q_PER_KV):
            sink = jnp.where(head_of_row == jq, sinks_ref[Q_PER_KV * kh + jq], sink)
        mx = jnp.maximum(jnp.maximum(jnp.max(s_c, axis=-1, keepdims=True),
                                     jnp.max(s_n, axis=-1, keepdims=True)), sink)
        p_c = jnp.exp(s_c - mx)
        p_n = jnp.exp(s_n - mx)
        denom = (jnp.sum(p_c, axis=-1, keepdims=True) + jnp.sum(p_n, axis=-1, keepdims=True)
                 + jnp.exp(sink - mx))
        o = lax.dot_general(p_c.astype(BF16), vt2, (((2,), (2,)), ((0,), (0,))),
                            preferred_element_type=F32)
        o = o + jnp.dot(p_n.reshape(ns * rows_q, LANES).astype(BF16), v_new,
                        preferred_element_type=F32).reshape(ns, rows_q, LANES)
        o = o * (1.0 / denom)
        for i, m in enumerate((2 * kh, 2 * kh + 1)):
            base = 2 * DEC_SEQ * i
            o_m = jnp.where(low3, o[:, base:base + DEC_SEQ, :], o[:, base + DEC_SEQ:base + 2 * DEC_SEQ, :])
            cols = slice(m * LANES, (m + 1) * LANES)
            b_scr[slot, :, cols] = (o_m * zb3[:, :, cols]).reshape(CHUNK, LANES).astype(BF16)

    y_ref[...] = _merge(x_ref[...], a_scr[1 - slot], b_scr[1 - slot], ga_ref, gb_ref,
                        wa_ref, wb_ref, wo_ref, g_ref)

    keep = WINDOW - DEC_SEQ
    lane_w = lax.broadcasted_iota(jnp.int32, (KV_WIDTH, WINDOW), 1)
    for new_rows, cache_ref, out_ref in ((kv32_ref[:, :KV_WIDTH], ckt_ref, kwin_ref),
                                         (kv32_ref[:, KV_WIDTH:], cvt_ref, vwin_ref)):
        new_t = new_rows.T
        old = pltpu.roll(cache_ref[...].reshape(ns * KV_WIDTH, WINDOW), keep, axis=1)
        for s in range(ns):
            shifted_new = pltpu.roll(new_t, (keep - DEC_SEQ * s) % WINDOW, axis=1)
            out_ref[s] = jnp.where(lane_w < keep, old[s * KV_WIDTH:(s + 1) * KV_WIDTH], shifted_new)


def _sample_mix_merge(x2d, z, kvd, kv32, cache_kt, cache_vt, ln_g, ln_b, coef, b_s8, sinks,
                      w_a, w_b, w_o, g_post):
    n = x2d.shape[0]
    n_blocks = n // CHUNK
    n_seq = cache_kt.shape[0]
    mix_blk = lambda i: jnp.minimum(i, n_blocks - 1)
    merge_blk = lambda i: jnp.maximum(i - 1, 0)
    wide = lambda cb: pl.BlockSpec((None, CHUNK, Z_BLOCK), lambda i, cb=cb: (cb, mix_blk(i), 0))
    gate = lambda cb: pl.BlockSpec((D_MODEL // Z_BLOCK, CHUNK, Z_BLOCK),
                                   lambda i, cb=cb: (cb // (D_MODEL // Z_BLOCK), merge_blk(i), 0))
    const2 = lambda shape: pl.BlockSpec(shape, lambda i: (0, 0))
    resident = lambda shape: pl.BlockSpec(shape, lambda i: (0, 0), pipeline_mode=pl.Buffered(1))
    win = pl.BlockSpec((SEQS_PER_STEP, KV_WIDTH, WINDOW), lambda i: (mix_blk(i), 0, 0))
    return pl.pallas_call(
        _sample_mix_merge_kernel,
        grid=(n_blocks + 1,),
        in_specs=[
            wide(COL_U), wide(COL_V), wide(COL_ZA), wide(COL_Q), wide(COL_ZB),
            pl.BlockSpec((CHUNK, KVD_WIDTH), lambda i: (mix_blk(i), 0)),
            pl.BlockSpec((CHUNK, 2 * KV_WIDTH), lambda i: (mix_blk(i), 0)),
            win, win,
            const2((1, A_WIDTH)), const2((1, A_WIDTH)),
            pl.BlockSpec((DEC_SEQ, DEC_SEQ, A_WIDTH), lambda i: (0, 0, 0)),
            const2((DEC_SEQ, A_WIDTH)),
            pl.BlockSpec(memory_space=pltpu.SMEM),
            pl.BlockSpec((CHUNK, D_MODEL), lambda i: (merge_blk(i), 0)),
            gate(COL_GA), gate(COL_GB),
            resident((A_WIDTH, D_MODEL)),
            resident((B_WIDTH, D_MODEL)),
            resident((D_MODEL, D_MODEL)),
            resident((1, D_MODEL)),
        ],
        out_specs=[
            pl.BlockSpec((CHUNK, A_WIDTH), lambda i: (mix_blk(i), 0)),
            win, win,
            pl.BlockSpec((CHUNK, D_MODEL), lambda i: (merge_blk(i), 0)),
        ],
        out_shape=[
            jax.ShapeDtypeStruct((n, A_WIDTH), F32),
            jax.ShapeDtypeStruct((n_seq, KV_WIDTH, WINDOW), F32),
            jax.ShapeDtypeStruct((n_seq, KV_WIDTH, WINDOW), F32),
            jax.ShapeDtypeStruct((n, D_MODEL), F32),
        ],
        scratch_shapes=[pltpu.VMEM((2, CHUNK, A_WIDTH), BF16), pltpu.VMEM((2, CHUNK, B_WIDTH), BF16)],
        compiler_params=pltpu.CompilerParams(
            dimension_semantics=("arbitrary",), vmem_limit_bytes=VMEM_LIMIT),
        name="sample_mix_merge",
    )(z, z, z, z, z, kvd, kv32, cache_kt, cache_vt, ln_g, ln_b, coef, b_s8, sinks,
      x2d, z, z, w_a, w_b, w_o, g_post)


def _rope_tables(pos):
    lane = jnp.arange(LANES)
    inv = ROPE_THETA ** (-(2 * (lane % (HEAD_DIM // 2))).astype(F32) / HEAD_DIM)
    ang = pos.astype(F32)[:, None] * inv[None, :]
    sign = jnp.where((lane % HEAD_DIM) < HEAD_DIM // 2, -1.0, 1.0).astype(F32)
    return jnp.cos(ang), jnp.sin(ang) * sign[None, :]


def _window_first(win):
    n = win.shape[1]
    return jnp.transpose(win[0], (0, 2, 3, 1)).reshape(n, KV_WIDTH, WINDOW)


def _window_last(win_t):
    n = win_t.shape[0]
    return jnp.transpose(win_t.reshape(n, N_KV_HEADS, HEAD_DIM, WINDOW), (0, 3, 1, 2))[None]


def kernel(x_prompt, x_sample, cache_k_win, cache_v_win, g_pre, w_in, ln_v_g, ln_v_b, w_spatial,
           b_spatial, sinks, w_proj_a, w_proj_b, w_out, g_post):
    bsz, seq, _ = x_prompt.shape
    dbsz, dseq, _ = x_sample.shape
    assert seq == SEQ and dseq == DEC_SEQ and seq % IN_PROJ_ROWS == 0
    assert (dbsz * dseq) % IN_PROJ_ROWS == 0 and g_pre.shape[0] == 1

    b_s_rows = jnp.repeat(b_spatial[0].T, LANES, axis=1)
    coef = jnp.repeat(jnp.transpose(w_spatial[0][:, :DEC_SEQ, :DEC_SEQ], (2, 1, 0)), LANES, axis=2)
    cos_p, sin_p = _rope_tables(jnp.arange(SEQ))
    cos_s, sin_s = _rope_tables(PAST_LEN + jnp.arange(dseq))

    xs = x_sample.reshape(dbsz * dseq, D_MODEL)
    (z_s, kvd_s, kv32_s), (w_in_b,) = _in_proj(xs, g_pre, w_in[0], cos_s, sin_s, dseq, 1, 256, 1,
                                               emit_bf16=True)

    xp = x_prompt.reshape(bsz * seq, D_MODEL)
    (z_p, kvd_p, k_win_pt, v_win_pt), (w_a, w_b, w_o) = _in_proj(
        xp, g_pre, w_in_b, cos_p, sin_p, IN_PROJ_ROWS, 4, 256, 4, seq_rows=seq,
        cast_along=(w_proj_a[0], w_proj_b[0], w_out[0]))
    y_p = _prompt_mix_merge(xp, z_p, kvd_p, ln_v_g, ln_v_b, w_spatial[0], b_s_rows, sinks[0],
                            w_a, w_b, w_o, g_post)

    vn_s, k_win_t, v_win_t, y_s = _sample_mix_merge(
        xs, z_s, kvd_s, kv32_s, _window_first(cache_k_win), _window_first(cache_v_win),
        ln_v_g, ln_v_b, coef, b_s_rows[:dseq], sinks[0], w_a, w_b, w_o, g_post)

    return (y_p.reshape(bsz, seq, D_MODEL),
            y_s.reshape(dbsz, dseq, D_MODEL),
            _window_last(k_win_pt), _window_last(v_win_pt),
            _window_last(k_win_t), _window_last(v_win_t),
            vn_s.reshape(1, dbsz, dseq, A_WIDTH))
```
